```python
import math, functools
import jax, jax.numpy as jnp
from jax import lax
import numpy as np

D_MODEL = 1024
BATCH = 8
SEQ = 2048
DEPTH = 2
DEC_BATCH = 32
DEC_SEQ = 8
PAST_LEN = 8192
PAGE_SIZE = 128

ATT_HEADS = 8
KV_HEADS = 2
HEAD_DIM = 64
ROT_DIMS = HEAD_DIM // 4
ROPE_THETA = 500000.0
IDX_HEADS = 4
IDX_DIM = 32
IDX_ROT = IDX_DIM // 4
TOPK_MAX = 256
Q_BLOCK = 128
S5_GROUPS = 24
S5_GROUP_CH = 16
S5_STATE = 64
S5_WIDTH = S5_GROUPS * S5_GROUP_CH
RET_HEADS = 6
RET_DK = 64
RET_DV = 64
RET_THETA = 10000.0
RET_CHUNK = 64
GDN_HEADS = 6
GDN_DK = 64
GDN_DV = 64
GDN_CONV = 4
GDN_CHUNK = 64
ATT_WIDTH = ATT_HEADS * HEAD_DIM
RET_WIDTH = RET_HEADS * RET_DV
GDN_WIDTH = GDN_HEADS * GDN_DV
BRANCH_WIDTHS = (ATT_WIDTH, S5_WIDTH, RET_WIDTH, GDN_WIDTH)
N_BRANCH = 4
MIX_WIDTH = ATT_WIDTH + S5_WIDTH + RET_WIDTH + GDN_WIDTH
IN_SPLITS = (ATT_WIDTH, KV_HEADS * HEAD_DIM, KV_HEADS * HEAD_DIM,
             IDX_HEADS * IDX_DIM, IDX_DIM, IDX_HEADS,
             S5_WIDTH,
             RET_HEADS * RET_DK, RET_HEADS * RET_DK, RET_WIDTH, RET_WIDTH,
             GDN_HEADS * GDN_DK, GDN_HEADS * GDN_DK, GDN_WIDTH, GDN_HEADS, GDN_HEADS, GDN_WIDTH)
IN_COLS = sum(IN_SPLITS)
N_EXPERTS = 16
N_GROUPS = 4
EXPERTS_PER_GROUP = N_EXPERTS // N_GROUPS
TOP_K = 2
EXPERT_FF = 256
EPS = 1e-6

kernel_name = 'hybrid_dsa_s5_retnet_gdn_moe_step'


def rmsnorm(x, g):
    xf = x.astype(jnp.float32)
    y = xf * lax.rsqrt(jnp.mean(xf * xf, axis=-1, keepdims=True) + EPS)
    return (y * g.astype(jnp.float32)).astype(x.dtype)


def rope(x, pos, rot_dims, theta):
    half = rot_dims // 2
    inv_freq = jnp.power(jnp.float32(theta), -jnp.arange(half, dtype=jnp.float32) / half)
    ang = pos.astype(jnp.float32)[:, None] * inv_freq
    cos = jnp.cos(ang)[:, None, :]
    sin = jnp.sin(ang)[:, None, :]
    xf = x.astype(jnp.float32)
    x1 = xf[..., :half]
    x2 = xf[..., half:rot_dims]
    out = jnp.concatenate([x1 * cos - x2 * sin, x2 * cos + x1 * sin, xf[..., rot_dims:]], axis=-1)
    return out.astype(x.dtype)


def split_cols(x):
    cuts = [int(c) for c in np.cumsum(IN_SPLITS)[:-1]]
    return jnp.split(x, cuts, axis=-1)


def dsa_attend(q, iq, iw, qpos, k_all, v_all, ik_all, topk):
    f32 = jnp.float32
    B_, Tq = q.shape[:2]
    L = k_all.shape[1]
    logits = jnp.einsum('bthd,bsd->bths', iq.astype(f32), ik_all.astype(f32)) * IDX_DIM ** -0.5
    score = jnp.einsum('bths,bth->bts', jax.nn.relu(logits), iw.astype(f32))
    key_pos = jnp.arange(L, dtype=jnp.int32)
    admissible = key_pos[None, :] <= qpos[:, None]
    score = jnp.where(admissible[None], score, -jnp.inf)
    _, sel = lax.top_k(score, topk)
    gather = jax.vmap(lambda rows, ix: rows[ix])
    k_sel = gather(k_all, sel).astype(f32)
    v_sel = gather(v_all, sel).astype(f32)
    valid = sel <= qpos[None, :, None]
    qg = q.astype(f32).reshape(B_, Tq, KV_HEADS, ATT_HEADS // KV_HEADS, HEAD_DIM)
    s = jnp.einsum('btgrd,btkgd->btgrk', qg, k_sel) * HEAD_DIM ** -0.5
    s = jnp.where(valid[:, :, None, None, :], s, -jnp.inf)
    p = jax.nn.softmax(s, axis=-1)
    o = jnp.einsum('btgrk,btkgd->btgrd', p, v_sel)
    return o.reshape(B_, Tq, ATT_WIDTH).astype(q.dtype)


def attend_prompt(q, k, v, iq, ik, iw, pos, topk):
    B_, T = q.shape[:2]
    nblk = T // Q_BLOCK

    def to_blocks(a):
        return a.reshape((B_, nblk, Q_BLOCK) + a.shape[2:]).swapaxes(0, 1)

    def one_block(args):
        qb, iqb, iwb, pb = args
        return dsa_attend(qb, iqb, iwb, pb, k, v, ik, topk)

    out = lax.map(one_block, (to_blocks(q), to_blocks(iq), to_blocks(iw), pos.reshape(nblk, Q_BLOCK)))
    return out.swapaxes(0, 1).reshape(B_, T, ATT_WIDTH)


def attend_sample(q, k, v, iq, ik, iw, pos, cache_k, cache_v, cache_ik, page_table, topk):
    DB = q.shape[0]
    k_past = cache_k[page_table].reshape(DB, -1, KV_HEADS, HEAD_DIM)
    v_past = cache_v[page_table].reshape(DB, -1, KV_HEADS, HEAD_DIM)
    ik_past = cache_ik[page_table].reshape(DB, -1, IDX_DIM)
    k_all = jnp.concatenate([k_past, k.astype(k_past.dtype)], axis=1)
    v_all = jnp.concatenate([v_past, v.astype(v_past.dtype)], axis=1)
    ik_all = jnp.concatenate([ik_past, ik.astype(ik_past.dtype)], axis=1)
    return dsa_attend(q, iq, iw, pos, k_all, v_all, ik_all, topk)


def _complex_affine_combine(e1, e2):
    a1r, a1i, b1r, b1i = e1
    a2r, a2i, b2r, b2i = e2
    return (a2r * a1r - a2i * a1i, a2r * a1i + a2i * a1r,
            a2r * b1r - a2i * b1i + b2r, a2r * b1i + a2i * b1r + b2i)


def s5_branch(u, h0_re, h0_im, lw):
    f32 = jnp.float32
    B_, T, _ = u.shape
    uf = u.astype(f32)
    ug = uf.reshape(B_, T, S5_GROUPS, S5_GROUP_CH)
    dt = jnp.exp(lw['s5_log_dt'].astype(f32))[:, None]
    a_re = lw['s5_a_re'].astype(f32)
    a_im = lw['s5_a_im'].astype(f32)
    mag = jnp.exp(dt * a_re)
    abar_re = mag * jnp.cos(dt * a_im)
    abar_im = mag * jnp.sin(dt * a_im)
    den = a_re * a_re + a_im * a_im
    num_re = abar_re - 1.0
    coef_re = (num_re * a_re + abar_im * a_im) / den
    coef_im = (abar_im * a_re - num_re * a_im) / den
    bu_re = jnp.einsum('btgc,gnc->btgn', ug, lw['s5_b_re'].astype(f32))
    bu_im = jnp.einsum('btgc,gnc->btgn', ug, lw['s5_b_im'].astype(f32))
    in_re = coef_re * bu_re - coef_im * bu_im
    in_im = coef_re * bu_im + coef_im * bu_re
    a_r = jnp.broadcast_to(abar_re, in_re.shape)
    a_i = jnp.broadcast_to(abar_im, in_re.shape)
    ac_r, ac_i, bc_r, bc_i = lax.associative_scan(_complex_affine_combine, (a_r, a_i, in_re, in_im), axis=1)
    h0r = h0_re.astype(f32)[:, None]
    h0i = h0_im.astype(f32)[:, None]
    x_re = ac_r * h0r - ac_i * h0i + bc_r
    x_im = ac_r * h0i + ac_i * h0r + bc_i
    y = (jnp.einsum('btgn,gcn->btgc', x_re, lw['s5_c_re'].astype(f32))
         - jnp.einsum('btgn,gcn->btgc', x_im, lw['s5_c_im'].astype(f32)))
    y = y.reshape(B_, T, S5_WIDTH) + lw['s5_d'].astype(f32) * uf
    z = jax.nn.gelu(y)
    out = z * jax.nn.sigmoid(z @ lw['s5_w_glu'].astype(f32) + lw['s5_b_glu'].astype(f32))
    return out.astype(u.dtype), x_re[:, -1], x_im[:, -1]


def retention_scan(q, k, v, S0, log_gamma, chunk):
    B_, T, H, _ = q.shape
    n_chunks = T // chunk
    i = jnp.arange(chunk, dtype=jnp.float32)
    lg = log_gamma[:, None, None]
    rel = i[:, None] - i[None, :]
    dec_ij = jnp.exp(jnp.where(rel[None] >= 0, rel[None] * lg, -jnp.inf))
    q_dec = jnp.exp((i[None, :] + 1.0) * log_gamma[:, None]).T[None, :, :, None]
    k_dec = jnp.exp((chunk - 1.0 - i[None, :]) * log_gamma[:, None]).T[None, :, :, None]
    chunk_dec = jnp.exp(chunk * log_gamma)[None, :, None, None]

    def step(S, xs):
        qc, kc, vc = xs
        inner = jnp.einsum('bihd,bjhd->bhij', qc, kc) * dec_ij
        o = jnp.einsum('bhij,bjhe->bihe', inner, vc) + jnp.einsum('bihd,bhde->bihe', qc * q_dec, S)
        S = S * chunk_dec + jnp.einsum('bjhd,bjhe->bhde', kc * k_dec, vc)
        return S, o

    xs = tuple(a.reshape((B_, n_chunks, chunk) + a.shape[2:]).swapaxes(0, 1) for a in (q, k, v))
    S, o = lax.scan(step, S0, xs)
    return o.swapaxes(0, 1).reshape(B_, T, H, -1), S


def retention_branch(cq, ck, cv, cg, pos, S0):
    f32 = jnp.float32
    B_, T, _ = cq.shape
    q = rope(cq.reshape(B_, T, RET_HEADS, RET_DK), pos, RET_DK, RET_THETA).astype(f32)
    k = rope(ck.reshape(B_, T, RET_HEADS, RET_DK), pos, RET_DK, RET_THETA).astype(f32) * RET_DK ** -0.5
    v = cv.reshape(B_, T, RET_HEADS, RET_DV).astype(f32)
    log_gamma = jnp.log(1.0 - jnp.exp2(-5.0 - jnp.arange(RET_HEADS, dtype=f32)))
    o, S = retention_scan(q, k, v, S0.astype(f32), log_gamma, math.gcd(T, RET_CHUNK))
    mu = jnp.mean(o, axis=-1, keepdims=True)
    var = jnp.mean(jnp.square(o - mu), axis=-1, keepdims=True)
    o = (o - mu) * lax.rsqrt(var + 1e-5)
    y = jax.nn.silu(cg.astype(f32)) * o.reshape(B_, T, RET_WIDTH)
    return y.astype(cq.dtype), S


def causal_dwconv(x, buf, w):
    T = x.shape[1]
    xp = jnp.concatenate([buf.astype(x.dtype), x], axis=1)
    out = xp[:, 0:T] * w[0]
    for i in range(1, GDN_CONV):
        out = out + xp[:, i:i + T] * w[i]
    return jax.nn.silu(out), xp[:, T:]


def l2norm(x):
    return x * lax.rsqrt(jnp.sum(x * x, axis=-1, keepdims=True) + EPS)


def gated_delta_scan(q, k, v, beta, log_alpha, H0, chunk):
    B_, T, H, _ = q.shape
    n_chunks = T // chunk
    idx = jnp.arange(chunk)
    strict = (idx[:, None] > idx[None, :])
    incl = (idx[:, None] >= idx[None, :])
    eye = jnp.eye(chunk, dtype=jnp.float32)

    def step(Hs, xs):
        qc, kc, vc, bc, lac = xs
        gT = jnp.cumsum(lac, axis=1).transpose(0, 2, 1)
        diff = gT[..., :, None] - gT[..., None, :]
        dec_strict = jnp.exp(jnp.where(strict, diff, -jnp.inf))
        dec_incl = jnp.exp(jnp.where(incl, diff, -jnp.inf))
        bT = bc.transpose(0, 2, 1)
        kk = jnp.einsum('bihd,bjhd->bhij', kc, kc)
        amat = eye + bT[..., :, None] * dec_strict * kk
        kH = jnp.einsum('bihd,bhde->bhie', kc, Hs)
        rhs = bT[..., None] * (vc.transpose(0, 2, 1, 3) - jnp.exp(gT)[..., None] * kH)
        U = lax.linalg.triangular_solve(amat, rhs, left_side=True, lower=True, unit_diagonal=True)
        qk = jnp.einsum('bihd,bjhd->bhij', qc, kc) * dec_incl
        o = jnp.exp(gT)[..., None] * jnp.einsum('bihd,bhde->bhie', qc, Hs) + jnp.einsum('bhij,bhje->bhie', qk, U)
        g_last = gT[..., -1:]
        k_w = kc.transpose(0, 2, 1, 3) * jnp.exp(g_last - gT)[..., None]
        Hs = jnp.exp(g_last)[..., None] * Hs + jnp.einsum('bhjd,bhje->bhde', k_w, U)
        return Hs, o.transpose(0, 2, 1, 3)

    xs = tuple(a.reshape((B_, n_chunks, chunk) + a.shape[2:]).swapaxes(0, 1)
               for a in (q, k, v, beta, log_alpha))
    Hs, o = lax.scan(step, H0, xs)
    return o.swapaxes(0, 1).reshape(B_, T, H, -1), Hs


def gdn_branch(dq, dk_, dv_, da, db, dg, H0, conv_buf, lw):
    f32 = jnp.float32
    B_, T, _ = dq.shape
    qkv, new_buf = causal_dwconv(jnp.concatenate([dq, dk_, dv_], axis=-1), conv_buf, lw['gdn_conv_w'])
    q, k, v = jnp.split(qkv.astype(f32), 3, axis=-1)
    q = l2norm(q.reshape(B_, T, GDN_HEADS, GDN_DK)) * GDN_DK ** -0.5
    k = l2norm(k.reshape(B_, T, GDN_HEADS, GDN_DK))
    v = v.reshape(B_, T, GDN_HEADS, GDN_DV)
    beta = jax.nn.sigmoid(db.astype(f32))
    log_alpha = -jnp.exp(lw['gdn_a_log'].astype(f32)) * jax.nn.softplus(da.astype(f32) + lw['gdn_dt_bias'].astype(f32))
    o, Hn = gated_delta_scan(q, k, v, beta, log_alpha, H0.astype(f32), math.gcd(T, GDN_CHUNK))
    o = rmsnorm(o, lw['gdn_norm_g']) * jax.nn.silu(dg.reshape(B_, T, GDN_HEADS, GDN_DV).astype(f32))
    return o.reshape(B_, T, GDN_WIDTH).astype(dq.dtype), Hn, new_buf


def token_mixers(h, pos, st, attend, lw):
    f32 = jnp.float32
    B_, T, _ = h.shape
    (aq, ak, av, aiq, aik, aiw, bu, cq, ck, cv, cg,
     dq, dk_, dv_, da, db, dg) = split_cols(h @ lw['w_in'])
    q = rope(aq.reshape(B_, T, ATT_HEADS, HEAD_DIM), pos, ROT_DIMS, ROPE_THETA)
    k = rope(ak.reshape(B_, T, KV_HEADS, HEAD_DIM), pos, ROT_DIMS, ROPE_THETA)
    v = av.reshape(B_, T, KV_HEADS, HEAD_DIM)
    iq = rope(aiq.reshape(B_, T, IDX_HEADS, IDX_DIM), pos, IDX_ROT, ROPE_THETA)
    ik = rope(aik[:, :, None, :], pos, IDX_ROT, ROPE_THETA)[:, :, 0, :]
    ya = attend(q, k, v, iq, ik, aiw * IDX_HEADS ** -0.5, pos)
    yb, s5_re, s5_im = s5_branch(bu, st['s5_re'], st['s5_im'], lw)
    yc, ret_s = retention_branch(cq, ck, cv, cg, pos, st['ret'])
    yd, gdn_s, conv_s = gdn_branch(dq, dk_, dv_, da, db, dg, st['gdn'], st['conv'], lw)
    gates = jax.nn.sigmoid((h @ lw['w_gate']).astype(f32)).reshape(B_, T, N_BRANCH, D_MODEL)
    offs = [int(o) for o in np.cumsum((0,) + BRANCH_WIDTHS)]
    merged = None
    for b, y_b in enumerate((ya, yb, yc, yd)):
        term = gates[:, :, b] * (y_b @ lw['w_br'][offs[b]:offs[b + 1]]).astype(f32)
        merged = term if merged is None else merged + term
    y = merged.astype(h.dtype) @ lw['w_out']
    new_st = {'k': k, 'v': v, 'ik': ik, 's5_re': s5_re, 's5_im': s5_im,
              'ret': ret_s, 'gdn': gdn_s, 'conv': conv_s}
    return y, new_st


def moe(h, w_router, router_bias, w1, w3, w2):
    f32 = jnp.float32
    scores = jax.nn.sigmoid(jnp.einsum('btd,de->bte', h.astype(f32), w_router.astype(f32)))
    biased = scores + router_bias.astype(f32)
    grouped = biased.reshape(h.shape[:2] + (N_GROUPS, EXPERTS_PER_GROUP))
    group_score = jnp.sum(lax.top_k(grouped, TOP_K)[0], axis=-1)
    best = jnp.argmax(group_score, axis=-1)
    in_group = (jnp.arange(N_EXPERTS) // EXPERTS_PER_GROUP) == best[..., None]
    _, top_e = lax.top_k(jnp.where(in_group, biased, -jnp.inf), TOP_K)
    w_top = jnp.take_along_axis(scores, top_e, axis=-1)
    w_top = w_top / jnp.sum(w_top, axis=-1, keepdims=True)
    comb = jnp.einsum('btke,btk->bte', jax.nn.one_hot(top_e, N_EXPERTS, dtype=f32), w_top)
    hid = jax.nn.silu(jnp.einsum('btd,edf->btef', h, w1)) * jnp.einsum('btd,edf->btef', h, w3)
    hid = hid * comb[..., None].astype(hid.dtype)
    return jnp.einsum('btef,efd->btd', hid, w2)


def trunk_layer(x, c, pos, st, attend, lw):
    mod = jnp.einsum('bd,de->be', jax.nn.silu(c), lw['w_ada']) + lw['b_ada']
    sh1, sc1, g1, sh2, sc2, g2 = [m[:, None, :] for m in jnp.split(mod, 6, axis=-1)]
    h = rmsnorm(x, lw['norm1']) * (1.0 + sc1) + sh1
    y, new_st = token_mixers(h, pos, st, attend, lw)
    x = x + g1 * y
    h = rmsnorm(x, lw['norm2']) * (1.0 + sc2) + sh2
    x = x + g2 * moe(h, lw['w_router'], lw['router_bias'], lw['w_e1'], lw['w_e3'], lw['w_e2'])
    return x, new_st


def setup_inputs(seed: int = 0) -> dict:
    key = jax.random.key(seed)
    kit = iter(jax.random.split(key, 64))
    f32 = jnp.float32

    def normal(shape, scale=1.0):
        return jax.random.normal(next(kit), shape, f32) * scale

    def uniform(shape, lo, hi):
        return jax.random.uniform(next(kit), shape, f32, lo, hi)

    n_pages = PAST_LEN // PAGE_SIZE
    n_used = DEC_BATCH * n_pages
    n_pool = n_used + max(1, n_used // 4)
    page_table = jax.random.permutation(next(kit), n_pool)[:n_used].reshape(DEC_BATCH, n_pages).astype(jnp.int32)
    dt_gdn = jnp.exp(uniform((DEPTH, GDN_HEADS), math.log(1e-3), math.log(1e-1)))
    w_br = jnp.concatenate([normal((DEPTH, w, D_MODEL), w ** -0.5) for w in BRANCH_WIDTHS], axis=1)
    return {
        'x_prompt': normal((BATCH, SEQ, D_MODEL)),
        'x_sample': normal((DEC_BATCH, DEC_SEQ, D_MODEL)),
        'c_prompt': normal((BATCH, D_MODEL)),
        'c_sample': normal((DEC_BATCH, D_MODEL)),
        'cache_k': normal((DEPTH, n_pool, PAGE_SIZE, KV_HEADS, HEAD_DIM)),
        'cache_v': normal((DEPTH, n_pool, PAGE_SIZE, KV_HEADS, HEAD_DIM)),
        'cache_idx_k': normal((DEPTH, n_pool, PAGE_SIZE, IDX_DIM)),
        'page_table': page_table,
        'state_s5_re': normal((DEPTH, DEC_BATCH, S5_GROUPS, S5_STATE), 0.3),
        'state_s5_im': normal((DEPTH, DEC_BATCH, S5_GROUPS, S5_STATE), 0.3),
        'state_ret': normal((DEPTH, DEC_BATCH, RET_HEADS, RET_DK, RET_DV), 0.5),
        'state_gdn': normal((DEPTH, DEC_BATCH, GDN_HEADS, GDN_DK, GDN_DV), 0.3),
        'state_gdn_conv': normal((DEPTH, DEC_BATCH, GDN_CONV - 1, 3 * GDN_WIDTH)),
        'norm1_g': 1.0 + normal((DEPTH, D_MODEL), 0.02),
        'norm2_g': 1.0 + normal((DEPTH, D_MODEL), 0.02),
        'final_g': 1.0 + normal((D_MODEL,), 0.02),
        'w_ada': normal((DEPTH, D_MODEL, 6 * D_MODEL), 0.5 * D_MODEL ** -0.5),
        'b_ada': normal((DEPTH, 6 * D_MODEL), 0.02),
        'w_in': normal((DEPTH, D_MODEL, IN_COLS), D_MODEL ** -0.5),
        's5_a_re': -0.5 + normal((DEPTH, S5_GROUPS, S5_STATE), 0.01),
        's5_a_im': jnp.pi * jnp.arange(S5_STATE, dtype=f32) + normal((DEPTH, S5_GROUPS, S5_STATE), 0.01),
        's5_b_re': normal((DEPTH, S5_GROUPS, S5_STATE, S5_GROUP_CH), (2 * S5_GROUP_CH) ** -0.5),
        's5_b_im': normal((DEPTH, S5_GROUPS, S5_STATE, S5_GROUP_CH), (2 * S5_GROUP_CH) ** -0.5),
        's5_c_re': normal((DEPTH, S5_GROUPS, S5_GROUP_CH, S5_STATE), S5_STATE ** -0.5),
        's5_c_im': normal((DEPTH, S5_GROUPS, S5_GROUP_CH, S5_STATE), S5_STATE ** -0.5),
        's5_d': normal((DEPTH, S5_WIDTH)),
        's5_log_dt': uniform((DEPTH, S5_GROUPS), math.log(1e-3), math.log(1e-1)),
        's5_w_glu': normal((DEPTH, S5_WIDTH, S5_WIDTH), S5_WIDTH ** -0.5),
        's5_b_glu': normal((DEPTH, S5_WIDTH), 0.02),
        'gdn_conv_w': normal((DEPTH, GDN_CONV, 3 * GDN_WIDTH), GDN_CONV ** -0.5),
        'gdn_a_log': jnp.log(uniform((DEPTH, GDN_HEADS), 1.0, 16.0)),
        'gdn_dt_bias': dt_gdn + jnp.log(-jnp.expm1(-dt_gdn)),
        'gdn_norm_g': 1.0 + normal((DEPTH, GDN_DV), 0.02),
        'w_br': w_br,
        'w_gate': normal((DEPTH, D_MODEL, N_BRANCH * D_MODEL), D_MODEL ** -0.5),
        'w_out': normal((DEPTH, D_MODEL, D_MODEL), D_MODEL ** -0.5),
        'w_router': normal((D_MODEL, N_EXPERTS), D_MODEL ** -0.5),
        'router_bias': normal((N_EXPERTS,), 0.01),
        'w_e1': normal((DEPTH, N_EXPERTS, D_MODEL, EXPERT_FF), D_MODEL ** -0.5),
        'w_e3': normal((DEPTH, N_EXPERTS, D_MODEL, EXPERT_FF), D_MODEL ** -0.5),
        'w_e2': normal((DEPTH, N_EXPERTS, EXPERT_FF, D_MODEL), EXPERT_FF ** -0.5),
    }


def reference(x_prompt, x_sample, c_prompt, c_sample, cache_k, cache_v, cache_idx_k, page_table,
              state_s5_re, state_s5_im, state_ret, state_gdn, state_gdn_conv,
              norm1_g, norm2_g, final_g, w_ada, b_ada, w_in,
              s5_a_re, s5_a_im, s5_b_re, s5_b_im, s5_c_re, s5_c_im, s5_d, s5_log_dt, s5_w_glu, s5_b_glu,
              gdn_conv_w, gdn_a_log, gdn_dt_bias, gdn_norm_g,
              w_br, w_gate, w_out, w_router, router_bias, w_e1, w_e3, w_e2):
    f32 = jnp.float32
    pos_p = jnp.arange(SEQ, dtype=jnp.int32)
    pos_s = PAST_LEN + jnp.arange(DEC_SEQ, dtype=jnp.int32)
    topk_p = min(TOPK_MAX, SEQ // 4)
    topk_s = min(TOPK_MAX, (PAST_LEN + DEC_SEQ) // 4)
    zero_st = {'s5_re': jnp.zeros((BATCH, S5_GROUPS, S5_STATE), f32),
               's5_im': jnp.zeros((BATCH, S5_GROUPS, S5_STATE), f32),
               'ret': jnp.zeros((BATCH, RET_HEADS, RET_DK, RET_DV), f32),
               'gdn': jnp.zeros((BATCH, GDN_HEADS, GDN_DK, GDN_DV), f32),
               'conv': jnp.zeros((BATCH, GDN_CONV - 1, 3 * GDN_WIDTH), x_prompt.dtype)}
    xp, xs = x_prompt, x_sample
    outs_p, outs_s = [], []
    for l in range(DEPTH):
        lw = {'norm1': norm1_g[l], 'norm2': norm2_g[l], 'w_ada': w_ada[l], 'b_ada': b_ada[l],
              'w_in': w_in[l],
              's5_a_re': s5_a_re[l], 's5_a_im': s5_a_im[l], 's5_b_re': s5_b_re[l], 's5_b_im': s5_b_im[l],
              's5_c_re': s5_c_re[l], 's5_c_im': s5_c_im[l], 's5_d': s5_d[l], 's5_log_dt': s5_log_dt[l],
              's5_w_glu': s5_w_glu[l], 's5_b_glu': s5_b_glu[l],
              'gdn_conv_w': gdn_conv_w[l], 'gdn_a_log': gdn_a_log[l], 'gdn_dt_bias': gdn_dt_bias[l],
              'gdn_norm_g': gdn_norm_g[l],
              'w_br': w_br[l], 'w_gate': w_gate[l], 'w_out': w_out[l],
              'w_router': w_router, 'router_bias': router_bias,
              'w_e1': w_e1[l], 'w_e3': w_e3[l], 'w_e2': w_e2[l]}
        attend_p = functools.partial(attend_prompt, topk=topk_p)
        attend_s = functools.partial(attend_sample, cache_k=cache_k[l], cache_v=cache_v[l],
                                     cache_ik=cache_idx_k[l], page_table=page_table, topk=topk_s)
        st_s = {'s5_re': state_s5_re[l], 's5_im': state_s5_im[l], 'ret': state_ret[l],
                'gdn': state_gdn[l], 'conv': state_gdn_conv[l]}
        xp, ns_p = trunk_layer(xp, c_prompt, pos_p, zero_st, attend_p, lw)
        xs, ns_s = trunk_layer(xs, c_sample, pos_s, st_s, attend_s, lw)
        outs_p.append(ns_p)
        outs_s.append(ns_s)
    y_prompt = rmsnorm(xp, final_g)
    y_sample = rmsnorm(xs, final_g)

    def stack(outs, name):
        return jnp.stack([o[name] for o in outs], axis=0)

    k_prompt = stack(outs_p, 'k')
    v_prompt = stack(outs_p, 'v')
    idx_k_prompt = stack(outs_p, 'ik')
    k_sample = stack(outs_s, 'k')
    v_sample = stack(outs_s, 'v')
    idx_k_sample = stack(outs_s, 'ik')
    s5_re_prompt = stack(outs_p, 's5_re')
    s5_im_prompt = stack(outs_p, 's5_im')
    s5_re_sample = stack(outs_s, 's5_re')
    s5_im_sample = stack(outs_s, 's5_im')
    ret_prompt = stack(outs_p, 'ret')
    ret_sample = stack(outs_s, 'ret')
    gdn_prompt = stack(outs_p, 'gdn')
    gdn_sample = stack(outs_s, 'gdn')
    conv_prompt = stack(outs_p, 'conv')
    conv_sample = stack(outs_s, 'conv')
    return (y_prompt, y_sample, k_prompt, v_prompt, idx_k_prompt, k_sample, v_sample, idx_k_sample,
            s5_re_prompt, s5_im_prompt, s5_re_sample, s5_im_sample, ret_prompt, ret_sample,
            gdn_prompt, gdn_sample, conv_prompt, conv_sample)
```

```python
import functools
import math

import numpy as np
import jax
import jax.numpy as jnp
from jax import lax
from jax.experimental import pallas as pl
from jax.experimental.pallas import tpu as pltpu

F32 = jnp.float32
BF16 = jnp.bfloat16

D_MODEL = 1024
DEPTH = 2
PAST_LEN = 8192
PAGE_SIZE = 128
ATT_HEADS = 8
KV_HEADS = 2
HEAD_DIM = 64
ROT_DIMS = HEAD_DIM // 4
ROPE_THETA = 500000.0
IDX_HEADS = 4
IDX_DIM = 32
IDX_ROT = IDX_DIM // 4
TOPK_MAX = 256
S5_GROUPS = 24
S5_GROUP_CH = 16
S5_STATE = 64
S5_WIDTH = S5_GROUPS * S5_GROUP_CH
S5_NS = S5_GROUPS * S5_STATE
RET_HEADS = 6
RET_DK = 64
RET_THETA = 10000.0
GDN_HEADS = 6
GDN_DK = 64
GDN_CONV = 4
GDN_CHUNK = 64
ATT_WIDTH = ATT_HEADS * HEAD_DIM
BW = 384
KVW = KV_HEADS * HEAD_DIM
N_BRANCH = 4
MIX_WIDTH = ATT_WIDTH + 3 * BW
N_EXPERTS = 16
N_GROUPS = 4
EXPERTS_PER_GROUP = 4
EXPERT_FF = 256
EPS = 1e-6

PROJ_COLS = 12 * BW
ATT_COL0 = 9 * BW
MISC_DA = 36
MISC_DB = 42
LANE = 128
VMEM_LIMIT = 56 * 1024 * 1024
NEG = -1e30


def _cp(sem):
    return pltpu.CompilerParams(dimension_semantics=sem, vmem_limit_bytes=VMEM_LIMIT)


def _bf(x):
    return x.astype(BF16)


def _dot(a, b):
    return jnp.dot(_bf(a), _bf(b), preferred_element_type=F32)


def _dot_nt(a, b):
    return lax.dot_general(_bf(a), _bf(b), (((1,), (1,)), ((), ())), preferred_element_type=F32)


def _dot_tn(a, b):
    return lax.dot_general(_bf(a), _bf(b), (((0,), (0,)), ((), ())), preferred_element_type=F32)


def _split(x):
    hi = x.astype(BF16)
    lo = (x - hi.astype(F32)).astype(BF16)
    return hi, lo


def _dot3(a, b):
    ah, al = _split(a)
    bh, bl = _split(b)
    d = functools.partial(jnp.dot, preferred_element_type=F32)
    return d(ah, bh) + (d(ah, bl) + d(al, bh))


def _dot2(a, b01):
    ah, al = _split(a)
    d = functools.partial(jnp.dot, preferred_element_type=F32)
    return d(ah, b01) + d(al, b01)


def _sigmoid(x):
    return 1.0 / (1.0 + jnp.exp(-x))


def _silu(x):
    return x * _sigmoid(x)


def _lane_iota(shape):
    return lax.broadcasted_iota(jnp.int32, shape, len(shape) - 1)


def _row_iota(shape):
    return lax.broadcasted_iota(jnp.int32, shape, len(shape) - 2)


def _ada_kernel(c_ref, w_ref, b_ref, o_ref):
    o_ref[...] = _dot(_silu(c_ref[...]), w_ref[...]) + b_ref[...]


def ada_mod(c, w_bf, b):
    n = c.shape[0]
    cols = w_bf.shape[1]
    tn = 1024
    return pl.pallas_call(
        _ada_kernel,
        grid=(cols // tn,),
        in_specs=[pl.BlockSpec((n, D_MODEL), lambda j: (0, 0)),
                  pl.BlockSpec((D_MODEL, tn), lambda j: (0, j)),
                  pl.BlockSpec((1, tn), lambda j: (0, j))],
        out_specs=pl.BlockSpec((n, tn), lambda j: (0, j)),
        out_shape=jax.ShapeDtypeStruct((n, cols), F32),
        compiler_params=_cp(("parallel",)),
        name="ada_mod",
    )(c, w_bf, b.reshape(1, cols))


def _norm_mod(x, g, sc, sh):
    y = x * lax.rsqrt(jnp.mean(x * x, axis=-1, keepdims=True) + EPS) * g
    return y * (1.0 + sc) + sh


def _in_kernel(x_ref, g_ref, sc_ref, sh_ref, w_ref, wm_ref, wt_ref, o_ref, om_ref, ot_ref, h_sc):
    @pl.when(pl.program_id(1) == 0)
    def _():
        h = _norm_mod(x_ref[...], g_ref[...], sc_ref[0], sh_ref[0])
        h_sc[...] = h.astype(BF16)
        hh, hl = _split(h)
        wmh, wml = _split(wm_ref[...])
        d = functools.partial(jnp.dot, preferred_element_type=F32)
        om_ref[...] = d(hh, wmh) + (d(hh, wml) + d(hl, wmh))
        wth, wtl = _split(wt_ref[...])
        nt = functools.partial(lax.dot_general, dimension_numbers=(((1,), (1,)), ((), ())),
                               preferred_element_type=F32)
        ot_ref[...] = nt(wth, hh) + (nt(wth, hl) + nt(wtl, hh))

    o_ref[...] = jnp.dot(h_sc[...], w_ref[...], preferred_element_type=F32)


def _mod_spec(tm, mod_rows, tiles_per_seq):
    if mod_rows == 1:
        return pl.BlockSpec((1, 1, D_MODEL), lambda i, *_: (i // tiles_per_seq, 0, 0))
    return pl.BlockSpec((1, tm, D_MODEL), lambda i, *_: (i, 0, 0))


def in_proj(x, g, sc, sh, w_bf, w_misc, w_iwt, tm, tiles_per_seq):
    m = x.shape[0]
    tn = 3 * BW * 2 if tm <= 256 else 3 * BW
    mod_rows = sc.shape[1]
    ms = _mod_spec(tm, mod_rows, tiles_per_seq)
    return pl.pallas_call(
        _in_kernel,
        grid=(m // tm, PROJ_COLS // tn),
        in_specs=[pl.BlockSpec((tm, D_MODEL), lambda i, j: (i, 0)),
                  pl.BlockSpec((1, D_MODEL), lambda i, j: (0, 0)),
                  ms, ms,
                  pl.BlockSpec((D_MODEL, tn), lambda i, j: (0, j)),
                  pl.BlockSpec((D_MODEL, LANE), lambda i, j: (0, 0)),
                  pl.BlockSpec((8, D_MODEL), lambda i, j: (0, 0))],
        out_specs=[pl.BlockSpec((tm, tn), lambda i, j: (i, j)),
                   pl.BlockSpec((tm, LANE), lambda i, j: (i, 0)),
                   pl.BlockSpec((8, tm), lambda i, j: (0, i))],
        out_shape=[jax.ShapeDtypeStruct((m, PROJ_COLS), F32),
                   jax.ShapeDtypeStruct((m, LANE), F32),
                   jax.ShapeDtypeStruct((8, m), F32)],
        scratch_shapes=[pltpu.VMEM((tm, D_MODEL), BF16)],
        compiler_params=_cp(("parallel", "arbitrary")),
        name="in_proj",
    )(x, g.reshape(1, D_MODEL), sc, sh, w_bf, w_misc, w_iwt)


def _rope_tables(pos, rot_dims, theta, period, width=LANE, active=None):
    half = rot_dims // 2
    inv_freq = jnp.power(jnp.float32(theta), -jnp.arange(half, dtype=F32) / half)
    ang = pos.astype(F32)[:, None] * inv_freq
    cos, sin = jnp.cos(ang), jnp.sin(ang)
    t = pos.shape[0]
    c = jnp.concatenate([cos, cos, jnp.ones((t, period - rot_dims), F32)], axis=1)
    s_up = jnp.concatenate([-sin, jnp.zeros((t, period - half), F32)], axis=1)
    s_dn = jnp.concatenate([jnp.zeros((t, half), F32), sin, jnp.zeros((t, period - rot_dims), F32)], axis=1)
    reps = width // period
    tab = jnp.stack([jnp.tile(a, (1, reps)) for a in (c, s_up, s_dn)], axis=0)
    if active is not None:
        ident = jnp.stack([jnp.ones((t, width), F32), jnp.zeros((t, width), F32),
                           jnp.zeros((t, width), F32)], axis=0)
        tab = jnp.where(jnp.arange(width) < active, tab, ident)
    return tab


def _rope(x, tab_ref, half):
    w = x.shape[1]
    reps = w // LANE

    def wide(k):
        t = tab_ref[k]
        return t if reps == 1 else jnp.concatenate([t] * reps, axis=1)

    return (x * wide(0) + pltpu.roll(x, w - half, axis=1) * wide(1)
            + pltpu.roll(x, half, axis=1) * wide(2))


def _prep_kernel(p_ref, m_ref, ta_ref, ti_ref, tk_ref, q_ref, k_ref, v_ref, iq_ref, ik_ref, ik4_ref):
    q_ref[...] = _rope(p_ref[:, 0:ATT_WIDTH], ta_ref, ROT_DIMS // 2)
    k_ref[...] = _rope(p_ref[:, 512:640], ta_ref, ROT_DIMS // 2)
    v_ref[...] = p_ref[:, 640:768]
    iq_ref[...] = _rope(p_ref[:, 768:896], ti_ref, IDX_ROT // 2)
    ikr = _rope(m_ref[...], tk_ref, IDX_ROT // 2)
    ik_ref[...] = ikr[:, 0:IDX_DIM]
    m = jnp.where(_lane_iota(ikr.shape) < IDX_DIM, ikr, 0.0)
    ik4_ref[...] = (m + pltpu.roll(m, 32, axis=1)) + (pltpu.roll(m, 64, axis=1) + pltpu.roll(m, 96, axis=1))


def attn_prep(proj, misc, tab_a, tab_i, tab_k, tm, tiles_per_seq):
    m = proj.shape[0]
    tspec = pl.BlockSpec((3, tm, LANE), lambda i: (0, i % tiles_per_seq, 0))
    widths = (ATT_WIDTH, KVW, KVW, LANE, IDX_DIM, LANE)
    return pl.pallas_call(
        _prep_kernel,
        grid=(m // tm,),
        in_specs=[pl.BlockSpec((tm, 3 * BW), lambda i: (i, ATT_COL0 // (3 * BW))),
                  pl.BlockSpec((tm, LANE), lambda i: (i, 0)), tspec, tspec, tspec],
        out_specs=[pl.BlockSpec((tm, w), lambda i: (i, 0)) for w in widths],
        out_shape=[jax.ShapeDtypeStruct((m, w), F32) for w in widths],
        compiler_params=_cp(("parallel",)),
        name="attn_prep",
    )(proj, misc, tab_a, tab_i, tab_k)


BISECT_MAX_ITERS = 48
FAR = 2.0 ** 126


def _topk_select(s_ref, idx, axis, n_idx_bits, p_sc, topk):
    shape = s_ref.shape
    kshape = tuple(1 if a == axis else n for a, n in enumerate(shape))

    def count(ones):
        return jnp.sum(ones, axis=axis, keepdims=True)

    s = s_ref[...]
    lo0 = jnp.min(jnp.where(s > -FAR, s, FAR), axis=axis, keepdims=True)
    mx = jnp.max(s, axis=axis, keepdims=True)
    hi0 = mx + (jnp.abs(mx) * 2.0 ** -20 + 1e-30)
    cnt_lo0 = count(jnp.where(s >= lo0, 1, 0))

    def cond(c):
        it, _, _, cnt_lo, _ = c
        return jnp.logical_and(it < BISECT_MAX_ITERS, jnp.max(cnt_lo) > topk)

    def body(c):
        it, lo, hi, cnt_lo, cnt_hi = c
        mid = 0.5 * lo + 0.5 * hi
        cm = count(jnp.where(s_ref[...] >= mid, 1, 0))
        ge = cm >= topk
        return (it + 1, jnp.where(ge, mid, lo), jnp.where(ge, hi, mid),
                jnp.where(ge, cm, cnt_lo), jnp.where(ge, cnt_hi, cm))

    _, lo, hi, cnt_lo, cnt_hi = lax.while_loop(
        cond, body, (jnp.int32(0), lo0, hi0, cnt_lo0, jnp.zeros(kshape, jnp.int32)))
    need = topk - cnt_hi
    p_sc[...] = jnp.full(kshape, (1 << n_idx_bits) - 1, jnp.int32)

    @pl.when(jnp.max(cnt_lo - cnt_hi - need) > 0)
    def _():
        tied = jnp.where(s_ref[...] >= lo, jnp.where(s_ref[...] >= hi, 0, 1), 0)

        def ibody(i, p):
            cand = p + jnp.left_shift(jnp.int32(1), n_idx_bits - 1 - i)
            taken = count(jnp.where(idx < cand, tied, 0))
            return jnp.where(taken < need, cand, p)

        p_sc[...] = lax.fori_loop(0, n_idx_bits, ibody, jnp.zeros(kshape, jnp.int32))

    s = s_ref[...]
    return jnp.where(s >= hi, 1, jnp.where(s >= lo, jnp.where(idx <= p_sc[...], 1, 0), 0))


def _group_queries(q, g):
    tiles = []
    keep = (_lane_iota((q.shape[0], LANE)) // HEAD_DIM) == g
    for hl in range(ATT_HEADS // KV_HEADS):
        h = g * (ATT_HEADS // KV_HEADS) + hl
        t = q[:, (h // 2) * LANE:(h // 2 + 1) * LANE]
        if h % 2 != g:
            t = pltpu.roll(t, HEAD_DIM, axis=1)
        tiles.append(jnp.where(keep, t, 0.0))
    return jnp.concatenate(tiles, axis=0)


def _ungroup_outputs(o_groups, tq):
    low = _lane_iota((tq, LANE)) < HEAD_DIM
    tiles = []
    for j in range(ATT_HEADS // 2):
        halves = []
        for h in (2 * j, 2 * j + 1):
            g, hl = divmod(h, ATT_HEADS // KV_HEADS)
            t = o_groups[g][hl * tq:(hl + 1) * tq]
            if h % 2 != g:
                t = pltpu.roll(t, HEAD_DIM, axis=1)
            halves.append(t)
        tiles.append(jnp.where(low, halves[0], halves[1]))
    return jnp.concatenate(tiles, axis=1)


def _masked_attention(q, k, v, bias):
    tq = q.shape[0]
    kb, vb = _bf(k), _bf(v)
    outs = []
    for g in range(KV_HEADS):
        qg = _group_queries(q, g)
        s = _dot_nt(qg, kb) * HEAD_DIM ** -0.5
        s = s + jnp.concatenate([bias] * (ATT_HEADS // KV_HEADS), axis=0)
        p = jnp.exp(s - jnp.max(s, axis=-1, keepdims=True))
        l = jnp.sum(p, axis=-1, keepdims=True)
        outs.append(_dot(p, vb) / l)
    return _ungroup_outputs(outs, tq)


def _attn_prompt_kernel(q_ref, iq_ref, iwt_ref, k_ref, v_ref, ik4_ref, o_ref, key_sc, p_sc, *, qblk0, tq, topk):
    tk = k_ref.shape[1]
    q0 = (qblk0 + pl.program_id(1)) * tq
    iq = iq_ref[0]
    ik4 = _bf(ik4_ref[0])
    iwt = iwt_ref[...] * IDX_HEADS ** -0.5
    head_of_lane = _lane_iota((tq, LANE)) // IDX_DIM
    score = jnp.zeros((tk, tq), F32)
    for h in range(IDX_HEADS):
        lg = _dot_nt(ik4, jnp.where(head_of_lane == h, iq, 0.0)) * IDX_DIM ** -0.5
        score = score + jnp.maximum(lg, 0.0) * iwt[h:h + 1, :]
    s_idx = _row_iota((tk, tq))
    adm = s_idx <= q0 + _lane_iota((tk, tq))
    key_sc[...] = jnp.where(adm, score, -FAR)
    sel = _topk_select(key_sc, s_idx, 0, max(1, (tk - 1).bit_length()), p_sc, topk)
    bias_t = jnp.where(sel > 0, 0.0, NEG)
    o_ref[0] = _masked_attention(q_ref[0], k_ref[0], v_ref[0], bias_t.T)


def attn_prompt(qr, kr, v, iqr, ik4, iwt, bsz, seq, n_classes=4, tq=128):
    topk = min(TOPK_MAX, seq // 4)
    nq = seq // tq
    per = max(1, nq // n_classes)
    q3 = qr.reshape(bsz, seq, ATT_WIDTH)
    iq3 = iqr.reshape(bsz, seq, LANE)
    k3, v3, ik3 = (a.reshape(bsz, seq, LANE) for a in (kr, v, ik4))
    outs = []
    for c in range(nq // per):
        tk = (c + 1) * per * tq
        qb0 = c * per
        out = pl.pallas_call(
            functools.partial(_attn_prompt_kernel, qblk0=qb0, tq=tq, topk=topk),
            grid=(bsz, per),
            in_specs=[pl.BlockSpec((1, tq, ATT_WIDTH), lambda b, j, qb0=qb0: (b, qb0 + j, 0)),
                      pl.BlockSpec((1, tq, LANE), lambda b, j, qb0=qb0: (b, qb0 + j, 0)),
                      pl.BlockSpec((8, tq), lambda b, j, qb0=qb0: (0, b * nq + qb0 + j)),
                      pl.BlockSpec((1, tk, LANE), lambda b, j: (b, 0, 0)),
                      pl.BlockSpec((1, tk, LANE), lambda b, j: (b, 0, 0)),
                      pl.BlockSpec((1, tk, LANE), lambda b, j: (b, 0, 0))],
            out_specs=pl.BlockSpec((1, tq, ATT_WIDTH), lambda b, j: (b, j, 0)),
            out_shape=jax.ShapeDtypeStruct((bsz, per * tq, ATT_WIDTH), F32),
            scratch_shapes=[pltpu.VMEM((tk, tq), F32), pltpu.VMEM((1, tq), jnp.int32)],
            compiler_params=_cp(("parallel", "arbitrary")),
            name=f"attn_prompt_{tk}",
        )(q3, iq3, iwt, k3, v3, ik3)
        outs.append(out)
    return jnp.concatenate(outs, axis=1).reshape(bsz * seq, ATT_WIDTH)


def _attn_sample_kernel(pt_ref, q_ref, iq_ref, misc_ref, kn_ref, vn_ref, ikn_ref, *rest, npg, topk):
    del pt_ref
    kp, vp, ikp = rest[0:npg], rest[npg:2 * npg], rest[2 * npg:3 * npg]
    o_ref, kbuf, vbuf, ikbuf, key_sc, p_sc = rest[3 * npg:]
    tq = q_ref.shape[1]
    past = kbuf.shape[0] - LANE
    p = pl.program_id(1)
    for j in range(npg):
        row = pl.multiple_of((p * npg + j) * PAGE_SIZE, PAGE_SIZE)
        kbuf[pl.ds(row, PAGE_SIZE), :] = kp[j][0]
        vbuf[pl.ds(row, PAGE_SIZE), :] = vp[j][0]
        ikbuf[pl.ds(row, PAGE_SIZE), :] = ikp[j][0]

    @pl.when(p == pl.num_programs(1) - 1)
    def _():
        for buf, new in ((kbuf, kn_ref), (vbuf, vn_ref), (ikbuf, ikn_ref)):
            buf[past:past + LANE, :] = jnp.zeros((LANE, buf.shape[1]), F32)
            buf[past:past + tq, :] = new[0]
        lk = kbuf.shape[0]
        iq = iq_ref[0]
        iqs = jnp.concatenate([iq[:, h * IDX_DIM:(h + 1) * IDX_DIM] for h in range(IDX_HEADS)], axis=0)
        misc = misc_ref[0]
        iw = jnp.concatenate([misc[:, IDX_DIM + h:IDX_DIM + h + 1] for h in range(IDX_HEADS)], axis=0)
        lg = _dot_nt(iqs, ikbuf[...]) * IDX_DIM ** -0.5
        wl = jnp.maximum(lg, 0.0) * (iw * IDX_HEADS ** -0.5)
        score = (wl[0:tq] + wl[tq:2 * tq]) + (wl[2 * tq:3 * tq] + wl[3 * tq:4 * tq])
        s_idx = _lane_iota((tq, lk))
        adm = s_idx <= past + _row_iota((tq, lk))
        key_sc[...] = jnp.where(adm, score, -FAR)
        sel = _topk_select(key_sc, s_idx, 1, (lk - 1).bit_length(), p_sc, topk)
        bias = jnp.where(sel > 0, 0.0, NEG)
        o_ref[0] = _masked_attention(q_ref[0], kbuf[...], vbuf[...], bias)


def attn_sample(qr, kr, v, iqr, ikr, misc, cache_k, cache_v, cache_ik, page_table, layer, dseq, npg=8):
    db, n_pages = page_table.shape
    n_pool = cache_k.shape[0] // DEPTH
    past = n_pages * PAGE_SIZE
    lk = past + LANE
    topk = min(TOPK_MAX, (past + dseq) // 4)
    base = layer * n_pool
    r3 = lambda a: a.reshape(db, dseq, a.shape[-1])
    row_spec = lambda w, col=0: pl.BlockSpec((1, dseq, w), lambda b, p, pt, col=col: (b, 0, col))

    def page_spec(w, j):
        return pl.BlockSpec((1, PAGE_SIZE, w), lambda b, p, pt, j=j: (pt[b, p * npg + j] + base, 0, 0))

    in_specs = [row_spec(ATT_WIDTH), row_spec(LANE), row_spec(LANE),
                row_spec(LANE), row_spec(LANE), row_spec(IDX_DIM)]
    in_specs += [page_spec(LANE, j) for j in range(npg)]
    in_specs += [page_spec(LANE, j) for j in range(npg)]
    in_specs += [page_spec(IDX_DIM, j) for j in range(npg)]
    out = pl.pallas_call(
        functools.partial(_attn_sample_kernel, npg=npg, topk=topk),
        grid_spec=pltpu.PrefetchScalarGridSpec(
            num_scalar_prefetch=1,
            grid=(db, n_pages // npg),
            in_specs=in_specs,
            out_specs=pl.BlockSpec((1, dseq, ATT_WIDTH), lambda b, p, pt: (b, 0, 0)),
            scratch_shapes=[pltpu.VMEM((lk, LANE), F32), pltpu.VMEM((lk, LANE), F32),
                            pltpu.VMEM((lk, IDX_DIM), F32), pltpu.VMEM((dseq, lk), F32),
                            pltpu.VMEM((dseq, 1), jnp.int32)]),
        out_shape=jax.ShapeDtypeStruct((db, dseq, ATT_WIDTH), F32),
        compiler_params=_cp(("parallel", "arbitrary")),
        name="attn_sample",
    )(page_table, r3(qr), r3(iqr), r3(misc), r3(kr), r3(v), r3(ikr),
      *([cache_k] * npg), *([cache_v] * npg), *([cache_ik] * npg))
    return out.reshape(db * dseq, ATT_WIDTH)


def _s5_param_kernel(ldt_ref, are_ref, aim_ref, bre_ref, bim_ref, abar_ref, win_ref):
    dt = jnp.exp(ldt_ref[...])
    a_re, a_im = are_ref[...], aim_ref[...]
    mag = jnp.exp(dt * a_re)
    abar_re = mag * jnp.cos(dt * a_im)
    abar_im = mag * jnp.sin(dt * a_im)
    den = a_re * a_re + a_im * a_im
    num_re = abar_re - 1.0
    coef_re = (num_re * a_re + abar_im * a_im) / den
    coef_im = (abar_im * a_re - num_re * a_im) / den
    abar_ref[:, 0:S5_NS] = jnp.broadcast_to(abar_re, (8, S5_NS))
    abar_ref[:, S5_NS:2 * S5_NS] = jnp.broadcast_to(abar_im, (8, S5_NS))
    b_re, b_im = bre_ref[...], bim_ref[...]
    win_ref[:, 0:S5_NS] = _bf(coef_re * b_re - coef_im * b_im)
    win_ref[:, S5_NS:2 * S5_NS] = _bf(coef_re * b_im + coef_im * b_re)


def _block_diag_in(b):
    eye = jnp.eye(S5_GROUPS, dtype=b.dtype)
    return jnp.einsum('gnc,gh->gchn', b, eye).reshape(S5_WIDTH, S5_NS)


def _block_diag_out(c):
    eye = jnp.eye(S5_GROUPS, dtype=c.dtype)
    return jnp.einsum('gcn,gh->gnhc', c, eye).reshape(S5_NS, S5_WIDTH)


def s5_params(log_dt, a_re, a_im, b_re, b_im):
    per_state = lambda a: a.reshape(1, S5_NS)
    ldt = per_state(jnp.broadcast_to(log_dt[:, None], (S5_GROUPS, S5_STATE)))
    return pl.pallas_call(
        _s5_param_kernel,
        out_shape=[jax.ShapeDtypeStruct((8, 2 * S5_NS), F32),
                   jax.ShapeDtypeStruct((S5_WIDTH, 2 * S5_NS), BF16)],
        compiler_params=pltpu.CompilerParams(vmem_limit_bytes=VMEM_LIMIT),
        name="s5_params",
    )(ldt, per_state(a_re), per_state(a_im), _block_diag_in(b_re), _block_diag_in(b_im))


def _gelu_tanh(x):
    return 0.5 * x * (1.0 + jnp.tanh(math.sqrt(2.0 / math.pi) * (x + 0.044715 * (x * x * x))))


S5_LANES = 512


def _s5_kernel(u_ref, abar_ref, win_ref, h0_ref, wout_ref, d_ref, wglu_ref, bglu_ref,
               y_ref, hn_ref, s_sc, *, bsz, tc):
    c = pl.program_id(0)

    @pl.when(c == 0)
    def _():
        hn_ref[...] = h0_ref[...]

    u = u_ref[...]
    s_sc[...] = jnp.dot(_bf(u), win_ref[...], preferred_element_type=F32)
    nchunk = S5_NS // S5_LANES
    for rg in range(bsz // 8):
        rows = slice(rg * 8, rg * 8 + 8)

        def body(t, carry):
            row0 = pl.multiple_of(t * bsz + rg * 8, 8)
            new = []
            for cc in range(nchunk):
                lre = slice(cc * S5_LANES, (cc + 1) * S5_LANES)
                lim = slice(S5_NS + cc * S5_LANES, S5_NS + (cc + 1) * S5_LANES)
                xr, xi = carry[2 * cc], carry[2 * cc + 1]
                ar, ai = abar_ref[:, lre], abar_ref[:, lim]
                nr = (ar * xr - ai * xi) + s_sc[pl.ds(row0, 8), lre]
                ni = (ar * xi + ai * xr) + s_sc[pl.ds(row0, 8), lim]
                s_sc[pl.ds(row0, 8), lre] = nr
                s_sc[pl.ds(row0, 8), lim] = ni
                new += [nr, ni]
            return tuple(new)

        init = []
        for cc in range(nchunk):
            init += [hn_ref[rows, cc * S5_LANES:(cc + 1) * S5_LANES],
                     hn_ref[rows, S5_NS + cc * S5_LANES:S5_NS + (cc + 1) * S5_LANES]]
        fin = lax.fori_loop(0, tc, body, tuple(init))
        for cc in range(nchunk):
            hn_ref[rows, cc * S5_LANES:(cc + 1) * S5_LANES] = fin[2 * cc]
            hn_ref[rows, S5_NS + cc * S5_LANES:S5_NS + (cc + 1) * S5_LANES] = fin[2 * cc + 1]

    y = (jnp.dot(_bf(s_sc[:, 0:S5_NS]), wout_ref[0:S5_NS, :], preferred_element_type=F32)
         - jnp.dot(_bf(s_sc[:, S5_NS:2 * S5_NS]), wout_ref[S5_NS:2 * S5_NS, :], preferred_element_type=F32)
         + d_ref[...] * u)
    z = _gelu_tanh(y)
    y_ref[...] = z * _sigmoid(jnp.dot(_bf(z), wglu_ref[...], preferred_element_type=F32) + bglu_ref[...])


def s5_branch(u_tm, h0, abar8, win, wout_bf, d, wglu_bf, bglu, bsz, seq, tc):
    rows = tc * bsz
    const = lambda shape: pl.BlockSpec(shape, lambda c: (0,) * len(shape))
    return pl.pallas_call(
        functools.partial(_s5_kernel, bsz=bsz, tc=tc),
        grid=(seq // tc,),
        in_specs=[pl.BlockSpec((rows, S5_WIDTH), lambda c: (c, 0)),
                  const((8, 2 * S5_NS)), const((S5_WIDTH, 2 * S5_NS)), const((bsz, 2 * S5_NS)),
                  const((2 * S5_NS, S5_WIDTH)), const((1, S5_WIDTH)), const((S5_WIDTH, S5_WIDTH)),
                  const((1, S5_WIDTH))],
        out_specs=[pl.BlockSpec((rows, S5_WIDTH), lambda c: (c, 0)), const((bsz, 2 * S5_NS))],
        out_shape=[jax.ShapeDtypeStruct((seq * bsz, S5_WIDTH), F32),
                   jax.ShapeDtypeStruct((bsz, 2 * S5_NS), F32)],
        scratch_shapes=[pltpu.VMEM((rows, 2 * S5_NS), F32)],
        compiler_params=_cp(("arbitrary",)),
        name="s5_branch",
    )(u_tm, abar8, win, h0, wout_bf, d.reshape(1, S5_WIDTH), wglu_bf, bglu.reshape(1, S5_WIDTH))


def _pad_rows(x, rows):
    n = x.shape[0]
    return x if n == rows else jnp.concatenate([x, jnp.zeros((rows - n, x.shape[1]), x.dtype)], axis=0)


def _head_mean(x, amat_bf):
    return _dot2(x, amat_bf)


def _ret_kernel(q_ref, k_ref, v_ref, g_ref, tab_ref, s0_ref, dmat_ref, qdec_ref, kdec_ref, decm_ref,
                bdm_ref, amat_ref, y_ref, s_ref, *, rows):
    n = q_ref.shape[0]

    @pl.when(pl.program_id(1) == 0)
    def _():
        s_ref[...] = s0_ref[...]

    q = _pad_rows(_rope(q_ref[...], tab_ref, RET_DK // 2), rows)
    k = _pad_rows(_rope(k_ref[...], tab_ref, RET_DK // 2) * RET_DK ** -0.5, rows)
    v = _pad_rows(v_ref[...], rows)
    state = s_ref[0]
    inter = _dot(q * qdec_ref[...], state)
    lane = _lane_iota((rows, LANE))
    tiles = []
    for p in range(RET_HEADS // 2):
        lanes = slice(p * LANE, (p + 1) * LANE)
        qp, kp, vp = q[:, lanes], _bf(k[:, lanes]), v[:, lanes]
        acc = None
        for hh in range(2):
            mine = (lane < RET_DK) if hh == 0 else (lane >= RET_DK)
            s = _dot_nt(jnp.where(mine, qp, 0.0), kp) * dmat_ref[2 * p + hh]
            part = _dot(s, jnp.where(mine, vp, 0.0))
            acc = part if acc is None else acc + part
        tiles.append(acc)
    o = jnp.concatenate(tiles, axis=1) + inter
    s_ref[0] = state * decm_ref[...] + _dot_tn(k * kdec_ref[...], v) * bdm_ref[...]
    amat = amat_ref[...]
    mu = _head_mean(o, amat)
    d = o - mu
    var = _head_mean(d * d, amat)
    on = d * lax.rsqrt(var + 1e-5)
    y_ref[...] = (_silu(g_ref[...]) * on[0:n]).astype(y_ref.dtype)


def _head_block_mask():
    h = np.arange(BW) // 64
    return (h[:, None] == h[None, :]).astype(np.float32)


def _ret_consts(rows, n_true):
    lg = np.log(1.0 - np.exp2(-5.0 - np.arange(RET_HEADS, dtype=np.float64)))
    i = np.arange(rows, dtype=np.float64)
    rel = i[:, None] - i[None, :]
    dmat = np.where(rel[None] >= 0, np.exp(np.minimum(rel[None], rows) * lg[:, None, None]), 0.0)
    lane_lg = np.repeat(lg, 64)[None, :]
    qdec = np.exp((i[:, None] + 1.0) * lane_lg)
    kdec = np.where(i[:, None] < n_true, np.exp((n_true - 1.0 - i[:, None]) * lane_lg), 0.0)
    bdm = _head_block_mask()
    decm = bdm * np.exp(n_true * np.repeat(lg, 64))[:, None]
    f = lambda a: jnp.asarray(a, F32)
    return f(dmat), f(qdec), f(kdec), f(decm), f(bdm), jnp.asarray(bdm / 64.0, BF16)


def ret_branch(proj, tab_r, s0_bd, bsz, seq, rows, n):
    nch = seq // n
    dmat, qdec, kdec, decm, bdm, amat = _ret_consts(rows, n)
    col = lambda j: pl.BlockSpec((n, BW), lambda b, c, j=j: (b * nch + c, j))
    const = lambda shape: pl.BlockSpec(shape, lambda b, c: (0,) * len(shape))
    return pl.pallas_call(
        functools.partial(_ret_kernel, rows=rows),
        grid=(bsz, nch),
        in_specs=[col(0), col(1), col(2), col(3),
                  pl.BlockSpec((3, n, LANE), lambda b, c: (0, c, 0)),
                  pl.BlockSpec((1, BW, BW), lambda b, c: (b, 0, 0)),
                  const((RET_HEADS, rows, rows)), const((rows, BW)), const((rows, BW)),
                  const((BW, BW)), const((BW, BW)), const((BW, BW))],
        out_specs=[pl.BlockSpec((n, BW), lambda b, c: (b * nch + c, 0)),
                   pl.BlockSpec((1, BW, BW), lambda b, c: (b, 0, 0))],
        out_shape=[jax.ShapeDtypeStruct((bsz * seq, BW), F32),
                   jax.ShapeDtypeStruct((bsz, BW, BW), F32)],
        compiler_params=_cp(("parallel", "arbitrary")),
        name="ret_branch",
    )(proj, proj, proj, proj, tab_r, s0_bd, dmat, qdec, kdec, decm, bdm, amat)


def to_block_diag(s):
    b = s.shape[0]
    eye = jnp.eye(s.shape[1], dtype=s.dtype)
    return jnp.einsum('bhde,hg->bhdge', s, eye).reshape(b, BW, BW)


def from_block_diag(s_bd):
    b = s_bd.shape[0]
    s5 = s_bd.reshape(b, BW // 64, 64, BW // 64, 64)
    idx = jnp.arange(BW // 64)
    return s5[:, idx, :, idx, :].transpose(1, 0, 2, 3)


def _softplus(x):
    return jnp.maximum(x, 0.0) + jnp.log(1.0 + jnp.exp(-jnp.abs(x)))


def _merge_masks(c):
    rowi, coli = _row_iota((c, c)), _lane_iota((c, c))
    eye = jnp.where(rowi == coli, 1.0, 0.0)
    masks = []
    s = 1
    while s < c:
        same = (rowi // (2 * s)) == (coli // (2 * s))
        lower_left = jnp.where((rowi // s) % 2 == 1, jnp.where((coli // s) % 2 == 0, 1.0, 0.0), 0.0)
        masks.append(jnp.where(same, lower_left, 0.0))
        s *= 2
    return eye, masks


def _unit_lower_inverse(nmat, eye, masks):
    inv = eye - nmat * masks[0]
    for m in masks[1:]:
        inv = inv - _dot(inv, _dot(nmat * m, inv))
    return inv


def _pair_select(parts, rows):
    low = _lane_iota((rows, LANE)) < GDN_DK
    return jnp.concatenate([jnp.where(low, parts[2 * p], parts[2 * p + 1]) for p in range(GDN_HEADS // 2)],
                           axis=1)


def _gdn_kernel(q_ref, k_ref, v_ref, g_ref, misc_ref, cs0_ref, cw_ref, alog_ref, dtb_ref, ng_ref, h0_ref,
                ea_ref, eb_ref, bdm_ref, tri_ref, amat_ref, y_ref, h_ref, cs_ref, xp_sc, *, rows):
    n = q_ref.shape[0]
    cw = 3 * BW

    @pl.when(pl.program_id(1) == 0)
    def _():
        xp_sc[...] = jnp.zeros(xp_sc.shape, F32)
        xp_sc[5:8, :] = cs0_ref[0]
        h_ref[...] = h0_ref[...]

    for j, r in enumerate((q_ref, k_ref, v_ref)):
        xp_sc[8:8 + n, j * BW:(j + 1) * BW] = r[...]
    conv = xp_sc[5:5 + rows, :] * cw_ref[0:1, :]
    for i in range(1, GDN_CONV):
        conv = conv + xp_sc[5 + i:5 + i + rows, :] * cw_ref[i:i + 1, :]
    tail = xp_sc[8 + n - 3:8 + n, :]
    xp_sc[5:8, :] = tail
    cs_ref[0] = tail
    xc = _silu(conv)
    valid = _row_iota((rows, BW)) < n
    bdm = bdm_ref[...]
    bdm_bf = _bf(bdm)
    q, k, v = xc[:, 0:BW], xc[:, BW:2 * BW], xc[:, 2 * BW:cw]
    q = q * lax.rsqrt(_dot2(q * q, bdm_bf) + EPS) * GDN_DK ** -0.5
    k = k * lax.rsqrt(_dot2(k * k, bdm_bf) + EPS)
    misc = _pad_rows(misc_ref[...], rows)
    beta = _sigmoid(_dot2(misc, eb_ref[...]))
    la = -jnp.exp(alog_ref[...]) * _softplus(_dot2(misc, ea_ref[...]) + dtb_ref[...])
    k = jnp.where(valid, k, 0.0)
    v = jnp.where(valid, v, 0.0)
    la = jnp.where(valid, la, 0.0)
    la_hi, la_lo = _split(la)
    tri = tri_ref[...]
    gall = (jnp.dot(tri, la_hi, preferred_element_type=F32)
            + jnp.dot(tri, la_lo, preferred_element_type=F32))
    c = GDN_CHUNK
    rowi, coli = _row_iota((c, c)), _lane_iota((c, c))
    lane = _lane_iota((c, LANE))
    eye, masks = _merge_masks(c)
    outs = []
    for ci in range(rows // c):
        r = slice(ci * c, (ci + 1) * c)
        qc, kc, vc, bc, gc = q[r], k[r], v[r], beta[r], gall[r]
        g_t = gc.T
        eg = jnp.exp(gc)
        g_last = gc[c - 1:c, :]
        x_in = bc * eg * kc
        v_in = bc * vc
        w_parts, u_parts, a_mats = [], [], []
        for h in range(GDN_HEADS):
            lanes = slice((h // 2) * LANE, (h // 2 + 1) * LANE)
            mine = (lane < GDN_DK) if h % 2 == 0 else (lane >= GDN_DK)
            kp = _bf(kc[:, lanes])
            kk = _dot_nt(jnp.where(mine, kc[:, lanes], 0.0), kp)
            qk = _dot_nt(jnp.where(mine, qc[:, lanes], 0.0), kp)
            diff = gc[:, h * GDN_DK:h * GDN_DK + 1] - g_t[h * GDN_DK:h * GDN_DK + 1, :]
            dec = jnp.exp(jnp.where(rowi >= coli, diff, NEG))
            nmat = bc[:, h * GDN_DK:h * GDN_DK + 1] * jnp.where(rowi > coli, dec, 0.0) * kk
            inv = _unit_lower_inverse(nmat, eye, masks)
            wu = _dot(inv, jnp.concatenate([x_in[:, lanes], v_in[:, lanes]], axis=1))
            w_parts.append(wu[:, 0:LANE])
            u_parts.append(wu[:, LANE:2 * LANE])
            a_mats.append(qk * dec)
        hbd = h_ref[0]
        u = _pair_select(u_parts, c) - _dot(_pair_select(w_parts, c), hbd)
        o = eg * _dot(qc, hbd)
        o = o + _pair_select([_dot(a_mats[h], u[:, (h // 2) * LANE:(h // 2 + 1) * LANE])
                              for h in range(GDN_HEADS)], c)
        h_ref[0] = jnp.exp(g_last) * hbd + _dot_tn(kc * jnp.exp(g_last - gc), u) * bdm
        outs.append(o)
    o = outs[0] if len(outs) == 1 else jnp.concatenate(outs, axis=0)
    on = o * lax.rsqrt(_dot2(o * o, amat_ref[...]) + EPS) * ng_ref[...]
    y_ref[...] = on[0:n] * _silu(g_ref[...])


def _gdn_consts(rows):
    lanes = np.arange(BW) // 64
    ea = np.zeros((LANE, BW), np.float32)
    eb = np.zeros((LANE, BW), np.float32)
    ea[MISC_DA + lanes, np.arange(BW)] = 1.0
    eb[MISC_DB + lanes, np.arange(BW)] = 1.0
    i = np.arange(rows)
    tri = ((i[:, None] // GDN_CHUNK == i[None, :] // GDN_CHUNK) & (i[:, None] >= i[None, :])).astype(np.float32)
    bdm = _head_block_mask()
    return (jnp.asarray(ea, BF16), jnp.asarray(eb, BF16), jnp.asarray(bdm, F32), jnp.asarray(tri, BF16),
            jnp.asarray(bdm / 64.0, BF16))


def gdn_branch(proj, misc, cs0, conv_w, a_log, dt_bias, norm_g, h0_bd, bsz, seq, rows, n):
    nblk = seq // n
    ea, eb, bdm, tri, amat = _gdn_consts(rows)
    per_lane = lambda a, reps: jnp.repeat(a, reps).reshape(1, BW) if reps > 1 else jnp.tile(a, BW // a.shape[0]).reshape(1, BW)
    col = lambda j: pl.BlockSpec((n, BW), lambda b, c, j=j: (b * nblk + c, j))
    const = lambda shape: pl.BlockSpec(shape, lambda b, c: (0,) * len(shape))
    per_b = lambda shape: pl.BlockSpec(shape, lambda b, c: (b,) + (0,) * (len(shape) - 1))
    cw = 3 * BW
    return pl.pallas_call(
        functools.partial(_gdn_kernel, rows=rows),
        grid=(bsz, nblk),
        in_specs=[col(4), col(5), col(6), col(7),
                  pl.BlockSpec((n, LANE), lambda b, c: (b * nblk + c, 0)),
                  per_b((1, GDN_CONV - 1, cw)), const((GDN_CONV, cw)),
                  const((1, BW)), const((1, BW)), const((1, BW)), per_b((1, BW, BW)),
                  const((LANE, BW)), const((LANE, BW)), const((BW, BW)), const((rows, rows)), const((BW, BW))],
        out_specs=[pl.BlockSpec((n, BW), lambda b, c: (b * nblk + c, 0)),
                   per_b((1, BW, BW)), per_b((1, GDN_CONV - 1, cw))],
        out_shape=[jax.ShapeDtypeStruct((bsz * seq, BW), F32),
                   jax.ShapeDtypeStruct((bsz, BW, BW), F32),
                   jax.ShapeDtypeStruct((bsz, GDN_CONV - 1, cw), F32)],
        scratch_shapes=[pltpu.VMEM((rows + 8, cw), F32)],
        compiler_params=_cp(("parallel", "arbitrary")),
        name="gdn_branch",
    )(proj, proj, proj, proj, misc, cs0, conv_w, per_lane(a_log, 64), per_lane(dt_bias, 64),
      per_lane(norm_g, 1), h0_bd, ea, eb, bdm, tri, amat)


_BRANCH_OFFS = (0, ATT_WIDTH, ATT_WIDTH + BW, ATT_WIDTH + 2 * BW, MIX_WIDTH)


def _merge_kernel(x_ref, g_ref, sc_ref, sh_ref, gm_ref, ya_ref, yb_ref, yc_ref, yd_ref,
                  wg_ref, wb_ref, wo_ref, o_ref):
    x = x_ref[...]
    h = _bf(_norm_mod(x, g_ref[...], sc_ref[0], sh_ref[0]))
    merged = None
    for b, y_ref in enumerate((ya_ref, yb_ref, yc_ref, yd_ref)):
        gate = _sigmoid(jnp.dot(h, wg_ref[:, b * D_MODEL:(b + 1) * D_MODEL], preferred_element_type=F32))
        term = gate * jnp.dot(_bf(y_ref[...]), wb_ref[_BRANCH_OFFS[b]:_BRANCH_OFFS[b + 1], :],
                              preferred_element_type=F32)
        merged = term if merged is None else merged + term
    y = jnp.dot(_bf(merged), wo_ref[...], preferred_element_type=F32)
    o_ref[...] = x + gm_ref[0] * y


def merge_out(x, g, sc, sh, gm, ya, yb, yc, yd, wg_bf, wb_bf, wo_bf, tm, tiles_per_seq):
    m = x.shape[0]
    ms = _mod_spec(tm, sc.shape[1], tiles_per_seq)
    row = lambda w: pl.BlockSpec((tm, w), lambda i: (i, 0))
    const = lambda shape: pl.BlockSpec(shape, lambda i: (0,) * len(shape))
    return pl.pallas_call(
        _merge_kernel,
        grid=(m // tm,),
        in_specs=[row(D_MODEL), const((1, D_MODEL)), ms, ms, ms, row(ATT_WIDTH), row(BW), row(BW), row(BW),
                  const((D_MODEL, N_BRANCH * D_MODEL)), const((MIX_WIDTH, D_MODEL)), const((D_MODEL, D_MODEL))],
        out_specs=row(D_MODEL),
        out_shape=jax.ShapeDtypeStruct((m, D_MODEL), F32),
        compiler_params=_cp(("parallel",)),
        name="merge_out",
    )(x, g.reshape(1, D_MODEL), sc, sh, gm, ya, yb, yc, yd, wg_bf, wb_bf, wo_bf)


def _top2(masked, lane):
    m1 = jnp.max(masked, axis=-1, keepdims=True)
    i1 = jnp.min(jnp.where(masked == m1, lane, LANE), axis=-1, keepdims=True)
    rest = jnp.where(lane == i1, -jnp.inf, masked)
    m2 = jnp.max(rest, axis=-1, keepdims=True)
    i2 = jnp.min(jnp.where(rest == m2, lane, LANE), axis=-1, keepdims=True)
    return m1, i1, m2, i2


def _route(scores, biased):
    lane = _lane_iota(scores.shape)
    grp = lane // EXPERTS_PER_GROUP
    best_val, best_grp = None, None
    for g in range(N_GROUPS):
        m1, _, m2, _ = _top2(jnp.where(grp == g, biased, -jnp.inf), lane)
        gs = m1 + m2
        if g == 0:
            best_val, best_grp = gs, jnp.zeros(gs.shape, jnp.int32)
        else:
            better = gs > best_val
            best_val = jnp.where(better, gs, best_val)
            best_grp = jnp.where(better, g, best_grp)
    _, e1, _, e2 = _top2(jnp.where(grp == best_grp, biased, -jnp.inf), lane)
    s1 = jnp.sum(jnp.where(lane == e1, scores, 0.0), axis=-1, keepdims=True)
    s2 = jnp.sum(jnp.where(lane == e2, scores, 0.0), axis=-1, keepdims=True)
    tot = s1 + s2
    return jnp.where(lane == e1, s1 / tot, 0.0) + jnp.where(lane == e2, s2 / tot, 0.0)


def _moe_kernel(x_ref, g_ref, sc_ref, sh_ref, gm_ref, wr_ref, rb_ref, w1_ref, w3_ref, w2_ref, fg_ref,
                o_ref, h_sc, comb_sc, acc_sc, *, final):
    e = pl.program_id(1)

    @pl.when(e == 0)
    def _():
        h = _norm_mod(x_ref[...], g_ref[...], sc_ref[0], sh_ref[0])
        h_sc[...] = _bf(h)
        scores = _sigmoid(_dot3(h, wr_ref[...]))
        comb_sc[...] = _route(scores, scores + rb_ref[...])
        acc_sc[...] = jnp.zeros(acc_sc.shape, F32)

    h = h_sc[...]
    comb = comb_sc[...]
    ce = jnp.sum(jnp.where(_lane_iota(comb.shape) == e, comb, 0.0), axis=-1, keepdims=True)
    hid = _silu(jnp.dot(h, w1_ref[0], preferred_element_type=F32)) * jnp.dot(h, w3_ref[0], preferred_element_type=F32)
    acc_sc[...] += jnp.dot(_bf(hid * ce), w2_ref[0], preferred_element_type=F32)

    @pl.when(e == pl.num_programs(1) - 1)
    def _():
        out = x_ref[...] + gm_ref[0] * acc_sc[...]
        if final:
            out = out * lax.rsqrt(jnp.mean(out * out, axis=-1, keepdims=True) + EPS) * fg_ref[...]
        o_ref[...] = out


def moe_out(x, g, sc, sh, gm, wr_pad, rb_pad, w1_bf, w3_bf, w2_bf, final_g, final, tm, tiles_per_seq):
    m = x.shape[0]
    ms = _mod_spec(tm, sc.shape[1], tiles_per_seq)
    const = lambda shape: pl.BlockSpec(shape, lambda i, e: (0,) * len(shape))
    return pl.pallas_call(
        functools.partial(_moe_kernel, final=final),
        grid=(m // tm, N_EXPERTS),
        in_specs=[pl.BlockSpec((tm, D_MODEL), lambda i, e: (i, 0)), const((1, D_MODEL)), ms, ms, ms,
                  const((D_MODEL, LANE)), const((1, LANE)),
                  pl.BlockSpec((1, D_MODEL, EXPERT_FF), lambda i, e: (e, 0, 0)),
                  pl.BlockSpec((1, D_MODEL, EXPERT_FF), lambda i, e: (e, 0, 0)),
                  pl.BlockSpec((1, EXPERT_FF, D_MODEL), lambda i, e: (e, 0, 0)),
                  const((1, D_MODEL))],
        out_specs=pl.BlockSpec((tm, D_MODEL), lambda i, e: (i, 0)),
        out_shape=jax.ShapeDtypeStruct((m, D_MODEL), F32),
        scratch_shapes=[pltpu.VMEM((tm, D_MODEL), BF16), pltpu.VMEM((tm, LANE), F32),
                        pltpu.VMEM((tm, D_MODEL), F32)],
        compiler_params=_cp(("parallel", "arbitrary")),
        name="moe_out",
    )(x, g.reshape(1, D_MODEL), sc, sh, gm, wr_pad, rb_pad, w1_bf, w3_bf, w2_bf, final_g.reshape(1, D_MODEL))


_REF_SPLITS = (ATT_WIDTH, KVW, KVW, IDX_HEADS * IDX_DIM, IDX_DIM, IDX_HEADS, BW,
               BW, BW, BW, BW, BW, BW, BW, GDN_HEADS, GDN_HEADS, BW)


def pack_w_in(w_in):
    offs = np.concatenate([[0], np.cumsum(_REF_SPLITS)])
    seg = [w_in[:, int(offs[i]):int(offs[i + 1])] for i in range(len(_REF_SPLITS))]
    (aq, ak, av, aiq, aik, aiw, bu, cq, ck, cv, cg, dq, dk, dv, da, db, dg) = seg
    zeros = lambda n: jnp.zeros((D_MODEL, n), w_in.dtype)
    misc = jnp.concatenate([aik, aiw, da, db, zeros(LANE - IDX_DIM - IDX_HEADS - 2 * GDN_HEADS)], axis=1)
    packed = jnp.concatenate([cq, ck, cv, cg, dq, dk, dv, dg, bu, aq, ak, av, aiq, zeros(2 * LANE)], axis=1)
    wt = jnp.concatenate([aiw.T, jnp.zeros((8 - IDX_HEADS, D_MODEL), w_in.dtype)], axis=0)
    return _bf(packed), misc, wt


def _time_major(a, bsz, seq):
    return a.reshape(bsz, seq, a.shape[-1]).transpose(1, 0, 2).reshape(seq * bsz, a.shape[-1])


def _batch_major(a, bsz, seq):
    return a.reshape(seq, bsz, a.shape[-1]).transpose(1, 0, 2).reshape(bsz * seq, a.shape[-1])


def _trunk_layer(x, mods, geom, attend, st, lw, final_g, final):
    bsz, seq, tm, tps, s5_tc, ret_rows, ret_n, gdn_rows, gdn_n = geom
    sh1, sc1, g1, sh2, sc2, g2 = mods
    proj, misc, iwt = in_proj(x, lw['norm1'], sc1, sh1, lw['w_in'], lw['w_misc'], lw['w_iwt'], tm, tps)
    qr, kr, v, iqr, ikr, ik4 = attn_prep(proj, misc, lw['tab_a'], lw['tab_i'], lw['tab_k'], tm, tps)
    ya = attend(qr, kr, v, iqr, ikr, ik4, iwt, misc)
    u_tm = _time_major(proj[:, 8 * BW:9 * BW], bsz, seq)
    y_tm, s5_h = s5_branch(u_tm, st['s5'], lw['s5_abar'], lw['s5_win'], lw['s5_wout'], lw['s5_d'],
                           lw['s5_w_glu'], lw['s5_b_glu'], bsz, seq, s5_tc)
    yb = _batch_major(y_tm, bsz, seq)
    yc, ret_s = ret_branch(proj, lw['tab_r'], st['ret'], bsz, seq, ret_rows, ret_n)
    yd, gdn_s, conv_s = gdn_branch(proj, misc, st['conv'], lw['gdn_conv_w'], lw['gdn_a_log'], lw['gdn_dt_bias'],
                                   lw['gdn_norm_g'], st['gdn'], bsz, seq, gdn_rows, gdn_n)
    x = merge_out(x, lw['norm1'], sc1, sh1, g1, ya, yb, yc, yd, lw['w_gate'], lw['w_br'], lw['w_out'], tm, tps)
    x = moe_out(x, lw['norm2'], sc2, sh2, g2, lw['w_router'], lw['router_bias'], lw['w_e1'], lw['w_e3'],
                lw['w_e2'], final_g, final, tm, tps)
    new_st = {'k': kr, 'v': v, 'ik': ikr, 's5': s5_h, 'ret': ret_s, 'gdn': gdn_s, 'conv': conv_s}
    return x, new_st


def kernel(x_prompt, x_sample, c_prompt, c_sample, cache_k, cache_v, cache_idx_k, page_table,
           state_s5_re, state_s5_im, state_ret, state_gdn, state_gdn_conv,
           norm1_g, norm2_g, final_g, w_ada, b_ada, w_in,
           s5_a_re, s5_a_im, s5_b_re, s5_b_im, s5_c_re, s5_c_im, s5_d, s5_log_dt, s5_w_glu, s5_b_glu,
           gdn_conv_w, gdn_a_log, gdn_dt_bias, gdn_norm_g,
           w_br, w_gate, w_out, w_router, router_bias, w_e1, w_e3, w_e2):
    bsz, seq, _ = x_prompt.shape
    dbs, dseq, _ = x_sample.shape
    depth = w_in.shape[0]
    n_pool = cache_k.shape[1]
    past = page_table.shape[1] * PAGE_SIZE
    mp, ms = bsz * seq, dbs * dseq
    tm_p = 512
    pos_p = jnp.arange(seq, dtype=jnp.int32)
    pos_s = past + jnp.arange(dseq, dtype=jnp.int32)
    pos_s_tok = jnp.tile(pos_s, dbs)

    def tables(pos):
        return {'tab_a': _rope_tables(pos, ROT_DIMS, ROPE_THETA, HEAD_DIM),
                'tab_i': _rope_tables(pos, IDX_ROT, ROPE_THETA, IDX_DIM),
                'tab_k': _rope_tables(pos, IDX_ROT, ROPE_THETA, IDX_DIM, active=IDX_DIM)}

    tabs_p = dict(tables(pos_p), tab_r=_rope_tables(pos_p, RET_DK, RET_THETA, RET_DK))
    tabs_s = dict(tables(pos_s_tok), tab_r=_rope_tables(pos_s, RET_DK, RET_THETA, RET_DK))
    ck = cache_k.reshape(depth * n_pool, PAGE_SIZE, KVW)
    cv = cache_v.reshape(depth * n_pool, PAGE_SIZE, KVW)
    cik = cache_idx_k.reshape(depth * n_pool, PAGE_SIZE, IDX_DIM)
    wr_pad = jnp.pad(w_router, ((0, 0), (0, LANE - N_EXPERTS)))
    rb_pad = jnp.pad(router_bias, (0, LANE - N_EXPERTS)).reshape(1, LANE)
    c_all = jnp.concatenate([c_prompt, c_sample], axis=0)

    geom_p = (bsz, seq, tm_p, seq // tm_p, 64, 256, 256, 256, 256)
    geom_s = (dbs, dseq, ms, 1, dseq, LANE, dseq, GDN_CHUNK, dseq)
    zero_st = {'s5': jnp.zeros((bsz, 2 * S5_NS), F32), 'ret': jnp.zeros((bsz, BW, BW), F32),
               'gdn': jnp.zeros((bsz, BW, BW), F32), 'conv': jnp.zeros((bsz, GDN_CONV - 1, 3 * BW), F32)}

    xp = x_prompt.reshape(mp, D_MODEL)
    xs = x_sample.reshape(ms, D_MODEL)
    outs_p, outs_s = [], []
    for l in range(depth):
        w_in_p, w_misc, w_iwt = pack_w_in(w_in[l])
        abar8, win = s5_params(s5_log_dt[l], s5_a_re[l], s5_a_im[l], s5_b_re[l], s5_b_im[l])
        lw = {'norm1': norm1_g[l], 'norm2': norm2_g[l], 'w_in': w_in_p, 'w_misc': w_misc, 'w_iwt': w_iwt,
              's5_abar': abar8, 's5_win': win,
              's5_wout': _bf(jnp.concatenate([_block_diag_out(s5_c_re[l]), _block_diag_out(s5_c_im[l])], axis=0)),
              's5_d': s5_d[l], 's5_w_glu': _bf(s5_w_glu[l]), 's5_b_glu': s5_b_glu[l],
              'gdn_conv_w': gdn_conv_w[l], 'gdn_a_log': gdn_a_log[l], 'gdn_dt_bias': gdn_dt_bias[l],
              'gdn_norm_g': gdn_norm_g[l],
              'w_br': _bf(w_br[l]), 'w_gate': _bf(w_gate[l]), 'w_out': _bf(w_out[l]),
              'w_router': wr_pad, 'router_bias': rb_pad,
              'w_e1': _bf(w_e1[l]), 'w_e3': _bf(w_e3[l]), 'w_e2': _bf(w_e2[l])}
        mod = ada_mod(c_all, _bf(w_ada[l]), b_ada[l])
        mods = [mod[:, i * D_MODEL:(i + 1) * D_MODEL] for i in range(6)]
        mods_p = [m[:bsz].reshape(bsz, 1, D_MODEL) for m in mods]
        mods_s = [jnp.repeat(m[bsz:], dseq, axis=0).reshape(1, ms, D_MODEL) for m in mods]
        final = l == depth - 1

        def attend_p(qr, kr, v, iqr, ikr, ik4, iwt, misc):
            return attn_prompt(qr, kr, v, iqr, ik4, iwt, bsz, seq)

        def attend_s(qr, kr, v, iqr, ikr, ik4, iwt, misc, l=l):
            return attn_sample(qr, kr, v, iqr, ikr, misc, ck, cv, cik, page_table, l, dseq)

        st_s = {'s5': jnp.concatenate([state_s5_re[l].reshape(dbs, S5_NS), state_s5_im[l].reshape(dbs, S5_NS)], axis=1),
                'ret': to_block_diag(state_ret[l]), 'gdn': to_block_diag(state_gdn[l]), 'conv': state_gdn_conv[l]}
        xp, ns_p = _trunk_layer(xp, mods_p, geom_p, attend_p, zero_st, dict(lw, **tabs_p), final_g, final)
        xs, ns_s = _trunk_layer(xs, mods_s, geom_s, attend_s, st_s, dict(lw, **tabs_s), final_g, final)
        outs_p.append(ns_p)
        outs_s.append(ns_s)

    def stack(outs, name, shape):
        return jnp.stack([o[name] for o in outs], axis=0).reshape((depth,) + shape)

    def states(outs, b, t):
        re = jnp.stack([o['s5'][:, :S5_NS] for o in outs], axis=0).reshape(depth, b, S5_GROUPS, S5_STATE)
        im = jnp.stack([o['s5'][:, S5_NS:] for o in outs], axis=0).reshape(depth, b, S5_GROUPS, S5_STATE)
        ret = jnp.stack([from_block_diag(o['ret']) for o in outs], axis=0)
        gdn = jnp.stack([from_block_diag(o['gdn']) for o in outs], axis=0)
        return (stack(outs, 'k', (b, t, KV_HEADS, HEAD_DIM)), stack(outs, 'v', (b, t, KV_HEADS, HEAD_DIM)),
                stack(outs, 'ik', (b, t, IDX_DIM)), re, im, ret, gdn,
                stack(outs, 'conv', (b, GDN_CONV - 1, 3 * BW)))

    kp, vp, ikp, rep, imp, retp, gdnp, convp = states(outs_p, bsz, seq)
    ks_, vs_, iks, res, ims, rets, gdns, convs = states(outs_s, dbs, dseq)
    return (xp.reshape(bsz, seq, D_MODEL), xs.reshape(dbs, dseq, D_MODEL), kp, vp, ikp, ks_, vs_, iks,
            rep, imp, res, ims, retp, rets, gdnp, gdns, convp, convs)
```

```python
import functools
import math

import numpy as np
import jax
import jax.numpy as jnp
from jax import lax
from jax.experimental import pallas as pl
from jax.experimental.pallas import tpu as pltpu

F32 = jnp.float32
BF16 = jnp.bfloat16

D_MODEL = 1024
DEPTH = 2
PAST_LEN = 8192
PAGE_SIZE = 128
ATT_HEADS = 8
KV_HEADS = 2
HEAD_DIM = 64
ROT_DIMS = HEAD_DIM // 4
ROPE_THETA = 500000.0
IDX_HEADS = 4
IDX_DIM = 32
IDX_ROT = IDX_DIM // 4
TOPK_MAX = 256
S5_GROUPS = 24
S5_GROUP_CH = 16
S5_STATE = 64
S5_WIDTH = S5_GROUPS * S5_GROUP_CH
S5_NS = S5_GROUPS * S5_STATE
RET_HEADS = 6
RET_DK = 64
RET_THETA = 10000.0
GDN_HEADS = 6
GDN_DK = 64
GDN_CONV = 4
GDN_CHUNK = 64
ATT_WIDTH = ATT_HEADS * HEAD_DIM
BW = 384
KVW = KV_HEADS * HEAD_DIM
N_BRANCH = 4
MIX_WIDTH = ATT_WIDTH + 3 * BW
N_EXPERTS = 16
N_GROUPS = 4
EXPERTS_PER_GROUP = 4
EXPERT_FF = 256
EPS = 1e-6

PROJ_COLS = 12 * BW
ATT_COL0 = 9 * BW
MISC_DA = 36
MISC_DB = 42
LANE = 128
VMEM_LIMIT = 56 * 1024 * 1024
NEG = -1e30


def _cp(sem):
    return pltpu.CompilerParams(dimension_semantics=sem, vmem_limit_bytes=VMEM_LIMIT)


def _bf(x):
    return x.astype(BF16)


def _dot(a, b):
    return jnp.dot(_bf(a), _bf(b), preferred_element_type=F32)


def _dot_nt(a, b):
    return lax.dot_general(_bf(a), _bf(b), (((1,), (1,)), ((), ())), preferred_element_type=F32)


def _dot_tn(a, b):
    return lax.dot_general(_bf(a), _bf(b), (((0,), (0,)), ((), ())), preferred_element_type=F32)


def _split(x):
    hi = x.astype(BF16)
    lo = (x - hi.astype(F32)).astype(BF16)
    return hi, lo


def _dot3(a, b):
    ah, al = _split(a)
    bh, bl = _split(b)
    d = functools.partial(jnp.dot, preferred_element_type=F32)
    return d(ah, bh) + (d(ah, bl) + d(al, bh))


def _dot2(a, b01):
    ah, al = _split(a)
    d = functools.partial(jnp.dot, preferred_element_type=F32)
    return d(ah, b01) + d(al, b01)


def _sigmoid(x):
    return 1.0 / (1.0 + jnp.exp(-x))


def _silu(x):
    return x * _sigmoid(x)


def _lane_iota(shape):
    return lax.broadcasted_iota(jnp.int32, shape, len(shape) - 1)


def _row_iota(shape):
    return lax.broadcasted_iota(jnp.int32, shape, len(shape) - 2)


def _ada_kernel(c_ref, w_ref, b_ref, o_ref):
    o_ref[...] = _dot(_silu(c_ref[...]), w_ref[...]) + b_ref[...]


def ada_mod(c, w_bf, b):
    n = c.shape[0]
    cols = w_bf.shape[1]
    tn = 1024
    return pl.pallas_call(
        _ada_kernel,
        grid=(cols // tn,),
        in_specs=[pl.BlockSpec((n, D_MODEL), lambda j: (0, 0)),
                  pl.BlockSpec((D_MODEL, tn), lambda j: (0, j)),
                  pl.BlockSpec((1, tn), lambda j: (0, j))],
        out_specs=pl.BlockSpec((n, tn), lambda j: (0, j)),
        out_shape=jax.ShapeDtypeStruct((n, cols), F32),
        compiler_params=_cp(("parallel",)),
        name="ada_mod",
    )(c, w_bf, b.reshape(1, cols))


def _norm_mod(x, g, sc, sh):
    y = x * lax.rsqrt(jnp.mean(x * x, axis=-1, keepdims=True) + EPS) * g
    return y * (1.0 + sc) + sh


def _in_kernel(x_ref, g_ref, sc_ref, sh_ref, w_ref, wm_ref, wt_ref, o_ref, om_ref, ot_ref, h_sc):
    @pl.when(pl.program_id(1) == 0)
    def _():
        h = _norm_mod(x_ref[...], g_ref[...], sc_ref[0], sh_ref[0])
        h_sc[...] = h.astype(BF16)
        hh, hl = _split(h)
        wmh, wml = _split(wm_ref[...])
        d = functools.partial(jnp.dot, preferred_element_type=F32)
        om_ref[...] = d(hh, wmh) + (d(hh, wml) + d(hl, wmh))
        wth, wtl = _split(wt_ref[...])
        nt = functools.partial(lax.dot_general, dimension_numbers=(((1,), (1,)), ((), ())),
                               preferred_element_type=F32)
        ot_ref[...] = nt(wth, hh) + (nt(wth, hl) + nt(wtl, hh))

    o_ref[...] = jnp.dot(h_sc[...], w_ref[...], preferred_element_type=F32)


def _mod_spec(tm, mod_rows, tiles_per_seq):
    if mod_rows == 1:
        return pl.BlockSpec((1, 1, D_MODEL), lambda i, *_: (i // tiles_per_seq, 0, 0))
    return pl.BlockSpec((1, tm, D_MODEL), lambda i, *_: (i, 0, 0))


def in_proj(x, g, sc, sh, w_bf, w_misc, w_iwt, tm, tiles_per_seq):
    m = x.shape[0]
    tn = 3 * BW * 2 if tm <= 256 else 3 * BW
    mod_rows = sc.shape[1]
    ms = _mod_spec(tm, mod_rows, tiles_per_seq)
    return pl.pallas_call(
        _in_kernel,
        grid=(m // tm, PROJ_COLS // tn),
        in_specs=[pl.BlockSpec((tm, D_MODEL), lambda i, j: (i, 0)),
                  pl.BlockSpec((1, D_MODEL), lambda i, j: (0, 0)),
                  ms, ms,
                  pl.BlockSpec((D_MODEL, tn), lambda i, j: (0, j)),
                  pl.BlockSpec((D_MODEL, LANE), lambda i, j: (0, 0)),
                  pl.BlockSpec((8, D_MODEL), lambda i, j: (0, 0))],
        out_specs=[pl.BlockSpec((tm, tn), lambda i, j: (i, j)),
                   pl.BlockSpec((tm, LANE), lambda i, j: (i, 0)),
                   pl.BlockSpec((8, tm), lambda i, j: (0, i))],
        out_shape=[jax.ShapeDtypeStruct((m, PROJ_COLS), F32),
                   jax.ShapeDtypeStruct((m, LANE), F32),
                   jax.ShapeDtypeStruct((8, m), F32)],
        scratch_shapes=[pltpu.VMEM((tm, D_MODEL), BF16)],
        compiler_params=_cp(("parallel", "arbitrary")),
        name="in_proj",
    )(x, g.reshape(1, D_MODEL), sc, sh, w_bf, w_misc, w_iwt)


def _rope_tables(pos, rot_dims, theta, period, width=LANE, active=None):
    half = rot_dims // 2
    inv_freq = jnp.power(jnp.float32(theta), -jnp.arange(half, dtype=F32) / half)
    ang = pos.astype(F32)[:, None] * inv_freq
    cos, sin = jnp.cos(ang), jnp.sin(ang)
    t = pos.shape[0]
    c = jnp.concatenate([cos, cos, jnp.ones((t, period - rot_dims), F32)], axis=1)
    s_up = jnp.concatenate([-sin, jnp.zeros((t, period - half), F32)], axis=1)
    s_dn = jnp.concatenate([jnp.zeros((t, half), F32), sin, jnp.zeros((t, period - rot_dims), F32)], axis=1)
    reps = width // period
    tab = jnp.stack([jnp.tile(a, (1, reps)) for a in (c, s_up, s_dn)], axis=0)
    if active is not None:
        ident = jnp.stack([jnp.ones((t, width), F32), jnp.zeros((t, width), F32),
                           jnp.zeros((t, width), F32)], axis=0)
        tab = jnp.where(jnp.arange(width) < active, tab, ident)
    return tab


def _rope(x, tab_ref, half):
    w = x.shape[1]
    reps = w // LANE

    def wide(k):
        t = tab_ref[k]
        return t if reps == 1 else jnp.concatenate([t] * reps, axis=1)

    return (x * wide(0) + pltpu.roll(x, w - half, axis=1) * wide(1)
            + pltpu.roll(x, half, axis=1) * wide(2))


def _prep_kernel(p_ref, m_ref, ta_ref, ti_ref, tk_ref, q_ref, k_ref, v_ref, iq_ref, ik_ref, ik4_ref, vt_ref):
    q_ref[...] = _rope(p_ref[:, 0:ATT_WIDTH], ta_ref, ROT_DIMS // 2)
    k_ref[...] = _rope(p_ref[:, 512:640], ta_ref, ROT_DIMS // 2)
    v_ref[...] = p_ref[:, 640:768]
    vt_ref[0] = p_ref[:, 640:768].T
    iq_ref[...] = _rope(p_ref[:, 768:896], ti_ref, IDX_ROT // 2)
    ikr = _rope(m_ref[...], tk_ref, IDX_ROT // 2)
    ik_ref[...] = ikr[:, 0:IDX_DIM]
    m = jnp.where(_lane_iota(ikr.shape) < IDX_DIM, ikr, 0.0)
    ik4_ref[...] = (m + pltpu.roll(m, 32, axis=1)) + (pltpu.roll(m, 64, axis=1) + pltpu.roll(m, 96, axis=1))


def attn_prep(proj, misc, tab_a, tab_i, tab_k, tm, tiles_per_seq):
    m = proj.shape[0]
    tspec = pl.BlockSpec((3, tm, LANE), lambda i: (0, i % tiles_per_seq, 0))
    widths = (ATT_WIDTH, KVW, KVW, LANE, IDX_DIM, LANE)
    seq = tm * tiles_per_seq
    return pl.pallas_call(
        _prep_kernel,
        grid=(m // tm,),
        in_specs=[pl.BlockSpec((tm, 3 * BW), lambda i: (i, ATT_COL0 // (3 * BW))),
                  pl.BlockSpec((tm, LANE), lambda i: (i, 0)), tspec, tspec, tspec],
        out_specs=[pl.BlockSpec((tm, w), lambda i: (i, 0)) for w in widths]
        + [pl.BlockSpec((1, KVW, tm), lambda i: (i // tiles_per_seq, 0, i % tiles_per_seq))],
        out_shape=[jax.ShapeDtypeStruct((m, w), F32) for w in widths]
        + [jax.ShapeDtypeStruct((m // seq, KVW, seq), F32)],
        compiler_params=_cp(("parallel",)),
        name="attn_prep",
    )(proj, misc, tab_a, tab_i, tab_k)


BISECT_MAX_ITERS = 48
BISECT_UNROLL = 4
FAR = 2.0 ** 126


def _count(ones, axis):
    return jnp.sum(ones, axis=axis, keepdims=True)


COUNT_ROWS = 32
COUNT_ACCS = 4


def _count_where(s_ref, pred, axis):
    n = s_ref.shape[0]
    if axis != 0 or n % (COUNT_ROWS * COUNT_ACCS) != 0:
        return _count(pred(s_ref[...]), axis)
    accs = [None] * COUNT_ACCS
    for j, i in enumerate(range(0, n, COUNT_ROWS)):
        part = pred(s_ref[i:i + COUNT_ROWS, :])
        a = j % COUNT_ACCS
        accs[a] = part if accs[a] is None else accs[a] + part
    return jnp.sum((accs[0] + accs[1]) + (accs[2] + accs[3]), axis=0, keepdims=True)


def _bisect_topk(s_ref, axis, topk):
    kshape = tuple(1 if a == axis else n for a, n in enumerate(s_ref.shape))
    s = s_ref[...]
    lo0 = jnp.min(jnp.where(s > -FAR, s, FAR), axis=axis, keepdims=True)
    mx = jnp.max(s, axis=axis, keepdims=True)
    hi0 = mx + (jnp.abs(mx) * 2.0 ** -20 + 1e-30)
    cnt_lo0 = _count_where(s_ref, lambda t: jnp.where(t >= lo0, 1, 0), axis)
    n_zero = _count_where(s_ref, lambda t: jnp.where(t == 0.0, 1, 0), axis)

    def pending(lo, hi, cnt_lo, cnt_hi):
        only_zeros = jnp.where(lo <= 0.0, jnp.where(hi > 0.0, jnp.where(cnt_lo - cnt_hi == n_zero, 1, 0), 0), 0)
        return jnp.max(jnp.where(cnt_lo <= topk, 0, 1 - only_zeros))

    def cond(c):
        return jnp.logical_and(c[0] < BISECT_MAX_ITERS, c[1] > 0)

    def body(c):
        it, _, lo, hi, cnt_lo, cnt_hi = c
        for _ in range(BISECT_UNROLL):
            mid = 0.5 * lo + 0.5 * hi
            cm = _count_where(s_ref, lambda t: jnp.where(t >= mid, 1, 0), axis)
            ge = cm >= topk
            lo, hi = jnp.where(ge, mid, lo), jnp.where(ge, hi, mid)
            cnt_lo, cnt_hi = jnp.where(ge, cm, cnt_lo), jnp.where(ge, cnt_hi, cm)
        return it + BISECT_UNROLL, pending(lo, hi, cnt_lo, cnt_hi), lo, hi, cnt_lo, cnt_hi

    cnt_hi0 = jnp.zeros(kshape, jnp.int32)
    init = (jnp.int32(0), pending(lo0, hi0, cnt_lo0, cnt_hi0), lo0, hi0, cnt_lo0, cnt_hi0)
    _, _, lo, hi, cnt_lo, cnt_hi = lax.while_loop(cond, body, init)
    return lo, hi, cnt_lo, cnt_hi


def _topk_select(s_ref, idx, axis, n_idx_bits, p_sc, topk):
    kshape = tuple(1 if a == axis else n for a, n in enumerate(s_ref.shape))
    lo, hi, cnt_lo, cnt_hi = _bisect_topk(s_ref, axis, topk)
    need = topk - cnt_hi
    p_sc[...] = jnp.full(kshape, (1 << n_idx_bits) - 1, jnp.int32)

    @pl.when(jnp.max(cnt_lo - cnt_hi - need) > 0)
    def _():
        tied = jnp.where(s_ref[...] >= lo, jnp.where(s_ref[...] >= hi, 0, 1), 0)

        def ibody(i, p):
            cand = p + jnp.left_shift(jnp.int32(1), n_idx_bits - 1 - i)
            taken = _count(jnp.where(idx < cand, tied, 0), axis)
            return jnp.where(taken < need, cand, p)

        p_sc[...] = lax.fori_loop(0, n_idx_bits, ibody, jnp.zeros(kshape, jnp.int32))

    s = s_ref[...]
    return jnp.where(s >= hi, 1, jnp.where(s >= lo, jnp.where(idx <= p_sc[...], 1, 0), 0))


def _topk_bias_keys_major(s_ref, tri_ref, topk):
    tk, tq = s_ref.shape
    lo, hi, _, cnt_hi = _bisect_topk(s_ref, 0, topk)
    need = (topk - cnt_hi).astype(F32)
    tri = tri_ref[...]
    offset = jnp.zeros((1, tq), F32)
    parts = []
    for c in range(tk // tri.shape[0]):
        s = s_ref[c * tri.shape[0]:(c + 1) * tri.shape[0], :]
        cand = jnp.where(s >= lo, jnp.where(s >= hi, 0.0, 1.0), 0.0)
        rank = jnp.dot(tri, _bf(cand), preferred_element_type=F32) + offset
        offset = rank[tri.shape[0] - 1:tri.shape[0], :]
        parts.append(jnp.where(s >= hi, 0.0, jnp.where(cand * rank > 0.5, jnp.where(rank <= need, 0.0, NEG), NEG)))
    return jnp.concatenate(parts, axis=0)


def _group_queries(q, g):
    tiles = []
    keep = (_lane_iota((q.shape[0], LANE)) // HEAD_DIM) == g
    for hl in range(ATT_HEADS // KV_HEADS):
        h = g * (ATT_HEADS // KV_HEADS) + hl
        t = q[:, (h // 2) * LANE:(h // 2 + 1) * LANE]
        if h % 2 != g:
            t = pltpu.roll(t, HEAD_DIM, axis=1)
        tiles.append(jnp.where(keep, t, 0.0))
    return jnp.concatenate(tiles, axis=0)


def _ungroup_outputs(o_groups, tq):
    low = _lane_iota((tq, LANE)) < HEAD_DIM
    tiles = []
    for j in range(ATT_HEADS // 2):
        halves = []
        for h in (2 * j, 2 * j + 1):
            g, hl = divmod(h, ATT_HEADS // KV_HEADS)
            t = o_groups[g][hl * tq:(hl + 1) * tq]
            if h % 2 != g:
                t = pltpu.roll(t, HEAD_DIM, axis=1)
            halves.append(t)
        tiles.append(jnp.where(low, halves[0], halves[1]))
    return jnp.concatenate(tiles, axis=1)


def _masked_attention(q, k, v, bias):
    tq = q.shape[0]
    kb, vb = _bf(k), _bf(v)
    outs = []
    for g in range(KV_HEADS):
        qg = _group_queries(q, g)
        s = _dot_nt(qg, kb) * HEAD_DIM ** -0.5
        s = s + jnp.concatenate([bias] * (ATT_HEADS // KV_HEADS), axis=0)
        p = jnp.exp(s - jnp.max(s, axis=-1, keepdims=True))
        l = jnp.sum(p, axis=-1, keepdims=True)
        outs.append(_dot(p, vb) / l)
    return _ungroup_outputs(outs, tq)


def _masked_attention_keys_major(q, kb, vt, bias_t):
    tq = q.shape[0]
    heads = ATT_HEADS // KV_HEADS
    tk = kb.shape[0]
    bias4 = jnp.concatenate([bias_t] * heads, axis=1)
    vrow_group = _row_iota((LANE, tk)) // HEAD_DIM
    q = q * HEAD_DIM ** -0.5
    normed = []
    for g in range(KV_HEADS):
        st = _dot_nt(kb, _group_queries(q, g)) + bias4
        p = jnp.exp(_bf(st - jnp.max(st, axis=0, keepdims=True)))
        ot = jnp.dot(_bf(jnp.where(vrow_group == g, vt, 1.0)), p, preferred_element_type=F32)
        other = (1 - g) * HEAD_DIM
        normed.append(ot[g * HEAD_DIM:(g + 1) * HEAD_DIM, :] / ot[other:other + 1, :])
    tiles = []
    for j in range(ATT_HEADS // 2):
        g, hl = divmod(2 * j, heads)
        pair = jnp.concatenate([normed[g][:, hl * tq:(hl + 1) * tq], normed[g][:, (hl + 1) * tq:(hl + 2) * tq]], axis=0)
        tiles.append(pair.T)
    return jnp.concatenate(tiles, axis=1)


def _attn_prompt_kernel(q_ref, iq_ref, iwt_ref, k_ref, vt_ref, ik4_ref, tri_ref, o_ref, s_sc, *, qblk0, tq, topk):
    tk = k_ref.shape[1]
    q0 = (qblk0 + pl.program_id(1)) * tq
    iq = iq_ref[0]
    head_of_lane = _lane_iota((tq, LANE)) // IDX_DIM
    iq4 = jnp.concatenate([jnp.where(head_of_lane == h, iq, 0.0) for h in range(IDX_HEADS)], axis=0)
    lg = _dot_nt(ik4_ref[0], iq4)
    iwt = iwt_ref[...] * (IDX_HEADS ** -0.5 * IDX_DIM ** -0.5)
    score = None
    for h in range(IDX_HEADS):
        part = jnp.maximum(lg[:, h * tq:(h + 1) * tq], 0.0) * iwt[h:h + 1, :]
        score = part if score is None else score + part
    adm = _row_iota((tk, tq)) <= q0 + _lane_iota((tk, tq))
    s_sc[...] = jnp.where(adm, score, -FAR)
    bias_t = _topk_bias_keys_major(s_sc, tri_ref, topk)
    o_ref[0] = _masked_attention_keys_major(q_ref[0], _bf(k_ref[0]), vt_ref[0], bias_t)


PREFIX_ROWS = 256


def attn_prompt(qr, kr, vt, iqr, ik4, iwt, bsz, seq, n_classes=4, tq=128):
    topk = min(TOPK_MAX, seq // 4)
    nq = seq // tq
    per = max(1, nq // n_classes)
    q3 = qr.reshape(bsz, seq, ATT_WIDTH)
    iq3 = iqr.reshape(bsz, seq, LANE)
    k3, ik3 = (a.reshape(bsz, seq, LANE) for a in (kr, ik4))
    outs = []
    for c in range(nq // per):
        tk = (c + 1) * per * tq
        qb0 = c * per
        r = np.arange(math.gcd(PREFIX_ROWS, tk))
        tri = jnp.asarray(r[:, None] >= r[None, :], BF16)
        out = pl.pallas_call(
            functools.partial(_attn_prompt_kernel, qblk0=qb0, tq=tq, topk=topk),
            grid=(bsz, per),
            in_specs=[pl.BlockSpec((1, tq, ATT_WIDTH), lambda b, j, qb0=qb0: (b, qb0 + j, 0)),
                      pl.BlockSpec((1, tq, LANE), lambda b, j, qb0=qb0: (b, qb0 + j, 0)),
                      pl.BlockSpec((8, tq), lambda b, j, qb0=qb0: (0, b * nq + qb0 + j)),
                      pl.BlockSpec((1, tk, LANE), lambda b, j: (b, 0, 0)),
                      pl.BlockSpec((1, LANE, tk), lambda b, j: (b, 0, 0)),
                      pl.BlockSpec((1, tk, LANE), lambda b, j: (b, 0, 0)),
                      pl.BlockSpec(tri.shape, lambda b, j: (0, 0))],
            out_specs=pl.BlockSpec((1, tq, ATT_WIDTH), lambda b, j: (b, j, 0)),
            out_shape=jax.ShapeDtypeStruct((bsz, per * tq, ATT_WIDTH), F32),
            scratch_shapes=[pltpu.VMEM((tk, tq), F32)],
            compiler_params=_cp(("parallel", "arbitrary")),
            name=f"attn_prompt_{tk}",
        )(q3, iq3, iwt, k3, vt, ik3, tri)
        outs.append(out)
    return jnp.concatenate(outs, axis=1).reshape(bsz * seq, ATT_WIDTH)


def _attn_sample_kernel(pt_ref, q_ref, iq_ref, misc_ref, kn_ref, vn_ref, ikn_ref, *rest, npg, topk):
    del pt_ref
    kp, vp, ikp = rest[0:npg], rest[npg:2 * npg], rest[2 * npg:3 * npg]
    o_ref, kbuf, vbuf, ikbuf, key_sc, p_sc = rest[3 * npg:]
    tq = q_ref.shape[1]
    past = kbuf.shape[0] - LANE
    p = pl.program_id(1)
    for j in range(npg):
        row = pl.multiple_of((p * npg + j) * PAGE_SIZE, PAGE_SIZE)
        kbuf[pl.ds(row, PAGE_SIZE), :] = kp[j][0]
        vbuf[pl.ds(row, PAGE_SIZE), :] = vp[j][0]
        ikbuf[pl.ds(row, PAGE_SIZE), :] = ikp[j][0]

    @pl.when(p == pl.num_programs(1) - 1)
    def _():
        for buf, new in ((kbuf, kn_ref), (vbuf, vn_ref), (ikbuf, ikn_ref)):
            buf[past:past + LANE, :] = jnp.zeros((LANE, buf.shape[1]), F32)
            buf[past:past + tq, :] = new[0]
        lk = kbuf.shape[0]
        iq = iq_ref[0]
        iqs = jnp.concatenate([iq[:, h * IDX_DIM:(h + 1) * IDX_DIM] for h in range(IDX_HEADS)], axis=0)
        misc = misc_ref[0]
        iw = jnp.concatenate([misc[:, IDX_DIM + h:IDX_DIM + h + 1] for h in range(IDX_HEADS)], axis=0)
        lg = _dot_nt(iqs, ikbuf[...]) * IDX_DIM ** -0.5
        wl = jnp.maximum(lg, 0.0) * (iw * IDX_HEADS ** -0.5)
        score = (wl[0:tq] + wl[tq:2 * tq]) + (wl[2 * tq:3 * tq] + wl[3 * tq:4 * tq])
        s_idx = _lane_iota((tq, lk))
        adm = s_idx <= past + _row_iota((tq, lk))
        key_sc[...] = jnp.where(adm, score, -FAR)
        sel = _topk_select(key_sc, s_idx, 1, (lk - 1).bit_length(), p_sc, topk)
        bias = jnp.where(sel > 0, 0.0, NEG)
        o_ref[0] = _masked_attention(q_ref[0], kbuf[...], vbuf[...], bias)


def attn_sample(qr, kr, v, iqr, ikr, misc, cache_k, cache_v, cache_ik, page_table, layer, dseq, npg=8):
    db, n_pages = page_table.shape
    n_pool = cache_k.shape[0] // DEPTH
    past = n_pages * PAGE_SIZE
    lk = past + LANE
    topk = min(TOPK_MAX, (past + dseq) // 4)
    base = layer * n_pool
    r3 = lambda a: a.reshape(db, dseq, a.shape[-1])
    row_spec = lambda w, col=0: pl.BlockSpec((1, dseq, w), lambda b, p, pt, col=col: (b, 0, col))

    def page_spec(w, j):
        return pl.BlockSpec((1, PAGE_SIZE, w), lambda b, p, pt, j=j: (pt[b, p * npg + j] + base, 0, 0))

    in_specs = [row_spec(ATT_WIDTH), row_spec(LANE), row_spec(LANE),
                row_spec(LANE), row_spec(LANE), row_spec(IDX_DIM)]
    in_specs += [page_spec(LANE, j) for j in range(npg)]
    in_specs += [page_spec(LANE, j) for j in range(npg)]
    in_specs += [page_spec(IDX_DIM, j) for j in range(npg)]
    out = pl.pallas_call(
        functools.partial(_attn_sample_kernel, npg=npg, topk=topk),
        grid_spec=pltpu.PrefetchScalarGridSpec(
            num_scalar_prefetch=1,
            grid=(db, n_pages // npg),
            in_specs=in_specs,
            out_specs=pl.BlockSpec((1, dseq, ATT_WIDTH), lambda b, p, pt: (b, 0, 0)),
            scratch_shapes=[pltpu.VMEM((lk, LANE), F32), pltpu.VMEM((lk, LANE), F32),
                            pltpu.VMEM((lk, IDX_DIM), F32), pltpu.VMEM((dseq, lk), F32),
                            pltpu.VMEM((dseq, 1), jnp.int32)]),
        out_shape=jax.ShapeDtypeStruct((db, dseq, ATT_WIDTH), F32),
        compiler_params=_cp(("parallel", "arbitrary")),
        name="attn_sample",
    )(page_table, r3(qr), r3(iqr), r3(misc), r3(kr), r3(v), r3(ikr),
      *([cache_k] * npg), *([cache_v] * npg), *([cache_ik] * npg))
    return out.reshape(db * dseq, ATT_WIDTH)


def _s5_param_kernel(ldt_ref, are_ref, aim_ref, bre_ref, bim_ref, abar_ref, win_ref):
    dt = jnp.exp(ldt_ref[...])
    a_re, a_im = are_ref[...], aim_ref[...]
    mag = jnp.exp(dt * a_re)
    abar_re = mag * jnp.cos(dt * a_im)
    abar_im = mag * jnp.sin(dt * a_im)
    den = a_re * a_re + a_im * a_im
    num_re = abar_re - 1.0
    coef_re = (num_re * a_re + abar_im * a_im) / den
    coef_im = (abar_im * a_re - num_re * a_im) / den
    abar_ref[:, 0:S5_NS] = jnp.broadcast_to(abar_re, (8, S5_NS))
    abar_ref[:, S5_NS:2 * S5_NS] = jnp.broadcast_to(abar_im, (8, S5_NS))
    b_re, b_im = bre_ref[...], bim_ref[...]
    win_ref[:, 0:S5_NS] = _bf(coef_re * b_re - coef_im * b_im)
    win_ref[:, S5_NS:2 * S5_NS] = _bf(coef_re * b_im + coef_im * b_re)


def _block_diag_in(b):
    eye = jnp.eye(S5_GROUPS, dtype=b.dtype)
    return jnp.einsum('gnc,gh->gchn', b, eye).reshape(S5_WIDTH, S5_NS)


def _block_diag_out(c):
    eye = jnp.eye(S5_GROUPS, dtype=c.dtype)
    return jnp.einsum('gcn,gh->gnhc', c, eye).reshape(S5_NS, S5_WIDTH)


def s5_params(log_dt, a_re, a_im, b_re, b_im):
    per_state = lambda a: a.reshape(1, S5_NS)
    ldt = per_state(jnp.broadcast_to(log_dt[:, None], (S5_GROUPS, S5_STATE)))
    return pl.pallas_call(
        _s5_param_kernel,
        out_shape=[jax.ShapeDtypeStruct((8, 2 * S5_NS), F32),
                   jax.ShapeDtypeStruct((S5_WIDTH, 2 * S5_NS), BF16)],
        compiler_params=pltpu.CompilerParams(vmem_limit_bytes=VMEM_LIMIT),
        name="s5_params",
    )(ldt, per_state(a_re), per_state(a_im), _block_diag_in(b_re), _block_diag_in(b_im))


def _gelu_tanh(x):
    return 0.5 * x * (1.0 + jnp.tanh(math.sqrt(2.0 / math.pi) * (x + 0.044715 * (x * x * x))))


S5_LANES = 512


def _s5_kernel(u_ref, abar_ref, win_ref, h0_ref, wout_ref, d_ref, wglu_ref, bglu_ref,
               y_ref, hn_ref, s_sc, *, bsz, tc):
    c = pl.program_id(0)

    @pl.when(c == 0)
    def _():
        hn_ref[...] = h0_ref[...]

    u = u_ref[...]
    s_sc[...] = jnp.dot(_bf(u), win_ref[...], preferred_element_type=F32)
    nchunk = S5_NS // S5_LANES
    for rg in range(bsz // 8):
        rows = slice(rg * 8, rg * 8 + 8)

        def body(t, carry):
            row0 = pl.multiple_of(t * bsz + rg * 8, 8)
            new = []
            for cc in range(nchunk):
                lre = slice(cc * S5_LANES, (cc + 1) * S5_LANES)
                lim = slice(S5_NS + cc * S5_LANES, S5_NS + (cc + 1) * S5_LANES)
                xr, xi = carry[2 * cc], carry[2 * cc + 1]
                ar, ai = abar_ref[:, lre], abar_ref[:, lim]
                nr = (ar * xr - ai * xi) + s_sc[pl.ds(row0, 8), lre]
                ni = (ar * xi + ai * xr) + s_sc[pl.ds(row0, 8), lim]
                s_sc[pl.ds(row0, 8), lre] = nr
                s_sc[pl.ds(row0, 8), lim] = ni
                new += [nr, ni]
            return tuple(new)

        init = []
        for cc in range(nchunk):
            init += [hn_ref[rows, cc * S5_LANES:(cc + 1) * S5_LANES],
                     hn_ref[rows, S5_NS + cc * S5_LANES:S5_NS + (cc + 1) * S5_LANES]]
        fin = lax.fori_loop(0, tc, body, tuple(init))
        for cc in range(nchunk):
            hn_ref[rows, cc * S5_LANES:(cc + 1) * S5_LANES] = fin[2 * cc]
            hn_ref[rows, S5_NS + cc * S5_LANES:S5_NS + (cc + 1) * S5_LANES] = fin[2 * cc + 1]

    y = (jnp.dot(_bf(s_sc[:, 0:S5_NS]), wout_ref[0:S5_NS, :], preferred_element_type=F32)
         - jnp.dot(_bf(s_sc[:, S5_NS:2 * S5_NS]), wout_ref[S5_NS:2 * S5_NS, :], preferred_element_type=F32)
         + d_ref[...] * u)
    z = _gelu_tanh(y)
    y_ref[...] = z * _sigmoid(jnp.dot(_bf(z), wglu_ref[...], preferred_element_type=F32) + bglu_ref[...])


def s5_branch(u_tm, h0, abar8, win, wout_bf, d, wglu_bf, bglu, bsz, seq, tc):
    rows = tc * bsz
    const = lambda shape: pl.BlockSpec(shape, lambda c: (0,) * len(shape))
    return pl.pallas_call(
        functools.partial(_s5_kernel, bsz=bsz, tc=tc),
        grid=(seq // tc,),
        in_specs=[pl.BlockSpec((rows, S5_WIDTH), lambda c: (c, 0)),
                  const((8, 2 * S5_NS)), const((S5_WIDTH, 2 * S5_NS)), const((bsz, 2 * S5_NS)),
                  const((2 * S5_NS, S5_WIDTH)), const((1, S5_WIDTH)), const((S5_WIDTH, S5_WIDTH)),
                  const((1, S5_WIDTH))],
        out_specs=[pl.BlockSpec((rows, S5_WIDTH), lambda c: (c, 0)), const((bsz, 2 * S5_NS))],
        out_shape=[jax.ShapeDtypeStruct((seq * bsz, S5_WIDTH), F32),
                   jax.ShapeDtypeStruct((bsz, 2 * S5_NS), F32)],
        scratch_shapes=[pltpu.VMEM((rows, 2 * S5_NS), F32)],
        compiler_params=_cp(("arbitrary",)),
        name="s5_branch",
    )(u_tm, abar8, win, h0, wout_bf, d.reshape(1, S5_WIDTH), wglu_bf, bglu.reshape(1, S5_WIDTH))


def _pad_rows(x, rows):
    n = x.shape[0]
    return x if n == rows else jnp.concatenate([x, jnp.zeros((rows - n, x.shape[1]), x.dtype)], axis=0)


def _head_mean(x, amat_bf):
    return _dot2(x, amat_bf)


def _ret_kernel(q_ref, k_ref, v_ref, g_ref, tab_ref, s0_ref, dmat_ref, qdec_ref, kdec_ref, decm_ref,
                bdm_ref, amat_ref, y_ref, s_ref, *, rows):
    n = q_ref.shape[0]

    @pl.when(pl.program_id(1) == 0)
    def _():
        s_ref[...] = s0_ref[...]

    q = _pad_rows(_rope(q_ref[...], tab_ref, RET_DK // 2), rows)
    k = _pad_rows(_rope(k_ref[...], tab_ref, RET_DK // 2) * RET_DK ** -0.5, rows)
    v = _pad_rows(v_ref[...], rows)
    state = s_ref[0]
    inter = _dot(q * qdec_ref[...], state)
    lane = _lane_iota((rows, LANE))
    tiles = []
    for p in range(RET_HEADS // 2):
        lanes = slice(p * LANE, (p + 1) * LANE)
        qp, kp, vp = q[:, lanes], _bf(k[:, lanes]), v[:, lanes]
        acc = None
        for hh in range(2):
            mine = (lane < RET_DK) if hh == 0 else (lane >= RET_DK)
            s = _dot_nt(jnp.where(mine, qp, 0.0), kp) * dmat_ref[2 * p + hh]
            part = _dot(s, jnp.where(mine, vp, 0.0))
            acc = part if acc is None else acc + part
        tiles.append(acc)
    o = jnp.concatenate(tiles, axis=1) + inter
    s_ref[0] = state * decm_ref[...] + _dot_tn(k * kdec_ref[...], v) * bdm_ref[...]
    amat = amat_ref[...]
    mu = _head_mean(o, amat)
    d = o - mu
    var = _head_mean(d * d, amat)
    on = d * lax.rsqrt(var + 1e-5)
    y_ref[...] = (_silu(g_ref[...]) * on[0:n]).astype(y_ref.dtype)


def _head_block_mask():
    h = np.arange(BW) // 64
    return (h[:, None] == h[None, :]).astype(np.float32)


def _ret_consts(rows, n_true):
    lg = np.log(1.0 - np.exp2(-5.0 - np.arange(RET_HEADS, dtype=np.float64)))
    i = np.arange(rows, dtype=np.float64)
    rel = i[:, None] - i[None, :]
    dmat = np.where(rel[None] >= 0, np.exp(np.minimum(rel[None], rows) * lg[:, None, None]), 0.0)
    lane_lg = np.repeat(lg, 64)[None, :]
    qdec = np.exp((i[:, None] + 1.0) * lane_lg)
    kdec = np.where(i[:, None] < n_true, np.exp((n_true - 1.0 - i[:, None]) * lane_lg), 0.0)
    bdm = _head_block_mask()
    decm = bdm * np.exp(n_true * np.repeat(lg, 64))[:, None]
    f = lambda a: jnp.asarray(a, F32)
    return f(dmat), f(qdec), f(kdec), f(decm), f(bdm), jnp.asarray(bdm / 64.0, BF16)


def ret_branch(proj, tab_r, s0_bd, bsz, seq, rows, n):
    nch = seq // n
    dmat, qdec, kdec, decm, bdm, amat = _ret_consts(rows, n)
    col = lambda j: pl.BlockSpec((n, BW), lambda b, c, j=j: (b * nch + c, j))
    const = lambda shape: pl.BlockSpec(shape, lambda b, c: (0,) * len(shape))
    return pl.pallas_call(
        functools.partial(_ret_kernel, rows=rows),
        grid=(bsz, nch),
        in_specs=[col(0), col(1), col(2), col(3),
                  pl.BlockSpec((3, n, LANE), lambda b, c: (0, c, 0)),
                  pl.BlockSpec((1, BW, BW), lambda b, c: (b, 0, 0)),
                  const((RET_HEADS, rows, rows)), const((rows, BW)), const((rows, BW)),
                  const((BW, BW)), const((BW, BW)), const((BW, BW))],
        out_specs=[pl.BlockSpec((n, BW), lambda b, c: (b * nch + c, 0)),
                   pl.BlockSpec((1, BW, BW), lambda b, c: (b, 0, 0))],
        out_shape=[jax.ShapeDtypeStruct((bsz * seq, BW), F32),
                   jax.ShapeDtypeStruct((bsz, BW, BW), F32)],
        compiler_params=_cp(("parallel", "arbitrary")),
        name="ret_branch",
    )(proj, proj, proj, proj, tab_r, s0_bd, dmat, qdec, kdec, decm, bdm, amat)


def to_block_diag(s):
    b = s.shape[0]
    eye = jnp.eye(s.shape[1], dtype=s.dtype)
    return jnp.einsum('bhde,hg->bhdge', s, eye).reshape(b, BW, BW)


def from_block_diag(s_bd):
    b = s_bd.shape[0]
    s5 = s_bd.reshape(b, BW // 64, 64, BW // 64, 64)
    idx = jnp.arange(BW // 64)
    return s5[:, idx, :, idx, :].transpose(1, 0, 2, 3)


def _softplus(x):
    return jnp.maximum(x, 0.0) + jnp.log(1.0 + jnp.exp(-jnp.abs(x)))


GDN_SUPER = ((0, 4 * GDN_CHUNK), (4 * GDN_CHUNK, 6 * GDN_CHUNK))
GM_BLOCK, GM_INCL, GM_STRICT, GM_EYE, GM_LEVEL0 = 0, 1, 2, 3, 4


def _stack_heads(a, bdm_rows):
    return jnp.concatenate([a] * (bdm_rows.shape[0] // GDN_CHUNK), axis=0) * bdm_rows


def _unstack_heads(parts):
    blocks = [p[i:i + GDN_CHUNK] for p in parts for i in range(0, p.shape[0], GDN_CHUNK)]
    out = blocks[0]
    for b in blocks[1:]:
        out = out + b
    return out


def _col_of_heads(a, s, e):
    return jnp.concatenate([jnp.broadcast_to(a[:, h * GDN_DK:h * GDN_DK + 1], (GDN_CHUNK, e - s))
                            for h in range(s // GDN_CHUNK, e // GDN_CHUNK)], axis=0)


def _row_of_heads(a_t, s, e):
    return jnp.concatenate([a_t[h * GDN_DK:h * GDN_DK + 1, :] for h in range(s // GDN_CHUNK, e // GDN_CHUNK)],
                           axis=1)


def _gdn_prepare(chunks, gm_ref):
    items = [(ci, s, e) for ci in range(len(chunks)) for s, e in GDN_SUPER]
    g_ts = [gc.T for _, _, _, _, gc in chunks]
    nmats, decs, qks = [], [], []
    for ci, s, e in items:
        qc, kc, _, bc, gc = chunks[ci]
        bdm_rows = gm_ref[GM_BLOCK, s:e, :]
        ks = _bf(_stack_heads(kc, bdm_rows))
        kk = _dot_nt(ks, ks)
        qks.append(_dot_nt(_stack_heads(qc, bdm_rows), ks))
        diff = _col_of_heads(gc, s, e) - _row_of_heads(g_ts[ci], s, e)
        dec = jnp.exp(jnp.where(gm_ref[GM_INCL, s:e, s:e] > 0.5, diff, NEG))
        decs.append(dec)
        nmats.append(_col_of_heads(bc, s, e) * (dec * gm_ref[GM_STRICT, s:e, s:e]) * kk)
    invs = [gm_ref[GM_EYE, s:e, s:e] - nm * gm_ref[GM_LEVEL0, s:e, s:e] for nm, (_, s, e) in zip(nmats, items)]
    for lvl in range(1, (GDN_CHUNK - 1).bit_length()):
        right = [_dot(nm * gm_ref[GM_LEVEL0 + lvl, s:e, s:e], inv) for nm, inv, (_, s, e) in zip(nmats, invs, items)]
        invs = [inv - _dot(inv, r) for inv, r in zip(invs, right)]
    w_st, uv_st = [[] for _ in chunks], [[] for _ in chunks]
    for inv, (ci, s, e) in zip(invs, items):
        _, kc, vc, bc, gc = chunks[ci]
        bdm_rows = gm_ref[GM_BLOCK, s:e, :]
        wu = _dot(inv, jnp.concatenate([_stack_heads(bc * jnp.exp(gc) * kc, bdm_rows),
                                        _stack_heads(bc * vc, bdm_rows)], axis=1))
        w_st[ci].append(wu[:, 0:BW])
        uv_st[ci].append(wu[:, BW:2 * BW])
    n_sb = len(GDN_SUPER)
    return [(_unstack_heads(w_st[ci]), _unstack_heads(uv_st[ci]),
             [qks[ci * n_sb + j] * decs[ci * n_sb + j] for j in range(n_sb)]) for ci in range(len(chunks))]


def _gdn_apply(chunk, prepared, hbd, gm_ref):
    qc, kc, _, _, gc = chunk
    w, uv, a_mats = prepared
    g_last = gc[GDN_CHUNK - 1:GDN_CHUNK, :]
    u = uv - _dot(w, hbd)
    o_st = [_dot(a_mats[i], _stack_heads(u, gm_ref[GM_BLOCK, s:e, :])) for i, (s, e) in enumerate(GDN_SUPER)]
    o = jnp.exp(gc) * _dot(qc, hbd) + _unstack_heads(o_st)
    h_new = jnp.exp(g_last) * hbd + _dot_tn(kc * jnp.exp(g_last - gc), u) * gm_ref[GM_BLOCK]
    return o, h_new


def _gdn_kernel(q_ref, k_ref, v_ref, g_ref, misc_ref, cs0_ref, cw_ref, alog_ref, dtb_ref, ng_ref, h0_ref,
                ea_ref, eb_ref, gm_ref, tri_ref, amat_ref, y_ref, h_ref, cs_ref, xp_sc, *, rows):
    n = q_ref.shape[0]
    cw = 3 * BW

    @pl.when(pl.program_id(1) == 0)
    def _():
        xp_sc[...] = jnp.zeros(xp_sc.shape, F32)
        xp_sc[5:8, :] = cs0_ref[0]
        h_ref[...] = h0_ref[...]

    for j, r in enumerate((q_ref, k_ref, v_ref)):
        xp_sc[8:8 + n, j * BW:(j + 1) * BW] = r[...]
    conv = xp_sc[5:5 + rows, :] * cw_ref[0:1, :]
    for i in range(1, GDN_CONV):
        conv = conv + xp_sc[5 + i:5 + i + rows, :] * cw_ref[i:i + 1, :]
    tail = xp_sc[8 + n - 3:8 + n, :]
    xp_sc[5:8, :] = tail
    cs_ref[0] = tail
    xc = _silu(conv)
    valid = _row_iota((rows, BW)) < n
    bdm_bf = _bf(gm_ref[GM_BLOCK])
    q, k, v = xc[:, 0:BW], xc[:, BW:2 * BW], xc[:, 2 * BW:cw]
    q = q * lax.rsqrt(_dot2(q * q, bdm_bf) + EPS) * GDN_DK ** -0.5
    k = k * lax.rsqrt(_dot2(k * k, bdm_bf) + EPS)
    misc = _pad_rows(misc_ref[...], rows)
    beta = _sigmoid(_dot2(misc, eb_ref[...]))
    la = -jnp.exp(alog_ref[...]) * _softplus(_dot2(misc, ea_ref[...]) + dtb_ref[...])
    k = jnp.where(valid, k, 0.0)
    v = jnp.where(valid, v, 0.0)
    la = jnp.where(valid, la, 0.0)
    la_hi, la_lo = _split(la)
    tri = tri_ref[...]
    gall = (jnp.dot(tri, la_hi, preferred_element_type=F32)
            + jnp.dot(tri, la_lo, preferred_element_type=F32))
    c = GDN_CHUNK
    chunks = [(q[r], k[r], v[r], beta[r], gall[r]) for r in (slice(i, i + c) for i in range(0, rows, c))]
    prepared = _gdn_prepare(chunks, gm_ref)
    outs = []
    hbd = h_ref[0]
    for chunk, prep in zip(chunks, prepared):
        o, hbd = _gdn_apply(chunk, prep, hbd, gm_ref)
        outs.append(o)
    h_ref[0] = hbd
    o = outs[0] if len(outs) == 1 else jnp.concatenate(outs, axis=0)
    on = o * lax.rsqrt(_dot2(o * o, amat_ref[...]) + EPS) * ng_ref[...]
    y_ref[...] = on[0:n] * _silu(g_ref[...])


def _gdn_consts(rows):
    lanes = np.arange(BW) // 64
    ea = np.zeros((LANE, BW), np.float32)
    eb = np.zeros((LANE, BW), np.float32)
    ea[MISC_DA + lanes, np.arange(BW)] = 1.0
    eb[MISC_DB + lanes, np.arange(BW)] = 1.0
    i = np.arange(rows)
    tri = ((i[:, None] // GDN_CHUNK == i[None, :] // GDN_CHUNK) & (i[:, None] >= i[None, :])).astype(np.float32)
    bdm = _head_block_mask()
    r = np.arange(BW)
    ri, ci = r[:, None] % GDN_CHUNK, r[None, :] % GDN_CHUNK
    gm = [bdm, bdm * (ri >= ci), bdm * (ri > ci), np.eye(BW, dtype=np.float32)]
    s = 1
    while s < GDN_CHUNK:
        gm.append(bdm * ((ri // (2 * s)) == (ci // (2 * s))) * ((ri // s) % 2 == 1) * ((ci // s) % 2 == 0))
        s *= 2
    return (jnp.asarray(ea, BF16), jnp.asarray(eb, BF16), jnp.asarray(np.stack(gm), F32), jnp.asarray(tri, BF16),
            jnp.asarray(bdm / 64.0, BF16))


def gdn_branch(proj, misc, cs0, conv_w, a_log, dt_bias, norm_g, h0_bd, bsz, seq, rows, n):
    nblk = seq // n
    ea, eb, gm, tri, amat = _gdn_consts(rows)
    per_lane = lambda a, reps: jnp.repeat(a, reps).reshape(1, BW) if reps > 1 else jnp.tile(a, BW // a.shape[0]).reshape(1, BW)
    col = lambda j: pl.BlockSpec((n, BW), lambda b, c, j=j: (b * nblk + c, j))
    const = lambda shape: pl.BlockSpec(shape, lambda b, c: (0,) * len(shape))
    per_b = lambda shape: pl.BlockSpec(shape, lambda b, c: (b,) + (0,) * (len(shape) - 1))
    cw = 3 * BW
    return pl.pallas_call(
        functools.partial(_gdn_kernel, rows=rows),
        grid=(bsz, nblk),
        in_specs=[col(4), col(5), col(6), col(7),
                  pl.BlockSpec((n, LANE), lambda b, c: (b * nblk + c, 0)),
                  per_b((1, GDN_CONV - 1, cw)), const((GDN_CONV, cw)),
                  const((1, BW)), const((1, BW)), const((1, BW)), per_b((1, BW, BW)),
                  const((LANE, BW)), const((LANE, BW)), const(tuple(gm.shape)), const((rows, rows)), const((BW, BW))],
        out_specs=[pl.BlockSpec((n, BW), lambda b, c: (b * nblk + c, 0)),
                   per_b((1, BW, BW)), per_b((1, GDN_CONV - 1, cw))],
        out_shape=[jax.ShapeDtypeStruct((bsz * seq, BW), F32),
                   jax.ShapeDtypeStruct((bsz, BW, BW), F32),
                   jax.ShapeDtypeStruct((bsz, GDN_CONV - 1, cw), F32)],
        scratch_shapes=[pltpu.VMEM((rows + 8, cw), F32)],
        compiler_params=_cp(("parallel", "arbitrary")),
        name="gdn_branch",
    )(proj, proj, proj, proj, misc, cs0, conv_w, per_lane(a_log, 64), per_lane(dt_bias, 64),
      per_lane(norm_g, 1), h0_bd, ea, eb, gm, tri, amat)


_BRANCH_OFFS = (0, ATT_WIDTH, ATT_WIDTH + BW, ATT_WIDTH + 2 * BW, MIX_WIDTH)


def _merge_kernel(x_ref, g_ref, sc_ref, sh_ref, gm_ref, ya_ref, yb_ref, yc_ref, yd_ref,
                  wg_ref, wb_ref, wo_ref, o_ref):
    x = x_ref[...]
    h = _bf(_norm_mod(x, g_ref[...], sc_ref[0], sh_ref[0]))
    merged = None
    for b, y_ref in enumerate((ya_ref, yb_ref, yc_ref, yd_ref)):
        gate = _sigmoid(jnp.dot(h, wg_ref[:, b * D_MODEL:(b + 1) * D_MODEL], preferred_element_type=F32))
        term = gate * jnp.dot(_bf(y_ref[...]), wb_ref[_BRANCH_OFFS[b]:_BRANCH_OFFS[b + 1], :],
                              preferred_element_type=F32)
        merged = term if merged is None else merged + term
    y = jnp.dot(_bf(merged), wo_ref[...], preferred_element_type=F32)
    o_ref[...] = x + gm_ref[0] * y


def merge_out(x, g, sc, sh, gm, ya, yb, yc, yd, wg_bf, wb_bf, wo_bf, tm, tiles_per_seq):
    m = x.shape[0]
    ms = _mod_spec(tm, sc.shape[1], tiles_per_seq)
    row = lambda w: pl.BlockSpec((tm, w), lambda i: (i, 0))
    const = lambda shape: pl.BlockSpec(shape, lambda i: (0,) * len(shape))
    return pl.pallas_call(
        _merge_kernel,
        grid=(m // tm,),
        in_specs=[row(D_MODEL), const((1, D_MODEL)), ms, ms, ms, row(ATT_WIDTH), row(BW), row(BW), row(BW),
                  const((D_MODEL, N_BRANCH * D_MODEL)), const((MIX_WIDTH, D_MODEL)), const((D_MODEL, D_MODEL))],
        out_specs=row(D_MODEL),
        out_shape=jax.ShapeDtypeStruct((m, D_MODEL), F32),
        compiler_params=_cp(("parallel",)),
        name="merge_out",
    )(x, g.reshape(1, D_MODEL), sc, sh, gm, ya, yb, yc, yd, wg_bf, wb_bf, wo_bf)


def _top2(masked, lane):
    m1 = jnp.max(masked, axis=-1, keepdims=True)
    i1 = jnp.min(jnp.where(masked == m1, lane, LANE), axis=-1, keepdims=True)
    rest = jnp.where(lane == i1, -jnp.inf, masked)
    m2 = jnp.max(rest, axis=-1, keepdims=True)
    i2 = jnp.min(jnp.where(rest == m2, lane, LANE), axis=-1, keepdims=True)
    return m1, i1, m2, i2


def _route(scores, biased):
    lane = _lane_iota(scores.shape)
    grp = lane // EXPERTS_PER_GROUP
    best_val, best_grp = None, None
    for g in range(N_GROUPS):
        m1, _, m2, _ = _top2(jnp.where(grp == g, biased, -jnp.inf), lane)
        gs = m1 + m2
        if g == 0:
            best_val, best_grp = gs, jnp.zeros(gs.shape, jnp.int32)
        else:
            better = gs > best_val
            best_val = jnp.where(better, gs, best_val)
            best_grp = jnp.where(better, g, best_grp)
    _, e1, _, e2 = _top2(jnp.where(grp == best_grp, biased, -jnp.inf), lane)
    s1 = jnp.sum(jnp.where(lane == e1, scores, 0.0), axis=-1, keepdims=True)
    s2 = jnp.sum(jnp.where(lane == e2, scores, 0.0), axis=-1, keepdims=True)
    tot = s1 + s2
    return jnp.where(lane == e1, s1 / tot, 0.0) + jnp.where(lane == e2, s2 / tot, 0.0)


def _moe_kernel(x_ref, g_ref, sc_ref, sh_ref, gm_ref, wr_ref, rb_ref, w1_ref, w3_ref, w2_ref, fg_ref,
                o_ref, h_sc, comb_sc, acc_sc, *, final):
    e = pl.program_id(1)

    @pl.when(e == 0)
    def _():
        h = _norm_mod(x_ref[...], g_ref[...], sc_ref[0], sh_ref[0])
        h_sc[...] = _bf(h)
        scores = _sigmoid(_dot3(h, wr_ref[...]))
        comb_sc[...] = _route(scores, scores + rb_ref[...])
        acc_sc[...] = jnp.zeros(acc_sc.shape, F32)

    h = h_sc[...]
    comb = comb_sc[...]
    ce = jnp.sum(jnp.where(_lane_iota(comb.shape) == e, comb, 0.0), axis=-1, keepdims=True)
    hid = _silu(jnp.dot(h, w1_ref[0], preferred_element_type=F32)) * jnp.dot(h, w3_ref[0], preferred_element_type=F32)
    acc_sc[...] += jnp.dot(_bf(hid * ce), w2_ref[0], preferred_element_type=F32)

    @pl.when(e == pl.num_programs(1) - 1)
    def _():
        out = x_ref[...] + gm_ref[0] * acc_sc[...]
        if final:
            out = out * lax.rsqrt(jnp.mean(out * out, axis=-1, keepdims=True) + EPS) * fg_ref[...]
        o_ref[...] = out


def moe_out(x, g, sc, sh, gm, wr_pad, rb_pad, w1_bf, w3_bf, w2_bf, final_g, final, tm, tiles_per_seq):
    m = x.shape[0]
    ms = _mod_spec(tm, sc.shape[1], tiles_per_seq)
    const = lambda shape: pl.BlockSpec(shape, lambda i, e: (0,) * len(shape))
    return pl.pallas_call(
        functools.partial(_moe_kernel, final=final),
        grid=(m // tm, N_EXPERTS),
        in_specs=[pl.BlockSpec((tm, D_MODEL), lambda i, e: (i, 0)), const((1, D_MODEL)), ms, ms, ms,
                  const((D_MODEL, LANE)), const((1, LANE)),
                  pl.BlockSpec((1, D_MODEL, EXPERT_FF), lambda i, e: (e, 0, 0)),
                  pl.BlockSpec((1, D_MODEL, EXPERT_FF), lambda i, e: (e, 0, 0)),
                  pl.BlockSpec((1, EXPERT_FF, D_MODEL), lambda i, e: (e, 0, 0)),
                  const((1, D_MODEL))],
        out_specs=pl.BlockSpec((tm, D_MODEL), lambda i, e: (i, 0)),
        out_shape=jax.ShapeDtypeStruct((m, D_MODEL), F32),
        scratch_shapes=[pltpu.VMEM((tm, D_MODEL), BF16), pltpu.VMEM((tm, LANE), F32),
                        pltpu.VMEM((tm, D_MODEL), F32)],
        compiler_params=_cp(("parallel", "arbitrary")),
        name="moe_out",
    )(x, g.reshape(1, D_MODEL), sc, sh, gm, wr_pad, rb_pad, w1_bf, w3_bf, w2_bf, final_g.reshape(1, D_MODEL))


_REF_SPLITS = (ATT_WIDTH, KVW, KVW, IDX_HEADS * IDX_DIM, IDX_DIM, IDX_HEADS, BW,
               BW, BW, BW, BW, BW, BW, BW, GDN_HEADS, GDN_HEADS, BW)


def pack_w_in(w_in):
    offs = np.concatenate([[0], np.cumsum(_REF_SPLITS)])
    seg = [w_in[:, int(offs[i]):int(offs[i + 1])] for i in range(len(_REF_SPLITS))]
    (aq, ak, av, aiq, aik, aiw, bu, cq, ck, cv, cg, dq, dk, dv, da, db, dg) = seg
    zeros = lambda n: jnp.zeros((D_MODEL, n), w_in.dtype)
    misc = jnp.concatenate([aik, aiw, da, db, zeros(LANE - IDX_DIM - IDX_HEADS - 2 * GDN_HEADS)], axis=1)
    packed = jnp.concatenate([cq, ck, cv, cg, dq, dk, dv, dg, bu, aq, ak, av, aiq, zeros(2 * LANE)], axis=1)
    wt = jnp.concatenate([aiw.T, jnp.zeros((8 - IDX_HEADS, D_MODEL), w_in.dtype)], axis=0)
    return _bf(packed), misc, wt


def _time_major(a, bsz, seq):
    return a.reshape(bsz, seq, a.shape[-1]).transpose(1, 0, 2).reshape(seq * bsz, a.shape[-1])


def _batch_major(a, bsz, seq):
    return a.reshape(seq, bsz, a.shape[-1]).transpose(1, 0, 2).reshape(bsz * seq, a.shape[-1])


def _trunk_layer(x, mods, geom, attend, st, lw, final_g, final):
    bsz, seq, tm, tps, s5_tc, ret_rows, ret_n, gdn_rows, gdn_n = geom
    sh1, sc1, g1, sh2, sc2, g2 = mods
    proj, misc, iwt = in_proj(x, lw['norm1'], sc1, sh1, lw['w_in'], lw['w_misc'], lw['w_iwt'], tm, tps)
    qr, kr, v, iqr, ikr, ik4, vt = attn_prep(proj, misc, lw['tab_a'], lw['tab_i'], lw['tab_k'], tm, tps)
    ya = attend(qr, kr, v, vt, iqr, ikr, ik4, iwt, misc)
    u_tm = _time_major(proj[:, 8 * BW:9 * BW], bsz, seq)
    y_tm, s5_h = s5_branch(u_tm, st['s5'], lw['s5_abar'], lw['s5_win'], lw['s5_wout'], lw['s5_d'],
                           lw['s5_w_glu'], lw['s5_b_glu'], bsz, seq, s5_tc)
    yb = _batch_major(y_tm, bsz, seq)
    yc, ret_s = ret_branch(proj, lw['tab_r'], st['ret'], bsz, seq, ret_rows, ret_n)
    yd, gdn_s, conv_s = gdn_branch(proj, misc, st['conv'], lw['gdn_conv_w'], lw['gdn_a_log'], lw['gdn_dt_bias'],
                                   lw['gdn_norm_g'], st['gdn'], bsz, seq, gdn_rows, gdn_n)
    x = merge_out(x, lw['norm1'], sc1, sh1, g1, ya, yb, yc, yd, lw['w_gate'], lw['w_br'], lw['w_out'], tm, tps)
    x = moe_out(x, lw['norm2'], sc2, sh2, g2, lw['w_router'], lw['router_bias'], lw['w_e1'], lw['w_e3'],
                lw['w_e2'], final_g, final, tm, tps)
    new_st = {'k': kr, 'v': v, 'ik': ikr, 's5': s5_h, 'ret': ret_s, 'gdn': gdn_s, 'conv': conv_s}
    return x, new_st


def kernel(x_prompt, x_sample, c_prompt, c_sample, cache_k, cache_v, cache_idx_k, page_table,
           state_s5_re, state_s5_im, state_ret, state_gdn, state_gdn_conv,
           norm1_g, norm2_g, final_g, w_ada, b_ada, w_in,
           s5_a_re, s5_a_im, s5_b_re, s5_b_im, s5_c_re, s5_c_im, s5_d, s5_log_dt, s5_w_glu, s5_b_glu,
           gdn_conv_w, gdn_a_log, gdn_dt_bias, gdn_norm_g,
           w_br, w_gate, w_out, w_router, router_bias, w_e1, w_e3, w_e2):
    bsz, seq, _ = x_prompt.shape
    dbs, dseq, _ = x_sample.shape
    depth = w_in.shape[0]
    n_pool = cache_k.shape[1]
    past = page_table.shape[1] * PAGE_SIZE
    mp, ms = bsz * seq, dbs * dseq
    tm_p = 512
    pos_p = jnp.arange(seq, dtype=jnp.int32)
    pos_s = past + jnp.arange(dseq, dtype=jnp.int32)
    pos_s_tok = jnp.tile(pos_s, dbs)

    def tables(pos):
        return {'tab_a': _rope_tables(pos, ROT_DIMS, ROPE_THETA, HEAD_DIM),
                'tab_i': _rope_tables(pos, IDX_ROT, ROPE_THETA, IDX_DIM),
                'tab_k': _rope_tables(pos, IDX_ROT, ROPE_THETA, IDX_DIM, active=IDX_DIM)}

    tabs_p = dict(tables(pos_p), tab_r=_rope_tables(pos_p, RET_DK, RET_THETA, RET_DK))
    tabs_s = dict(tables(pos_s_tok), tab_r=_rope_tables(pos_s, RET_DK, RET_THETA, RET_DK))
    ck = cache_k.reshape(depth * n_pool, PAGE_SIZE, KVW)
    cv = cache_v.reshape(depth * n_pool, PAGE_SIZE, KVW)
    cik = cache_idx_k.reshape(depth * n_pool, PAGE_SIZE, IDX_DIM)
    wr_pad = jnp.pad(w_router, ((0, 0), (0, LANE - N_EXPERTS)))
    rb_pad = jnp.pad(router_bias, (0, LANE - N_EXPERTS)).reshape(1, LANE)
    c_all = jnp.concatenate([c_prompt, c_sample], axis=0)

    geom_p = (bsz, seq, tm_p, seq // tm_p, 64, 256, 256, 256, 256)
    geom_s = (dbs, dseq, ms, 1, dseq, LANE, dseq, GDN_CHUNK, dseq)
    zero_st = {'s5': jnp.zeros((bsz, 2 * S5_NS), F32), 'ret': jnp.zeros((bsz, BW, BW), F32),
               'gdn': jnp.zeros((bsz, BW, BW), F32), 'conv': jnp.zeros((bsz, GDN_CONV - 1, 3 * BW), F32)}

    xp = x_prompt.reshape(mp, D_MODEL)
    xs = x_sample.reshape(ms, D_MODEL)
    outs_p, outs_s = [], []
    for l in range(depth):
        w_in_p, w_misc, w_iwt = pack_w_in(w_in[l])
        abar8, win = s5_params(s5_log_dt[l], s5_a_re[l], s5_a_im[l], s5_b_re[l], s5_b_im[l])
        lw = {'norm1': norm1_g[l], 'norm2': norm2_g[l], 'w_in': w_in_p, 'w_misc': w_misc, 'w_iwt': w_iwt,
              's5_abar': abar8, 's5_win': win,
              's5_wout': _bf(jnp.concatenate([_block_diag_out(s5_c_re[l]), _block_diag_out(s5_c_im[l])], axis=0)),
              's5_d': s5_d[l], 's5_w_glu': _bf(s5_w_glu[l]), 's5_b_glu': s5_b_glu[l],
              'gdn_conv_w': gdn_conv_w[l], 'gdn_a_log': gdn_a_log[l], 'gdn_dt_bias': gdn_dt_bias[l],
              'gdn_norm_g': gdn_norm_g[l],
              'w_br': _bf(w_br[l]), 'w_gate': _bf(w_gate[l]), 'w_out': _bf(w_out[l]),
              'w_router': wr_pad, 'router_bias': rb_pad,
              'w_e1': _bf(w_e1[l]), 'w_e3': _bf(w_e3[l]), 'w_e2': _bf(w_e2[l])}
        mod = ada_mod(c_all, _bf(w_ada[l]), b_ada[l])
        mods = [mod[:, i * D_MODEL:(i + 1) * D_MODEL] for i in range(6)]
        mods_p = [m[:bsz].reshape(bsz, 1, D_MODEL) for m in mods]
        mods_s = [jnp.repeat(m[bsz:], dseq, axis=0).reshape(1, ms, D_MODEL) for m in mods]
        final = l == depth - 1

        def attend_p(qr, kr, v, vt, iqr, ikr, ik4, iwt, misc):
            return attn_prompt(qr, kr, vt, iqr, ik4, iwt, bsz, seq)

        def attend_s(qr, kr, v, vt, iqr, ikr, ik4, iwt, misc, l=l):
            return attn_sample(qr, kr, v, iqr, ikr, misc, ck, cv, cik, page_table, l, dseq)

        st_s = {'s5': jnp.concatenate([state_s5_re[l].reshape(dbs, S5_NS), state_s5_im[l].reshape(dbs, S5_NS)], axis=1),
                'ret': to_block_diag(state_ret[l]), 'gdn': to_block_diag(state_gdn[l]), 'conv': state_gdn_conv[l]}
        xp, ns_p = _trunk_layer(xp, mods_p, geom_p, attend_p, zero_st, dict(lw, **tabs_p), final_g, final)
        xs, ns_s = _trunk_layer(xs, mods_s, geom_s, attend_s, st_s, dict(lw, **tabs_s), final_g, final)
        outs_p.append(ns_p)
        outs_s.append(ns_s)

    def stack(outs, name, shape):
        return jnp.stack([o[name] for o in outs], axis=0).reshape((depth,) + shape)

    def states(outs, b, t):
        re = jnp.stack([o['s5'][:, :S5_NS] for o in outs], axis=0).reshape(depth, b, S5_GROUPS, S5_STATE)
        im = jnp.stack([o['s5'][:, S5_NS:] for o in outs], axis=0).reshape(depth, b, S5_GROUPS, S5_STATE)
        ret = jnp.stack([from_block_diag(o['ret']) for o in outs], axis=0)
        gdn = jnp.stack([from_block_diag(o['gdn']) for o in outs], axis=0)
        return (stack(outs, 'k', (b, t, KV_HEADS, HEAD_DIM)), stack(outs, 'v', (b, t, KV_HEADS, HEAD_DIM)),
                stack(outs, 'ik', (b, t, IDX_DIM)), re, im, ret, gdn,
                stack(outs, 'conv', (b, GDN_CONV - 1, 3 * BW)))

    kp, vp, ikp, rep, imp, retp, gdnp, convp = states(outs_p, bsz, seq)
    ks_, vs_, iks, res, ims, rets, gdns, convs = states(outs_s, dbs, dseq)
    return (xp.reshape(bsz, seq, D_MODEL), xs.reshape(dbs, dseq, D_MODEL), kp, vp, ikp, ks_, vs_, iks,
            rep, imp, res, ims, retp, rets, gdnp, gdns, convp, convs)
```

```python
import functools
import math

import numpy as np
import jax
import jax.numpy as jnp
from jax import lax
from jax.experimental import pallas as pl
from jax.experimental.pallas import tpu as pltpu

F32 = jnp.float32
BF16 = jnp.bfloat16

D_MODEL = 1024
DEPTH = 2
PAST_LEN = 8192
PAGE_SIZE = 128
ATT_HEADS = 8
KV_HEADS = 2
HEAD_DIM = 64
ROT_DIMS = HEAD_DIM // 4
ROPE_THETA = 500000.0
IDX_HEADS = 4
IDX_DIM = 32
IDX_ROT = IDX_DIM // 4
TOPK_MAX = 256
S5_GROUPS = 24
S5_GROUP_CH = 16
S5_STATE = 64
S5_WIDTH = S5_GROUPS * S5_GROUP_CH
S5_NS = S5_GROUPS * S5_STATE
RET_HEADS = 6
RET_DK = 64
RET_THETA = 10000.0
GDN_HEADS = 6
GDN_DK = 64
GDN_CONV = 4
GDN_CHUNK = 64
ATT_WIDTH = ATT_HEADS * HEAD_DIM
BW = 384
KVW = KV_HEADS * HEAD_DIM
N_BRANCH = 4
MIX_WIDTH = ATT_WIDTH + 3 * BW
N_EXPERTS = 16
N_GROUPS = 4
EXPERTS_PER_GROUP = 4
EXPERT_FF = 256
EPS = 1e-6

PROJ_COLS = 12 * BW
ATT_COL0 = 9 * BW
MISC_DA = 36
MISC_DB = 42
LANE = 128
VMEM_LIMIT = 56 * 1024 * 1024
NEG = -1e30


def _cp(sem):
    return pltpu.CompilerParams(dimension_semantics=sem, vmem_limit_bytes=VMEM_LIMIT)


def _bf(x):
    return x.astype(BF16)


def _dot(a, b):
    return jnp.dot(_bf(a), _bf(b), preferred_element_type=F32)


def _dot_nt(a, b):
    return lax.dot_general(_bf(a), _bf(b), (((1,), (1,)), ((), ())), preferred_element_type=F32)


def _dot_tn(a, b):
    return lax.dot_general(_bf(a), _bf(b), (((0,), (0,)), ((), ())), preferred_element_type=F32)


def _split(x):
    hi = x.astype(BF16)
    lo = (x - hi.astype(F32)).astype(BF16)
    return hi, lo


def _dot3(a, b):
    ah, al = _split(a)
    bh, bl = _split(b)
    d = functools.partial(jnp.dot, preferred_element_type=F32)
    return d(ah, bh) + (d(ah, bl) + d(al, bh))


def _dot2(a, b01):
    ah, al = _split(a)
    d = functools.partial(jnp.dot, preferred_element_type=F32)
    return d(ah, b01) + d(al, b01)


def _sigmoid(x):
    return 1.0 / (1.0 + jnp.exp(-x))


def _silu(x):
    return x * _sigmoid(x)


def _lane_iota(shape):
    return lax.broadcasted_iota(jnp.int32, shape, len(shape) - 1)


def _row_iota(shape):
    return lax.broadcasted_iota(jnp.int32, shape, len(shape) - 2)


def _ada_kernel(c_ref, w_ref, b_ref, o_ref):
    o_ref[...] = _dot(_silu(c_ref[...]), w_ref[...]) + b_ref[...]


def ada_mod(c, w_bf, b):
    n = c.shape[0]
    cols = w_bf.shape[1]
    tn = 1024
    return pl.pallas_call(
        _ada_kernel,
        grid=(cols // tn,),
        in_specs=[pl.BlockSpec((n, D_MODEL), lambda j: (0, 0)),
                  pl.BlockSpec((D_MODEL, tn), lambda j: (0, j)),
                  pl.BlockSpec((1, tn), lambda j: (0, j))],
        out_specs=pl.BlockSpec((n, tn), lambda j: (0, j)),
        out_shape=jax.ShapeDtypeStruct((n, cols), F32),
        compiler_params=_cp(("parallel",)),
        name="ada_mod",
    )(c, w_bf, b.reshape(1, cols))


def _norm_mod(x, g, sc, sh):
    y = x * lax.rsqrt(jnp.mean(x * x, axis=-1, keepdims=True) + EPS) * g
    return y * (1.0 + sc) + sh


def _in_kernel(x_ref, g_ref, sc_ref, sh_ref, w_ref, wm_ref, wt_ref, o_ref, om_ref, ot_ref, h_sc):
    @pl.when(pl.program_id(1) == 0)
    def _():
        h = _norm_mod(x_ref[...], g_ref[...], sc_ref[0], sh_ref[0])
        h_sc[...] = h.astype(BF16)
        hh, hl = _split(h)
        wmh, wml = _split(wm_ref[...])
        d = functools.partial(jnp.dot, preferred_element_type=F32)
        om_ref[...] = d(hh, wmh) + (d(hh, wml) + d(hl, wmh))
        wth, wtl = _split(wt_ref[...])
        nt = functools.partial(lax.dot_general, dimension_numbers=(((1,), (1,)), ((), ())),
                               preferred_element_type=F32)
        ot_ref[...] = nt(wth, hh) + (nt(wth, hl) + nt(wtl, hh))

    o_ref[...] = jnp.dot(h_sc[...], w_ref[...], preferred_element_type=F32)


def _mod_spec(tm, mod_rows, tiles_per_seq):
    if mod_rows == 1:
        return pl.BlockSpec((1, 1, D_MODEL), lambda i, *_: (i // tiles_per_seq, 0, 0))
    return pl.BlockSpec((1, tm, D_MODEL), lambda i, *_: (i, 0, 0))


def in_proj(x, g, sc, sh, w_bf, w_misc, w_iwt, tm, tiles_per_seq):
    m = x.shape[0]
    tn = PROJ_COLS
    mod_rows = sc.shape[1]
    ms = _mod_spec(tm, mod_rows, tiles_per_seq)
    return pl.pallas_call(
        _in_kernel,
        grid=(m // tm, PROJ_COLS // tn),
        in_specs=[pl.BlockSpec((tm, D_MODEL), lambda i, j: (i, 0)),
                  pl.BlockSpec((1, D_MODEL), lambda i, j: (0, 0)),
                  ms, ms,
                  pl.BlockSpec((D_MODEL, tn), lambda i, j: (0, j)),
                  pl.BlockSpec((D_MODEL, LANE), lambda i, j: (0, 0)),
                  pl.BlockSpec((8, D_MODEL), lambda i, j: (0, 0))],
        out_specs=[pl.BlockSpec((tm, tn), lambda i, j: (i, j)),
                   pl.BlockSpec((tm, LANE), lambda i, j: (i, 0)),
                   pl.BlockSpec((8, tm), lambda i, j: (0, i))],
        out_shape=[jax.ShapeDtypeStruct((m, PROJ_COLS), F32),
                   jax.ShapeDtypeStruct((m, LANE), F32),
                   jax.ShapeDtypeStruct((8, m), F32)],
        scratch_shapes=[pltpu.VMEM((tm, D_MODEL), BF16)],
        compiler_params=_cp(("parallel", "arbitrary")),
        name="in_proj",
    )(x, g.reshape(1, D_MODEL), sc, sh, w_bf, w_misc, w_iwt)


def _rope_tables(pos, rot_dims, theta, period, width=LANE, active=None):
    half = rot_dims // 2
    inv_freq = jnp.power(jnp.float32(theta), -jnp.arange(half, dtype=F32) / half)
    ang = pos.astype(F32)[:, None] * inv_freq
    cos, sin = jnp.cos(ang), jnp.sin(ang)
    t = pos.shape[0]
    c = jnp.concatenate([cos, cos, jnp.ones((t, period - rot_dims), F32)], axis=1)
    s_up = jnp.concatenate([-sin, jnp.zeros((t, period - half), F32)], axis=1)
    s_dn = jnp.concatenate([jnp.zeros((t, half), F32), sin, jnp.zeros((t, period - rot_dims), F32)], axis=1)
    reps = width // period
    tab = jnp.stack([jnp.tile(a, (1, reps)) for a in (c, s_up, s_dn)], axis=0)
    if active is not None:
        ident = jnp.stack([jnp.ones((t, width), F32), jnp.zeros((t, width), F32),
                           jnp.zeros((t, width), F32)], axis=0)
        tab = jnp.where(jnp.arange(width) < active, tab, ident)
    return tab


def _rope(x, tab_ref, half):
    w = x.shape[1]
    reps = w // LANE

    def wide(k):
        t = tab_ref[k]
        return t if reps == 1 else jnp.concatenate([t] * reps, axis=1)

    return (x * wide(0) + pltpu.roll(x, w - half, axis=1) * wide(1)
            + pltpu.roll(x, half, axis=1) * wide(2))


def _prep_kernel(p_ref, m_ref, ta_ref, ti_ref, tk_ref, q_ref, k_ref, v_ref, iq_ref, ik_ref, ik4_ref, vt_ref):
    q_ref[...] = _rope(p_ref[:, 0:ATT_WIDTH], ta_ref, ROT_DIMS // 2)
    k_ref[...] = _rope(p_ref[:, 512:640], ta_ref, ROT_DIMS // 2)
    v_ref[...] = p_ref[:, 640:768]
    vt_ref[0] = p_ref[:, 640:768].T
    iq_ref[...] = _rope(p_ref[:, 768:896], ti_ref, IDX_ROT // 2)
    ikr = _rope(m_ref[...], tk_ref, IDX_ROT // 2)
    ik_ref[...] = ikr[:, 0:IDX_DIM]
    m = jnp.where(_lane_iota(ikr.shape) < IDX_DIM, ikr, 0.0)
    ik4_ref[...] = (m + pltpu.roll(m, 32, axis=1)) + (pltpu.roll(m, 64, axis=1) + pltpu.roll(m, 96, axis=1))


def attn_prep(proj, misc, tab_a, tab_i, tab_k, tm, tiles_per_seq):
    m = proj.shape[0]
    tspec = pl.BlockSpec((3, tm, LANE), lambda i: (0, i % tiles_per_seq, 0))
    widths = (ATT_WIDTH, KVW, KVW, LANE, IDX_DIM, LANE)
    seq = tm * tiles_per_seq
    return pl.pallas_call(
        _prep_kernel,
        grid=(m // tm,),
        in_specs=[pl.BlockSpec((tm, 3 * BW), lambda i: (i, ATT_COL0 // (3 * BW))),
                  pl.BlockSpec((tm, LANE), lambda i: (i, 0)), tspec, tspec, tspec],
        out_specs=[pl.BlockSpec((tm, w), lambda i: (i, 0)) for w in widths]
        + [pl.BlockSpec((1, KVW, tm), lambda i: (i // tiles_per_seq, 0, i % tiles_per_seq))],
        out_shape=[jax.ShapeDtypeStruct((m, w), F32) for w in widths]
        + [jax.ShapeDtypeStruct((m // seq, KVW, seq), F32)],
        compiler_params=_cp(("parallel",)),
        name="attn_prep",
    )(proj, misc, tab_a, tab_i, tab_k)


BISECT_MAX_ITERS = 48
BISECT_UNROLL = 4
FAR = 2.0 ** 126


def _count(ones, axis):
    return jnp.sum(ones, axis=axis, keepdims=True)


COUNT_ROWS = 32
COUNT_ACCS = 4


def _count_where(s_ref, pred, axis):
    n = s_ref.shape[0]
    if axis != 0 or n % (COUNT_ROWS * COUNT_ACCS) != 0:
        return _count(pred(s_ref[...]), axis)
    accs = [None] * COUNT_ACCS
    for j, i in enumerate(range(0, n, COUNT_ROWS)):
        part = pred(s_ref[i:i + COUNT_ROWS, :])
        a = j % COUNT_ACCS
        accs[a] = part if accs[a] is None else accs[a] + part
    return jnp.sum((accs[0] + accs[1]) + (accs[2] + accs[3]), axis=0, keepdims=True)


def _bisect_topk(s_ref, axis, topk):
    kshape = tuple(1 if a == axis else n for a, n in enumerate(s_ref.shape))
    s = s_ref[...]
    lo0 = jnp.min(jnp.where(s > -FAR, s, FAR), axis=axis, keepdims=True)
    mx = jnp.max(s, axis=axis, keepdims=True)
    hi0 = mx + (jnp.abs(mx) * 2.0 ** -20 + 1e-30)
    cnt_lo0 = _count_where(s_ref, lambda t: jnp.where(t >= lo0, 1, 0), axis)
    n_zero = _count_where(s_ref, lambda t: jnp.where(t == 0.0, 1, 0), axis)

    def pending(lo, hi, cnt_lo, cnt_hi):
        only_zeros = jnp.where(lo <= 0.0, jnp.where(hi > 0.0, jnp.where(cnt_lo - cnt_hi == n_zero, 1, 0), 0), 0)
        return jnp.max(jnp.where(cnt_lo <= topk, 0, 1 - only_zeros))

    def cond(c):
        return jnp.logical_and(c[0] < BISECT_MAX_ITERS, c[1] > 0)

    def body(c):
        it, _, lo, hi, cnt_lo, cnt_hi = c
        for _ in range(BISECT_UNROLL):
            mid = 0.5 * lo + 0.5 * hi
            cm = _count_where(s_ref, lambda t: jnp.where(t >= mid, 1, 0), axis)
            ge = cm >= topk
            lo, hi = jnp.where(ge, mid, lo), jnp.where(ge, hi, mid)
            cnt_lo, cnt_hi = jnp.where(ge, cm, cnt_lo), jnp.where(ge, cnt_hi, cm)
        return it + BISECT_UNROLL, pending(lo, hi, cnt_lo, cnt_hi), lo, hi, cnt_lo, cnt_hi

    cnt_hi0 = jnp.zeros(kshape, jnp.int32)
    init = (jnp.int32(0), pending(lo0, hi0, cnt_lo0, cnt_hi0), lo0, hi0, cnt_lo0, cnt_hi0)
    _, _, lo, hi, cnt_lo, cnt_hi = lax.while_loop(cond, body, init)
    return lo, hi, cnt_lo, cnt_hi


def _topk_select(s_ref, idx, axis, n_idx_bits, p_sc, topk):
    kshape = tuple(1 if a == axis else n for a, n in enumerate(s_ref.shape))
    lo, hi, cnt_lo, cnt_hi = _bisect_topk(s_ref, axis, topk)
    need = topk - cnt_hi
    p_sc[...] = jnp.full(kshape, (1 << n_idx_bits) - 1, jnp.int32)

    @pl.when(jnp.max(cnt_lo - cnt_hi - need) > 0)
    def _():
        tied = jnp.where(s_ref[...] >= lo, jnp.where(s_ref[...] >= hi, 0, 1), 0)

        def ibody(i, p):
            cand = p + jnp.left_shift(jnp.int32(1), n_idx_bits - 1 - i)
            taken = _count(jnp.where(idx < cand, tied, 0), axis)
            return jnp.where(taken < need, cand, p)

        p_sc[...] = lax.fori_loop(0, n_idx_bits, ibody, jnp.zeros(kshape, jnp.int32))

    s = s_ref[...]
    return jnp.where(s >= hi, 1, jnp.where(s >= lo, jnp.where(idx <= p_sc[...], 1, 0), 0))


def _topk_bias_keys_major(s_ref, tri_ref, topk):
    tk, tq = s_ref.shape
    lo, hi, _, cnt_hi = _bisect_topk(s_ref, 0, topk)
    need = (topk - cnt_hi).astype(F32)
    tri = tri_ref[...]
    offset = jnp.zeros((1, tq), F32)
    parts = []
    for c in range(tk // tri.shape[0]):
        s = s_ref[c * tri.shape[0]:(c + 1) * tri.shape[0], :]
        cand = jnp.where(s >= lo, jnp.where(s >= hi, 0.0, 1.0), 0.0)
        rank = jnp.dot(tri, _bf(cand), preferred_element_type=F32) + offset
        offset = rank[tri.shape[0] - 1:tri.shape[0], :]
        parts.append(jnp.where(s >= hi, 0.0, jnp.where(cand * rank > 0.5, jnp.where(rank <= need, 0.0, NEG), NEG)))
    return jnp.concatenate(parts, axis=0)


def _group_queries(q, g):
    tiles = []
    keep = (_lane_iota((q.shape[0], LANE)) // HEAD_DIM) == g
    for hl in range(ATT_HEADS // KV_HEADS):
        h = g * (ATT_HEADS // KV_HEADS) + hl
        t = q[:, (h // 2) * LANE:(h // 2 + 1) * LANE]
        if h % 2 != g:
            t = pltpu.roll(t, HEAD_DIM, axis=1)
        tiles.append(jnp.where(keep, t, 0.0))
    return jnp.concatenate(tiles, axis=0)


def _ungroup_outputs(o_groups, tq):
    low = _lane_iota((tq, LANE)) < HEAD_DIM
    tiles = []
    for j in range(ATT_HEADS // 2):
        halves = []
        for h in (2 * j, 2 * j + 1):
            g, hl = divmod(h, ATT_HEADS // KV_HEADS)
            t = o_groups[g][hl * tq:(hl + 1) * tq]
            if h % 2 != g:
                t = pltpu.roll(t, HEAD_DIM, axis=1)
            halves.append(t)
        tiles.append(jnp.where(low, halves[0], halves[1]))
    return jnp.concatenate(tiles, axis=1)


def _masked_attention_keys_major(q, kb, vt, bias_t):
    tq = q.shape[0]
    heads = ATT_HEADS // KV_HEADS
    tk = kb.shape[0]
    bias4 = jnp.concatenate([bias_t] * heads, axis=1)
    vrow_group = _row_iota((LANE, tk)) // HEAD_DIM
    q = q * HEAD_DIM ** -0.5
    normed = []
    for g in range(KV_HEADS):
        st = _dot_nt(kb, _group_queries(q, g)) + bias4
        p = jnp.exp(_bf(st - jnp.max(st, axis=0, keepdims=True)))
        ot = jnp.dot(_bf(jnp.where(vrow_group == g, vt, 1.0)), p, preferred_element_type=F32)
        other = (1 - g) * HEAD_DIM
        normed.append(ot[g * HEAD_DIM:(g + 1) * HEAD_DIM, :] / ot[other:other + 1, :])
    tiles = []
    for j in range(ATT_HEADS // 2):
        g, hl = divmod(2 * j, heads)
        pair = jnp.concatenate([normed[g][:, hl * tq:(hl + 1) * tq], normed[g][:, (hl + 1) * tq:(hl + 2) * tq]], axis=0)
        tiles.append(pair.T)
    return jnp.concatenate(tiles, axis=1)


def _attn_prompt_kernel(q_ref, iq_ref, iwt_ref, k_ref, vt_ref, ik4_ref, tri_ref, o_ref, s_sc, *, qblk0, tq, topk):
    tk = k_ref.shape[1]
    q0 = (qblk0 + pl.program_id(1)) * tq
    iq = iq_ref[0]
    head_of_lane = _lane_iota((tq, LANE)) // IDX_DIM
    iq4 = jnp.concatenate([jnp.where(head_of_lane == h, iq, 0.0) for h in range(IDX_HEADS)], axis=0)
    lg = _dot_nt(ik4_ref[0], iq4)
    iwt = iwt_ref[...] * (IDX_HEADS ** -0.5 * IDX_DIM ** -0.5)
    score = None
    for h in range(IDX_HEADS):
        part = jnp.maximum(lg[:, h * tq:(h + 1) * tq], 0.0) * iwt[h:h + 1, :]
        score = part if score is None else score + part
    adm = _row_iota((tk, tq)) <= q0 + _lane_iota((tk, tq))
    s_sc[...] = jnp.where(adm, score, -FAR)
    bias_t = _topk_bias_keys_major(s_sc, tri_ref, topk)
    o_ref[0] = _masked_attention_keys_major(q_ref[0], _bf(k_ref[0]), vt_ref[0], bias_t)


PREFIX_ROWS = 256


def attn_prompt(qr, kr, vt, iqr, ik4, iwt, bsz, seq, n_classes=4, tq=128):
    topk = min(TOPK_MAX, seq // 4)
    nq = seq // tq
    per = max(1, nq // n_classes)
    q3 = qr.reshape(bsz, seq, ATT_WIDTH)
    iq3 = iqr.reshape(bsz, seq, LANE)
    k3, ik3 = (a.reshape(bsz, seq, LANE) for a in (kr, ik4))
    outs = []
    for c in range(nq // per):
        tk = (c + 1) * per * tq
        qb0 = c * per
        r = np.arange(math.gcd(PREFIX_ROWS, tk))
        tri = jnp.asarray(r[:, None] >= r[None, :], BF16)
        out = pl.pallas_call(
            functools.partial(_attn_prompt_kernel, qblk0=qb0, tq=tq, topk=topk),
            grid=(bsz, per),
            in_specs=[pl.BlockSpec((1, tq, ATT_WIDTH), lambda b, j, qb0=qb0: (b, qb0 + j, 0)),
                      pl.BlockSpec((1, tq, LANE), lambda b, j, qb0=qb0: (b, qb0 + j, 0)),
                      pl.BlockSpec((8, tq), lambda b, j, qb0=qb0: (0, b * nq + qb0 + j)),
                      pl.BlockSpec((1, tk, LANE), lambda b, j: (b, 0, 0)),
                      pl.BlockSpec((1, LANE, tk), lambda b, j: (b, 0, 0)),
                      pl.BlockSpec((1, tk, LANE), lambda b, j: (b, 0, 0)),
                      pl.BlockSpec(tri.shape, lambda b, j: (0, 0))],
            out_specs=pl.BlockSpec((1, tq, ATT_WIDTH), lambda b, j: (b, j, 0)),
            out_shape=jax.ShapeDtypeStruct((bsz, per * tq, ATT_WIDTH), F32),
            scratch_shapes=[pltpu.VMEM((tk, tq), F32)],
            compiler_params=_cp(("parallel", "arbitrary")),
            name=f"attn_prompt_{tk}",
        )(q3, iq3, iwt, k3, vt, ik3, tri)
        outs.append(out)
    return jnp.concatenate(outs, axis=1).reshape(bsz * seq, ATT_WIDTH)


def _attn_sample_kernel(pt_ref, q_ref, iq_ref, misc_ref, kn_ref, vn_ref, ik4n_ref, *rest, npg, topk):
    del pt_ref
    kp, vp, ikp = rest[0:npg], rest[npg:2 * npg], rest[2 * npg:3 * npg]
    o_ref, kbuf, vbuf, ikbuf, key_sc, p_sc = rest[3 * npg:]
    tq = q_ref.shape[1]
    lk = kbuf.shape[1]
    past = lk - LANE
    p = pl.program_id(1)
    for j in range(npg):
        col = pl.multiple_of((p * npg + j) * PAGE_SIZE, PAGE_SIZE)
        kbuf[:, pl.ds(col, PAGE_SIZE)] = kp[j][0]
        vbuf[:, pl.ds(col, PAGE_SIZE)] = vp[j][0]
        ikbuf[:, pl.ds(col, PAGE_SIZE)] = ikp[j][0]

    @pl.when(p == pl.num_programs(1) - 1)
    def _():
        kbuf[:, past:lk] = _pad_rows(kn_ref[0], LANE).T
        vbuf[:, past:lk] = _pad_rows(vn_ref[0], LANE).T
        ikbuf[:, past:lk] = _pad_rows(ik4n_ref[0], LANE).T[0:IDX_DIM, :]
        iq = iq_ref[0]
        iqs = jnp.concatenate([iq[:, h * IDX_DIM:(h + 1) * IDX_DIM] for h in range(IDX_HEADS)], axis=0)
        misc = misc_ref[0]
        iw = jnp.concatenate([misc[:, IDX_DIM + h:IDX_DIM + h + 1] for h in range(IDX_HEADS)], axis=0)
        wl = jnp.maximum(_dot(iqs, ikbuf[...]), 0.0) * (iw * (IDX_HEADS ** -0.5 * IDX_DIM ** -0.5))
        score = (wl[0:tq] + wl[tq:2 * tq]) + (wl[2 * tq:3 * tq] + wl[3 * tq:4 * tq])
        s_idx = _lane_iota((tq, lk))
        adm = s_idx <= past + _row_iota((tq, lk))
        key_sc[...] = jnp.where(adm, score, -FAR)
        sel = _topk_select(key_sc, s_idx, 1, (lk - 1).bit_length(), p_sc, topk)
        bias = jnp.where(sel > 0, 0.0, NEG)
        heads = ATT_HEADS // KV_HEADS
        q = q_ref[0] * HEAD_DIM ** -0.5
        qg = jnp.concatenate([_group_queries(q, g) for g in range(KV_HEADS)], axis=0)
        s = _dot(qg, kbuf[...]) + jnp.concatenate([bias] * ATT_HEADS, axis=0)
        pr = jnp.exp(s - jnp.max(s, axis=-1, keepdims=True))
        o = _dot_nt(pr, vbuf[...]) / jnp.sum(pr, axis=-1, keepdims=True)
        o_ref[0] = _ungroup_outputs([o[g * heads * tq:(g + 1) * heads * tq] for g in range(KV_HEADS)], tq)


def attn_sample(qr, kr, v, iqr, ik4, misc, cache_kt, cache_vt, cache_ikt, page_table, layer, dseq, npg=8):
    db, n_pages = page_table.shape
    n_pool = cache_kt.shape[0] // DEPTH
    past = n_pages * PAGE_SIZE
    lk = past + LANE
    topk = min(TOPK_MAX, (past + dseq) // 4)
    base = layer * n_pool
    r3 = lambda a: a.reshape(db, dseq, a.shape[-1])
    row_spec = lambda w: pl.BlockSpec((1, dseq, w), lambda b, p, pt: (b, 0, 0))

    def page_spec(w, j):
        return pl.BlockSpec((1, w, PAGE_SIZE), lambda b, p, pt, j=j: (pt[b, p * npg + j] + base, 0, 0))

    in_specs = [row_spec(ATT_WIDTH), row_spec(LANE), row_spec(LANE),
                row_spec(LANE), row_spec(LANE), row_spec(LANE)]
    in_specs += [page_spec(KVW, j) for j in range(npg)]
    in_specs += [page_spec(KVW, j) for j in range(npg)]
    in_specs += [page_spec(IDX_DIM, j) for j in range(npg)]
    out = pl.pallas_call(
        functools.partial(_attn_sample_kernel, npg=npg, topk=topk),
        grid_spec=pltpu.PrefetchScalarGridSpec(
            num_scalar_prefetch=1,
            grid=(db, n_pages // npg),
            in_specs=in_specs,
            out_specs=pl.BlockSpec((1, dseq, ATT_WIDTH), lambda b, p, pt: (b, 0, 0)),
            scratch_shapes=[pltpu.VMEM((KVW, lk), F32), pltpu.VMEM((KVW, lk), F32),
                            pltpu.VMEM((IDX_DIM, lk), F32), pltpu.VMEM((dseq, lk), F32),
                            pltpu.VMEM((dseq, 1), jnp.int32)]),
        out_shape=jax.ShapeDtypeStruct((db, dseq, ATT_WIDTH), F32),
        compiler_params=_cp(("parallel", "arbitrary")),
        name="attn_sample",
    )(page_table, r3(qr), r3(iqr), r3(misc), r3(kr), r3(v), r3(ik4),
      *([cache_kt] * npg), *([cache_vt] * npg), *([cache_ikt] * npg))
    return out.reshape(db * dseq, ATT_WIDTH)


def _s5_param_kernel(ldt_ref, are_ref, aim_ref, bre_ref, bim_ref, abar_ref, win_ref):
    dt = jnp.exp(ldt_ref[...])
    a_re, a_im = are_ref[...], aim_ref[...]
    mag = jnp.exp(dt * a_re)
    abar_re = mag * jnp.cos(dt * a_im)
    abar_im = mag * jnp.sin(dt * a_im)
    den = a_re * a_re + a_im * a_im
    num_re = abar_re - 1.0
    coef_re = (num_re * a_re + abar_im * a_im) / den
    coef_im = (abar_im * a_re - num_re * a_im) / den
    abar_ref[:, 0:S5_NS] = jnp.broadcast_to(abar_re, (8, S5_NS))
    abar_ref[:, S5_NS:2 * S5_NS] = jnp.broadcast_to(abar_im, (8, S5_NS))
    b_re, b_im = bre_ref[...], bim_ref[...]
    win_ref[:, 0:S5_NS] = _bf(coef_re * b_re - coef_im * b_im)
    win_ref[:, S5_NS:2 * S5_NS] = _bf(coef_re * b_im + coef_im * b_re)


def _block_diag_in(b):
    eye = jnp.eye(S5_GROUPS, dtype=b.dtype)
    return jnp.einsum('gnc,gh->gchn', b, eye).reshape(S5_WIDTH, S5_NS)


def _block_diag_out(c):
    eye = jnp.eye(S5_GROUPS, dtype=c.dtype)
    return jnp.einsum('gcn,gh->gnhc', c, eye).reshape(S5_NS, S5_WIDTH)


def s5_params(log_dt, a_re, a_im, b_re, b_im):
    per_state = lambda a: a.reshape(1, S5_NS)
    ldt = per_state(jnp.broadcast_to(log_dt[:, None], (S5_GROUPS, S5_STATE)))
    return pl.pallas_call(
        _s5_param_kernel,
        out_shape=[jax.ShapeDtypeStruct((8, 2 * S5_NS), F32),
                   jax.ShapeDtypeStruct((S5_WIDTH, 2 * S5_NS), BF16)],
        compiler_params=pltpu.CompilerParams(vmem_limit_bytes=VMEM_LIMIT),
        name="s5_params",
    )(ldt, per_state(a_re), per_state(a_im), _block_diag_in(b_re), _block_diag_in(b_im))


def _gelu_tanh(x):
    return 0.5 * x * (1.0 + jnp.tanh(math.sqrt(2.0 / math.pi) * (x + 0.044715 * (x * x * x))))


S5_LANES = 512


def _s5_kernel(u_ref, abar_ref, win_ref, h0_ref, wout_ref, d_ref, wglu_ref, bglu_ref,
               y_ref, hn_ref, s_sc, *, bsz, tc):
    c = pl.program_id(0)

    @pl.when(c == 0)
    def _():
        hn_ref[...] = h0_ref[...]

    u = u_ref[...]
    s_sc[...] = jnp.dot(_bf(u), win_ref[...], preferred_element_type=F32)
    nchunk = S5_NS // S5_LANES
    for rg in range(bsz // 8):
        rows = slice(rg * 8, rg * 8 + 8)

        def body(t, carry):
            row0 = pl.multiple_of(t * bsz + rg * 8, 8)
            new = []
            for cc in range(nchunk):
                lre = slice(cc * S5_LANES, (cc + 1) * S5_LANES)
                lim = slice(S5_NS + cc * S5_LANES, S5_NS + (cc + 1) * S5_LANES)
                xr, xi = carry[2 * cc], carry[2 * cc + 1]
                ar, ai = abar_ref[:, lre], abar_ref[:, lim]
                nr = (ar * xr - ai * xi) + s_sc[pl.ds(row0, 8), lre]
                ni = (ar * xi + ai * xr) + s_sc[pl.ds(row0, 8), lim]
                s_sc[pl.ds(row0, 8), lre] = nr
                s_sc[pl.ds(row0, 8), lim] = ni
                new += [nr, ni]
            return tuple(new)

        init = []
        for cc in range(nchunk):
            init += [hn_ref[rows, cc * S5_LANES:(cc + 1) * S5_LANES],
                     hn_ref[rows, S5_NS + cc * S5_LANES:S5_NS + (cc + 1) * S5_LANES]]
        fin = lax.fori_loop(0, tc, body, tuple(init))
        for cc in range(nchunk):
            hn_ref[rows, cc * S5_LANES:(cc + 1) * S5_LANES] = fin[2 * cc]
            hn_ref[rows, S5_NS + cc * S5_LANES:S5_NS + (cc + 1) * S5_LANES] = fin[2 * cc + 1]

    y = (jnp.dot(_bf(s_sc[:, 0:S5_NS]), wout_ref[0:S5_NS, :], preferred_element_type=F32)
         - jnp.dot(_bf(s_sc[:, S5_NS:2 * S5_NS]), wout_ref[S5_NS:2 * S5_NS, :], preferred_element_type=F32)
         + d_ref[...] * u)
    z = _gelu_tanh(y)
    y_ref[...] = z * _sigmoid(jnp.dot(_bf(z), wglu_ref[...], preferred_element_type=F32) + bglu_ref[...])


def s5_branch(u_tm, h0, abar8, win, wout_bf, d, wglu_bf, bglu, bsz, seq, tc):
    rows = tc * bsz
    const = lambda shape: pl.BlockSpec(shape, lambda c: (0,) * len(shape))
    return pl.pallas_call(
        functools.partial(_s5_kernel, bsz=bsz, tc=tc),
        grid=(seq // tc,),
        in_specs=[pl.BlockSpec((rows, S5_WIDTH), lambda c: (c, 0)),
                  const((8, 2 * S5_NS)), const((S5_WIDTH, 2 * S5_NS)), const((bsz, 2 * S5_NS)),
                  const((2 * S5_NS, S5_WIDTH)), const((1, S5_WIDTH)), const((S5_WIDTH, S5_WIDTH)),
                  const((1, S5_WIDTH))],
        out_specs=[pl.BlockSpec((rows, S5_WIDTH), lambda c: (c, 0)), const((bsz, 2 * S5_NS))],
        out_shape=[jax.ShapeDtypeStruct((seq * bsz, S5_WIDTH), F32),
                   jax.ShapeDtypeStruct((bsz, 2 * S5_NS), F32)],
        scratch_shapes=[pltpu.VMEM((rows, 2 * S5_NS), F32)],
        compiler_params=_cp(("arbitrary",)),
        name="s5_branch",
    )(u_tm, abar8, win, h0, wout_bf, d.reshape(1, S5_WIDTH), wglu_bf, bglu.reshape(1, S5_WIDTH))


def _pad_rows(x, rows):
    n = x.shape[0]
    return x if n == rows else jnp.concatenate([x, jnp.zeros((rows - n, x.shape[1]), x.dtype)], axis=0)


def _head_mean(x, amat_bf):
    return _dot2(x, amat_bf)


def _split3(x):
    a = x.astype(BF16)
    r = x - a.astype(F32)
    b = r.astype(BF16)
    return a, b, (r - b.astype(F32)).astype(BF16)


def _expand_state(tall, rep_bf, bdm):
    d = functools.partial(jnp.dot, preferred_element_type=F32)
    a, b, c = _split3(tall)
    return (d(a, rep_bf) + (d(b, rep_bf) + d(c, rep_bf))) * bdm


def _collapse_state(bd, rept_bf):
    d = functools.partial(jnp.dot, preferred_element_type=F32)
    a, b, c = _split3(bd)
    return d(a, rept_bf) + (d(b, rept_bf) + d(c, rept_bf))


def _rep_consts():
    rep = np.tile(np.eye(64, dtype=np.float32), (1, BW // 64))
    return jnp.asarray(rep, BF16), jnp.asarray(rep.T, BF16)


def _ret_kernel(q_ref, k_ref, v_ref, g_ref, tab_ref, s0_ref, dmat_ref, qdec_ref, kdec_ref, decm_ref,
                bdm_ref, amat_ref, rep_ref, rept_ref, y_ref, so_ref, s_ref, *, rows):
    n = q_ref.shape[0]

    @pl.when(pl.program_id(1) == 0)
    def _():
        s_ref[...] = _expand_state(s0_ref[0], rep_ref[...], bdm_ref[...])

    q = _pad_rows(_rope(q_ref[...], tab_ref, RET_DK // 2), rows)
    k = _pad_rows(_rope(k_ref[...], tab_ref, RET_DK // 2) * RET_DK ** -0.5, rows)
    v = _pad_rows(v_ref[...], rows)
    state = s_ref[...]
    inter = _dot(q * qdec_ref[...], state)
    lane = _lane_iota((rows, LANE))
    tiles = []
    for p in range(RET_HEADS // 2):
        lanes = slice(p * LANE, (p + 1) * LANE)
        qp, kp, vp = q[:, lanes], _bf(k[:, lanes]), v[:, lanes]
        acc = None
        for hh in range(2):
            mine = (lane < RET_DK) if hh == 0 else (lane >= RET_DK)
            s = _dot_nt(jnp.where(mine, qp, 0.0), kp) * dmat_ref[2 * p + hh]
            part = _dot(s, jnp.where(mine, vp, 0.0))
            acc = part if acc is None else acc + part
        tiles.append(acc)
    o = jnp.concatenate(tiles, axis=1) + inter
    new_state = state * decm_ref[...] + _dot_tn(k * kdec_ref[...], v) * bdm_ref[...]
    s_ref[...] = new_state

    @pl.when(pl.program_id(1) == pl.num_programs(1) - 1)
    def _():
        so_ref[0] = _collapse_state(new_state, rept_ref[...])

    amat = amat_ref[...]
    mu = _head_mean(o, amat)
    d = o - mu
    var = _head_mean(d * d, amat)
    on = d * lax.rsqrt(var + 1e-5)
    y_ref[...] = (_silu(g_ref[...]) * on[0:n]).astype(y_ref.dtype)


def _head_block_mask():
    h = np.arange(BW) // 64
    return (h[:, None] == h[None, :]).astype(np.float32)


def _ret_consts(rows, n_true):
    lg = np.log(1.0 - np.exp2(-5.0 - np.arange(RET_HEADS, dtype=np.float64)))
    i = np.arange(rows, dtype=np.float64)
    rel = i[:, None] - i[None, :]
    dmat = np.where(rel[None] >= 0, np.exp(np.minimum(rel[None], rows) * lg[:, None, None]), 0.0)
    lane_lg = np.repeat(lg, 64)[None, :]
    qdec = np.exp((i[:, None] + 1.0) * lane_lg)
    kdec = np.where(i[:, None] < n_true, np.exp((n_true - 1.0 - i[:, None]) * lane_lg), 0.0)
    bdm = _head_block_mask()
    decm = bdm * np.exp(n_true * np.repeat(lg, 64))[:, None]
    f = lambda a: jnp.asarray(a, F32)
    return f(dmat), f(qdec), f(kdec), f(decm), f(bdm), jnp.asarray(bdm / 64.0, BF16)


def ret_branch(proj, tab_r, s0, bsz, seq, rows, n):
    nch = seq // n
    dmat, qdec, kdec, decm, bdm, amat = _ret_consts(rows, n)
    rep, rept = _rep_consts()
    col = lambda j: pl.BlockSpec((n, BW), lambda b, c, j=j: (b * nch + c, j))
    const = lambda shape: pl.BlockSpec(shape, lambda b, c: (0,) * len(shape))
    return pl.pallas_call(
        functools.partial(_ret_kernel, rows=rows),
        grid=(bsz, nch),
        in_specs=[col(0), col(1), col(2), col(3),
                  pl.BlockSpec((3, n, LANE), lambda b, c: (0, c, 0)),
                  pl.BlockSpec((1, BW, RET_DK), lambda b, c: (b, 0, 0)),
                  const((RET_HEADS, rows, rows)), const((rows, BW)), const((rows, BW)),
                  const((BW, BW)), const((BW, BW)), const((BW, BW)), const((RET_DK, BW)), const((BW, RET_DK))],
        out_specs=[pl.BlockSpec((n, BW), lambda b, c: (b * nch + c, 0)),
                   pl.BlockSpec((1, BW, RET_DK), lambda b, c: (b, 0, 0))],
        out_shape=[jax.ShapeDtypeStruct((bsz * seq, BW), F32),
                   jax.ShapeDtypeStruct((bsz, BW, RET_DK), F32)],
        scratch_shapes=[pltpu.VMEM((BW, BW), F32)],
        compiler_params=_cp(("parallel", "arbitrary")),
        name="ret_branch",
    )(proj, proj, proj, proj, tab_r, s0, dmat, qdec, kdec, decm, bdm, amat, rep, rept)


def _softplus(x):
    return jnp.maximum(x, 0.0) + jnp.log(1.0 + jnp.exp(-jnp.abs(x)))


GDN_SUPER = ((0, 4 * GDN_CHUNK), (4 * GDN_CHUNK, 6 * GDN_CHUNK))
GM_BLOCK, GM_INCL, GM_STRICT, GM_EYE, GM_LEVEL0 = 0, 1, 2, 3, 4


def _stack_heads(a, bdm_rows):
    return jnp.concatenate([a] * (bdm_rows.shape[0] // GDN_CHUNK), axis=0) * bdm_rows


def _unstack_heads(parts):
    blocks = [p[i:i + GDN_CHUNK] for p in parts for i in range(0, p.shape[0], GDN_CHUNK)]
    out = blocks[0]
    for b in blocks[1:]:
        out = out + b
    return out


def _col_of_heads(a, s, e):
    return jnp.concatenate([jnp.broadcast_to(a[:, h * GDN_DK:h * GDN_DK + 1], (GDN_CHUNK, e - s))
                            for h in range(s // GDN_CHUNK, e // GDN_CHUNK)], axis=0)


def _row_of_heads(a_t, s, e):
    return jnp.concatenate([a_t[h * GDN_DK:h * GDN_DK + 1, :] for h in range(s // GDN_CHUNK, e // GDN_CHUNK)],
                           axis=1)


def _gdn_prepare(chunks, gm_ref):
    items = [(ci, s, e) for ci in range(len(chunks)) for s, e in GDN_SUPER]
    g_ts = [gc.T for _, _, _, _, gc in chunks]
    nmats, decs, qks = [], [], []
    for ci, s, e in items:
        qc, kc, _, bc, gc = chunks[ci]
        bdm_rows = gm_ref[GM_BLOCK, s:e, :]
        ks = _bf(_stack_heads(kc, bdm_rows))
        kk = _dot_nt(ks, ks)
        qks.append(_dot_nt(_stack_heads(qc, bdm_rows), ks))
        diff = _col_of_heads(gc, s, e) - _row_of_heads(g_ts[ci], s, e)
        dec = jnp.exp(jnp.where(gm_ref[GM_INCL, s:e, s:e] > 0.5, diff, NEG))
        decs.append(dec)
        nmats.append(_col_of_heads(bc, s, e) * (dec * gm_ref[GM_STRICT, s:e, s:e]) * kk)
    invs = [gm_ref[GM_EYE, s:e, s:e] - nm * gm_ref[GM_LEVEL0, s:e, s:e] for nm, (_, s, e) in zip(nmats, items)]
    for lvl in range(1, (GDN_CHUNK - 1).bit_length()):
        right = [_dot(nm * gm_ref[GM_LEVEL0 + lvl, s:e, s:e], inv) for nm, inv, (_, s, e) in zip(nmats, invs, items)]
        invs = [inv - _dot(inv, r) for inv, r in zip(invs, right)]
    w_st, uv_st = [[] for _ in chunks], [[] for _ in chunks]
    for inv, (ci, s, e) in zip(invs, items):
        _, kc, vc, bc, gc = chunks[ci]
        bdm_rows = gm_ref[GM_BLOCK, s:e, :]
        wu = _dot(inv, jnp.concatenate([_stack_heads(bc * jnp.exp(gc) * kc, bdm_rows),
                                        _stack_heads(bc * vc, bdm_rows)], axis=1))
        w_st[ci].append(wu[:, 0:BW])
        uv_st[ci].append(wu[:, BW:2 * BW])
    n_sb = len(GDN_SUPER)
    return [(_unstack_heads(w_st[ci]), _unstack_heads(uv_st[ci]),
             [qks[ci * n_sb + j] * decs[ci * n_sb + j] for j in range(n_sb)]) for ci in range(len(chunks))]


def _gdn_apply(chunk, prepared, hbd, gm_ref):
    qc, kc, _, _, gc = chunk
    w, uv, a_mats = prepared
    g_last = gc[GDN_CHUNK - 1:GDN_CHUNK, :]
    u = uv - _dot(w, hbd)
    o_st = [_dot(a_mats[i], _stack_heads(u, gm_ref[GM_BLOCK, s:e, :])) for i, (s, e) in enumerate(GDN_SUPER)]
    o = jnp.exp(gc) * _dot(qc, hbd) + _unstack_heads(o_st)
    h_new = jnp.exp(g_last) * hbd + _dot_tn(kc * jnp.exp(g_last - gc), u) * gm_ref[GM_BLOCK]
    return o, h_new


def _gdn_kernel(q_ref, k_ref, v_ref, g_ref, misc_ref, cs0_ref, cw_ref, alog_ref, dtb_ref, ng_ref, h0_ref,
                ea_ref, eb_ref, gm_ref, tri_ref, amat_ref, rep_ref, rept_ref, y_ref, ho_ref, cs_ref,
                xp_sc, h_sc, *, rows):
    n = q_ref.shape[0]
    cw = 3 * BW

    @pl.when(pl.program_id(1) == 0)
    def _():
        xp_sc[...] = jnp.zeros(xp_sc.shape, F32)
        xp_sc[5:8, :] = cs0_ref[0]
        h_sc[...] = _expand_state(h0_ref[0], rep_ref[...], gm_ref[GM_BLOCK])

    for j, r in enumerate((q_ref, k_ref, v_ref)):
        xp_sc[8:8 + n, j * BW:(j + 1) * BW] = r[...]
    conv = xp_sc[5:5 + rows, :] * cw_ref[0:1, :]
    for i in range(1, GDN_CONV):
        conv = conv + xp_sc[5 + i:5 + i + rows, :] * cw_ref[i:i + 1, :]
    tail = xp_sc[8 + n - 3:8 + n, :]
    xp_sc[5:8, :] = tail
    cs_ref[0] = tail
    xc = _silu(conv)
    valid = _row_iota((rows, BW)) < n
    bdm_bf = _bf(gm_ref[GM_BLOCK])
    q, k, v = xc[:, 0:BW], xc[:, BW:2 * BW], xc[:, 2 * BW:cw]
    q = q * lax.rsqrt(_dot2(q * q, bdm_bf) + EPS) * GDN_DK ** -0.5
    k = k * lax.rsqrt(_dot2(k * k, bdm_bf) + EPS)
    misc = _pad_rows(misc_ref[...], rows)
    beta = _sigmoid(_dot2(misc, eb_ref[...]))
    la = -jnp.exp(alog_ref[...]) * _softplus(_dot2(misc, ea_ref[...]) + dtb_ref[...])
    k = jnp.where(valid, k, 0.0)
    v = jnp.where(valid, v, 0.0)
    la = jnp.where(valid, la, 0.0)
    la_hi, la_lo = _split(la)
    tri = tri_ref[...]
    gall = (jnp.dot(tri, la_hi, preferred_element_type=F32)
            + jnp.dot(tri, la_lo, preferred_element_type=F32))
    c = GDN_CHUNK
    chunks = [(q[r], k[r], v[r], beta[r], gall[r]) for r in (slice(i, i + c) for i in range(0, rows, c))]
    prepared = _gdn_prepare(chunks, gm_ref)
    outs = []
    hbd = h_sc[...]
    for chunk, prep in zip(chunks, prepared):
        o, hbd = _gdn_apply(chunk, prep, hbd, gm_ref)
        outs.append(o)
    h_sc[...] = hbd

    @pl.when(pl.program_id(1) == pl.num_programs(1) - 1)
    def _():
        ho_ref[0] = _collapse_state(hbd, rept_ref[...])

    o = outs[0] if len(outs) == 1 else jnp.concatenate(outs, axis=0)
    on = o * lax.rsqrt(_dot2(o * o, amat_ref[...]) + EPS) * ng_ref[...]
    y_ref[...] = on[0:n] * _silu(g_ref[...])


def _gdn_consts(rows):
    lanes = np.arange(BW) // 64
    ea = np.zeros((LANE, BW), np.float32)
    eb = np.zeros((LANE, BW), np.float32)
    ea[MISC_DA + lanes, np.arange(BW)] = 1.0
    eb[MISC_DB + lanes, np.arange(BW)] = 1.0
    i = np.arange(rows)
    tri = ((i[:, None] // GDN_CHUNK == i[None, :] // GDN_CHUNK) & (i[:, None] >= i[None, :])).astype(np.float32)
    bdm = _head_block_mask()
    r = np.arange(BW)
    ri, ci = r[:, None] % GDN_CHUNK, r[None, :] % GDN_CHUNK
    gm = [bdm, bdm * (ri >= ci), bdm * (ri > ci), np.eye(BW, dtype=np.float32)]
    s = 1
    while s < GDN_CHUNK:
        gm.append(bdm * ((ri // (2 * s)) == (ci // (2 * s))) * ((ri // s) % 2 == 1) * ((ci // s) % 2 == 0))
        s *= 2
    return (jnp.asarray(ea, BF16), jnp.asarray(eb, BF16), jnp.asarray(np.stack(gm), F32), jnp.asarray(tri, BF16),
            jnp.asarray(bdm / 64.0, BF16))


def gdn_branch(proj, misc, cs0, conv_w, a_log, dt_bias, norm_g, h0, bsz, seq, rows, n):
    nblk = seq // n
    ea, eb, gm, tri, amat = _gdn_consts(rows)
    rep, rept = _rep_consts()
    per_lane = lambda a, reps: jnp.repeat(a, reps).reshape(1, BW) if reps > 1 else jnp.tile(a, BW // a.shape[0]).reshape(1, BW)
    col = lambda j: pl.BlockSpec((n, BW), lambda b, c, j=j: (b * nblk + c, j))
    const = lambda shape: pl.BlockSpec(shape, lambda b, c: (0,) * len(shape))
    per_b = lambda shape: pl.BlockSpec(shape, lambda b, c: (b,) + (0,) * (len(shape) - 1))
    cw = 3 * BW
    return pl.pallas_call(
        functools.partial(_gdn_kernel, rows=rows),
        grid=(bsz, nblk),
        in_specs=[col(4), col(5), col(6), col(7),
                  pl.BlockSpec((n, LANE), lambda b, c: (b * nblk + c, 0)),
                  per_b((1, GDN_CONV - 1, cw)), const((GDN_CONV, cw)),
                  const((1, BW)), const((1, BW)), const((1, BW)), per_b((1, BW, GDN_DK)),
                  const((LANE, BW)), const((LANE, BW)), const(tuple(gm.shape)), const((rows, rows)), const((BW, BW)),
                  const((GDN_DK, BW)), const((BW, GDN_DK))],
        out_specs=[pl.BlockSpec((n, BW), lambda b, c: (b * nblk + c, 0)),
                   per_b((1, BW, GDN_DK)), per_b((1, GDN_CONV - 1, cw))],
        out_shape=[jax.ShapeDtypeStruct((bsz * seq, BW), F32),
                   jax.ShapeDtypeStruct((bsz, BW, GDN_DK), F32),
                   jax.ShapeDtypeStruct((bsz, GDN_CONV - 1, cw), F32)],
        scratch_shapes=[pltpu.VMEM((rows + 8, cw), F32), pltpu.VMEM((BW, BW), F32)],
        compiler_params=_cp(("parallel", "arbitrary")),
        name="gdn_branch",
    )(proj, proj, proj, proj, misc, cs0, conv_w, per_lane(a_log, 64), per_lane(dt_bias, 64),
      per_lane(norm_g, 1), h0, ea, eb, gm, tri, amat, rep, rept)


_BRANCH_OFFS = (0, ATT_WIDTH, ATT_WIDTH + BW, ATT_WIDTH + 2 * BW, MIX_WIDTH)


def _merge_kernel(x_ref, g_ref, sc_ref, sh_ref, gm_ref, ya_ref, yb_ref, yc_ref, yd_ref,
                  wg_ref, wb_ref, wo_ref, o_ref):
    x = x_ref[...]
    h = _bf(_norm_mod(x, g_ref[...], sc_ref[0], sh_ref[0]))
    merged = None
    for b, y_ref in enumerate((ya_ref, yb_ref, yc_ref, yd_ref)):
        gate = _sigmoid(jnp.dot(h, wg_ref[:, b * D_MODEL:(b + 1) * D_MODEL], preferred_element_type=F32))
        term = gate * jnp.dot(_bf(y_ref[...]), wb_ref[_BRANCH_OFFS[b]:_BRANCH_OFFS[b + 1], :],
                              preferred_element_type=F32)
        merged = term if merged is None else merged + term
    y = jnp.dot(_bf(merged), wo_ref[...], preferred_element_type=F32)
    o_ref[...] = x + gm_ref[0] * y


def merge_out(x, g, sc, sh, gm, ya, yb, yc, yd, wg_bf, wb_bf, wo_bf, tm, tiles_per_seq):
    m = x.shape[0]
    ms = _mod_spec(tm, sc.shape[1], tiles_per_seq)
    row = lambda w: pl.BlockSpec((tm, w), lambda i: (i, 0))
    const = lambda shape: pl.BlockSpec(shape, lambda i: (0,) * len(shape))
    return pl.pallas_call(
        _merge_kernel,
        grid=(m // tm,),
        in_specs=[row(D_MODEL), const((1, D_MODEL)), ms, ms, ms, row(ATT_WIDTH), row(BW), row(BW), row(BW),
                  const((D_MODEL, N_BRANCH * D_MODEL)), const((MIX_WIDTH, D_MODEL)), const((D_MODEL, D_MODEL))],
        out_specs=row(D_MODEL),
        out_shape=jax.ShapeDtypeStruct((m, D_MODEL), F32),
        compiler_params=_cp(("parallel",)),
        name="merge_out",
    )(x, g.reshape(1, D_MODEL), sc, sh, gm, ya, yb, yc, yd, wg_bf, wb_bf, wo_bf)


def _top2(masked, lane):
    m1 = jnp.max(masked, axis=-1, keepdims=True)
    i1 = jnp.min(jnp.where(masked == m1, lane, LANE), axis=-1, keepdims=True)
    rest = jnp.where(lane == i1, -jnp.inf, masked)
    m2 = jnp.max(rest, axis=-1, keepdims=True)
    i2 = jnp.min(jnp.where(rest == m2, lane, LANE), axis=-1, keepdims=True)
    return m1, i1, m2, i2


def _route(scores, biased):
    lane = _lane_iota(scores.shape)
    grp = lane // EXPERTS_PER_GROUP
    best_val, best_grp = None, None
    for g in range(N_GROUPS):
        m1, _, m2, _ = _top2(jnp.where(grp == g, biased, -jnp.inf), lane)
        gs = m1 + m2
        if g == 0:
            best_val, best_grp = gs, jnp.zeros(gs.shape, jnp.int32)
        else:
            better = gs > best_val
            best_val = jnp.where(better, gs, best_val)
            best_grp = jnp.where(better, g, best_grp)
    _, e1, _, e2 = _top2(jnp.where(grp == best_grp, biased, -jnp.inf), lane)
    s1 = jnp.sum(jnp.where(lane == e1, scores, 0.0), axis=-1, keepdims=True)
    s2 = jnp.sum(jnp.where(lane == e2, scores, 0.0), axis=-1, keepdims=True)
    tot = s1 + s2
    return jnp.where(lane == e1, s1 / tot, 0.0) + jnp.where(lane == e2, s2 / tot, 0.0)


def _moe_kernel(x_ref, g_ref, sc_ref, sh_ref, gm_ref, wr_ref, rb_ref, w1_ref, w3_ref, w2_ref, fg_ref,
                o_ref, h_sc, comb_sc, acc_sc, *, final):
    e = pl.program_id(1)

    @pl.when(e == 0)
    def _():
        h = _norm_mod(x_ref[...], g_ref[...], sc_ref[0], sh_ref[0])
        h_sc[...] = _bf(h)
        scores = _sigmoid(_dot3(h, wr_ref[...]))
        comb_sc[...] = _route(scores, scores + rb_ref[...])
        acc_sc[...] = jnp.zeros(acc_sc.shape, F32)

    h = h_sc[...]
    comb = comb_sc[...]
    ce = jnp.sum(jnp.where(_lane_iota(comb.shape) == e, comb, 0.0), axis=-1, keepdims=True)
    hid = _silu(jnp.dot(h, w1_ref[0], preferred_element_type=F32)) * jnp.dot(h, w3_ref[0], preferred_element_type=F32)
    acc_sc[...] += jnp.dot(_bf(hid * ce), w2_ref[0], preferred_element_type=F32)

    @pl.when(e == pl.num_programs(1) - 1)
    def _():
        out = x_ref[...] + gm_ref[0] * acc_sc[...]
        if final:
            out = out * lax.rsqrt(jnp.mean(out * out, axis=-1, keepdims=True) + EPS) * fg_ref[...]
        o_ref[...] = out


def moe_out(x, g, sc, sh, gm, wr_pad, rb_pad, w1_bf, w3_bf, w2_bf, final_g, final, tm, tiles_per_seq):
    m = x.shape[0]
    ms = _mod_spec(tm, sc.shape[1], tiles_per_seq)
    const = lambda shape: pl.BlockSpec(shape, lambda i, e: (0,) * len(shape))
    return pl.pallas_call(
        functools.partial(_moe_kernel, final=final),
        grid=(m // tm, N_EXPERTS),
        in_specs=[pl.BlockSpec((tm, D_MODEL), lambda i, e: (i, 0)), const((1, D_MODEL)), ms, ms, ms,
                  const((D_MODEL, LANE)), const((1, LANE)),
                  pl.BlockSpec((1, D_MODEL, EXPERT_FF), lambda i, e: (e, 0, 0)),
                  pl.BlockSpec((1, D_MODEL, EXPERT_FF), lambda i, e: (e, 0, 0)),
                  pl.BlockSpec((1, EXPERT_FF, D_MODEL), lambda i, e: (e, 0, 0)),
                  const((1, D_MODEL))],
        out_specs=pl.BlockSpec((tm, D_MODEL), lambda i, e: (i, 0)),
        out_shape=jax.ShapeDtypeStruct((m, D_MODEL), F32),
        scratch_shapes=[pltpu.VMEM((tm, D_MODEL), BF16), pltpu.VMEM((tm, LANE), F32),
                        pltpu.VMEM((tm, D_MODEL), F32)],
        compiler_params=_cp(("parallel", "arbitrary")),
        name="moe_out",
    )(x, g.reshape(1, D_MODEL), sc, sh, gm, wr_pad, rb_pad, w1_bf, w3_bf, w2_bf, final_g.reshape(1, D_MODEL))


_REF_SPLITS = (ATT_WIDTH, KVW, KVW, IDX_HEADS * IDX_DIM, IDX_DIM, IDX_HEADS, BW,
               BW, BW, BW, BW, BW, BW, BW, GDN_HEADS, GDN_HEADS, BW)


def pack_w_in(w_in):
    offs = np.concatenate([[0], np.cumsum(_REF_SPLITS)])
    seg = [w_in[:, int(offs[i]):int(offs[i + 1])] for i in range(len(_REF_SPLITS))]
    (aq, ak, av, aiq, aik, aiw, bu, cq, ck, cv, cg, dq, dk, dv, da, db, dg) = seg
    zeros = lambda n: jnp.zeros((D_MODEL, n), w_in.dtype)
    misc = jnp.concatenate([aik, aiw, da, db, zeros(LANE - IDX_DIM - IDX_HEADS - 2 * GDN_HEADS)], axis=1)
    packed = jnp.concatenate([cq, ck, cv, cg, dq, dk, dv, dg, bu, aq, ak, av, aiq, zeros(2 * LANE)], axis=1)
    wt = jnp.concatenate([aiw.T, jnp.zeros((8 - IDX_HEADS, D_MODEL), w_in.dtype)], axis=0)
    return _bf(packed), misc, wt


def _time_major(a, bsz, seq):
    return a.reshape(bsz, seq, a.shape[-1]).transpose(1, 0, 2).reshape(seq * bsz, a.shape[-1])


def _batch_major(a, bsz, seq):
    return a.reshape(seq, bsz, a.shape[-1]).transpose(1, 0, 2).reshape(bsz * seq, a.shape[-1])


def _trunk_layer(x, mods, geom, attend, st, lw, final_g, final):
    bsz, seq, tm, tps, s5_tc, ret_rows, ret_n, gdn_rows, gdn_n = geom
    sh1, sc1, g1, sh2, sc2, g2 = mods
    proj, misc, iwt = in_proj(x, lw['norm1'], sc1, sh1, lw['w_in'], lw['w_misc'], lw['w_iwt'], tm, tps)
    qr, kr, v, iqr, ikr, ik4, vt = attn_prep(proj, misc, lw['tab_a'], lw['tab_i'], lw['tab_k'], tm, tps)
    ya = attend(qr, kr, v, vt, iqr, ikr, ik4, iwt, misc)
    u_tm = _time_major(proj[:, 8 * BW:9 * BW], bsz, seq)
    y_tm, s5_h = s5_branch(u_tm, st['s5'], lw['s5_abar'], lw['s5_win'], lw['s5_wout'], lw['s5_d'],
                           lw['s5_w_glu'], lw['s5_b_glu'], bsz, seq, s5_tc)
    yb = _batch_major(y_tm, bsz, seq)
    yc, ret_s = ret_branch(proj, lw['tab_r'], st['ret'], bsz, seq, ret_rows, ret_n)
    yd, gdn_s, conv_s = gdn_branch(proj, misc, st['conv'], lw['gdn_conv_w'], lw['gdn_a_log'], lw['gdn_dt_bias'],
                                   lw['gdn_norm_g'], st['gdn'], bsz, seq, gdn_rows, gdn_n)
    x = merge_out(x, lw['norm1'], sc1, sh1, g1, ya, yb, yc, yd, lw['w_gate'], lw['w_br'], lw['w_out'], tm, tps)
    x = moe_out(x, lw['norm2'], sc2, sh2, g2, lw['w_router'], lw['router_bias'], lw['w_e1'], lw['w_e3'],
                lw['w_e2'], final_g, final, tm, tps)
    new_st = {'k': kr, 'v': v, 'ik': ikr, 's5': s5_h, 'ret': ret_s, 'gdn': gdn_s, 'conv': conv_s}
    return x, new_st


def kernel(x_prompt, x_sample, c_prompt, c_sample, cache_k, cache_v, cache_idx_k, page_table,
           state_s5_re, state_s5_im, state_ret, state_gdn, state_gdn_conv,
           norm1_g, norm2_g, final_g, w_ada, b_ada, w_in,
           s5_a_re, s5_a_im, s5_b_re, s5_b_im, s5_c_re, s5_c_im, s5_d, s5_log_dt, s5_w_glu, s5_b_glu,
           gdn_conv_w, gdn_a_log, gdn_dt_bias, gdn_norm_g,
           w_br, w_gate, w_out, w_router, router_bias, w_e1, w_e3, w_e2):
    bsz, seq, _ = x_prompt.shape
    dbs, dseq, _ = x_sample.shape
    depth = w_in.shape[0]
    n_pool = cache_k.shape[1]
    past = page_table.shape[1] * PAGE_SIZE
    mp, ms = bsz * seq, dbs * dseq
    tm_p = 512
    pos_p = jnp.arange(seq, dtype=jnp.int32)
    pos_s = past + jnp.arange(dseq, dtype=jnp.int32)
    pos_s_tok = jnp.tile(pos_s, dbs)

    def tables(pos):
        return {'tab_a': _rope_tables(pos, ROT_DIMS, ROPE_THETA, HEAD_DIM),
                'tab_i': _rope_tables(pos, IDX_ROT, ROPE_THETA, IDX_DIM),
                'tab_k': _rope_tables(pos, IDX_ROT, ROPE_THETA, IDX_DIM, active=IDX_DIM)}

    tabs_p = dict(tables(pos_p), tab_r=_rope_tables(pos_p, RET_DK, RET_THETA, RET_DK))
    tabs_s = dict(tables(pos_s_tok), tab_r=_rope_tables(pos_s, RET_DK, RET_THETA, RET_DK))
    ck = cache_k.transpose(0, 1, 3, 4, 2).reshape(depth * n_pool, KVW, PAGE_SIZE)
    cv = cache_v.transpose(0, 1, 3, 4, 2).reshape(depth * n_pool, KVW, PAGE_SIZE)
    cik = cache_idx_k.transpose(0, 1, 3, 2).reshape(depth * n_pool, IDX_DIM, PAGE_SIZE)
    wr_pad = jnp.pad(w_router, ((0, 0), (0, LANE - N_EXPERTS)))
    rb_pad = jnp.pad(router_bias, (0, LANE - N_EXPERTS)).reshape(1, LANE)
    c_all = jnp.concatenate([c_prompt, c_sample], axis=0)

    geom_p = (bsz, seq, tm_p, seq // tm_p, 64, 256, 256, 256, 256)
    geom_s = (dbs, dseq, ms, 1, dseq, LANE, dseq, GDN_CHUNK, dseq)
    zero_st = {'s5': jnp.zeros((bsz, 2 * S5_NS), F32), 'ret': jnp.zeros((bsz, BW, RET_DK), F32),
               'gdn': jnp.zeros((bsz, BW, GDN_DK), F32), 'conv': jnp.zeros((bsz, GDN_CONV - 1, 3 * BW), F32)}

    xp = x_prompt.reshape(mp, D_MODEL)
    xs = x_sample.reshape(ms, D_MODEL)
    outs_p, outs_s = [], []
    for l in range(depth):
        w_in_p, w_misc, w_iwt = pack_w_in(w_in[l])
        abar8, win = s5_params(s5_log_dt[l], s5_a_re[l], s5_a_im[l], s5_b_re[l], s5_b_im[l])
        lw = {'norm1': norm1_g[l], 'norm2': norm2_g[l], 'w_in': w_in_p, 'w_misc': w_misc, 'w_iwt': w_iwt,
              's5_abar': abar8, 's5_win': win,
              's5_wout': _bf(jnp.concatenate([_block_diag_out(s5_c_re[l]), _block_diag_out(s5_c_im[l])], axis=0)),
              's5_d': s5_d[l], 's5_w_glu': _bf(s5_w_glu[l]), 's5_b_glu': s5_b_glu[l],
              'gdn_conv_w': gdn_conv_w[l], 'gdn_a_log': gdn_a_log[l], 'gdn_dt_bias': gdn_dt_bias[l],
              'gdn_norm_g': gdn_norm_g[l],
              'w_br': _bf(w_br[l]), 'w_gate': _bf(w_gate[l]), 'w_out': _bf(w_out[l]),
              'w_router': wr_pad, 'router_bias': rb_pad,
              'w_e1': _bf(w_e1[l]), 'w_e3': _bf(w_e3[l]), 'w_e2': _bf(w_e2[l])}
        mod = ada_mod(c_all, _bf(w_ada[l]), b_ada[l])
        mods = [mod[:, i * D_MODEL:(i + 1) * D_MODEL] for i in range(6)]
        mods_p = [m[:bsz].reshape(bsz, 1, D_MODEL) for m in mods]
        mods_s = [jnp.repeat(m[bsz:], dseq, axis=0).reshape(1, ms, D_MODEL) for m in mods]
        final = l == depth - 1

        def attend_p(qr, kr, v, vt, iqr, ikr, ik4, iwt, misc):
            return attn_prompt(qr, kr, vt, iqr, ik4, iwt, bsz, seq)

        def attend_s(qr, kr, v, vt, iqr, ikr, ik4, iwt, misc, l=l):
            return attn_sample(qr, kr, v, iqr, ik4, misc, ck, cv, cik, page_table, l, dseq)

        st_s = {'s5': jnp.concatenate([state_s5_re[l].reshape(dbs, S5_NS), state_s5_im[l].reshape(dbs, S5_NS)], axis=1),
                'ret': state_ret[l].reshape(dbs, BW, RET_DK), 'gdn': state_gdn[l].reshape(dbs, BW, GDN_DK),
                'conv': state_gdn_conv[l]}
        xp, ns_p = _trunk_layer(xp, mods_p, geom_p, attend_p, zero_st, dict(lw, **tabs_p), final_g, final)
        xs, ns_s = _trunk_layer(xs, mods_s, geom_s, attend_s, st_s, dict(lw, **tabs_s), final_g, final)
        outs_p.append(ns_p)
        outs_s.append(ns_s)

    def stack(outs, name, shape):
        return jnp.stack([o[name] for o in outs], axis=0).reshape((depth,) + shape)

    def states(outs, b, t):
        re = jnp.stack([o['s5'][:, :S5_NS] for o in outs], axis=0).reshape(depth, b, S5_GROUPS, S5_STATE)
        im = jnp.stack([o['s5'][:, S5_NS:] for o in outs], axis=0).reshape(depth, b, S5_GROUPS, S5_STATE)
        ret = stack(outs, 'ret', (b, RET_HEADS, RET_DK, RET_DK))
        gdn = stack(outs, 'gdn', (b, GDN_HEADS, GDN_DK, GDN_DK))
        return (stack(outs, 'k', (b, t, KV_HEADS, HEAD_DIM)), stack(outs, 'v', (b, t, KV_HEADS, HEAD_DIM)),
                stack(outs, 'ik', (b, t, IDX_DIM)), re, im, ret, gdn,
                stack(outs, 'conv', (b, GDN_CONV - 1, 3 * BW)))

    kp, vp, ikp, rep, imp, retp, gdnp, convp = states(outs_p, bsz, seq)
    ks_, vs_, iks, res, ims, rets, gdns, convs = states(outs_s, dbs, dseq)
    return (xp.reshape(bsz, seq, D_MODEL), xs.reshape(dbs, dseq, D_MODEL), kp, vp, ikp, ks_, vs_, iks,
            rep, imp, res, ims, retp, rets, gdnp, gdns, convp, convs)
```

```python
import functools
import math

import numpy as np
import jax
import jax.numpy as jnp
from jax import lax
from jax.experimental import pallas as pl
from jax.experimental.pallas import tpu as pltpu

F32 = jnp.float32
BF16 = jnp.bfloat16

D_MODEL = 1024
DEPTH = 2
PAST_LEN = 8192
PAGE_SIZE = 128
ATT_HEADS = 8
KV_HEADS = 2
HEAD_DIM = 64
ROT_DIMS = HEAD_DIM // 4
ROPE_THETA = 500000.0
IDX_HEADS = 4
IDX_DIM = 32
IDX_ROT = IDX_DIM // 4
TOPK_MAX = 256
S5_GROUPS = 24
S5_GROUP_CH = 16
S5_STATE = 64
S5_WIDTH = S5_GROUPS * S5_GROUP_CH
S5_NS = S5_GROUPS * S5_STATE
RET_HEADS = 6
RET_DK = 64
RET_THETA = 10000.0
GDN_HEADS = 6
GDN_DK = 64
GDN_CONV = 4
GDN_CHUNK = 64
ATT_WIDTH = ATT_HEADS * HEAD_DIM
BW = 384
KVW = KV_HEADS * HEAD_DIM
N_BRANCH = 4
MIX_WIDTH = ATT_WIDTH + 3 * BW
N_EXPERTS = 16
N_GROUPS = 4
EXPERTS_PER_GROUP = 4
EXPERT_FF = 256
EPS = 1e-6

PROJ_COLS = 12 * BW
ATT_COL0 = 9 * BW
MISC_DA = 36
MISC_DB = 42
LANE = 128
VMEM_LIMIT = 56 * 1024 * 1024
NEG = -1e30


def _cp(sem):
    return pltpu.CompilerParams(dimension_semantics=sem, vmem_limit_bytes=VMEM_LIMIT)


def _bf(x):
    return x.astype(BF16)


def _dot(a, b):
    return jnp.dot(_bf(a), _bf(b), preferred_element_type=F32)


def _dot_nt(a, b):
    return lax.dot_general(_bf(a), _bf(b), (((1,), (1,)), ((), ())), preferred_element_type=F32)


def _dot_tn(a, b):
    return lax.dot_general(_bf(a), _bf(b), (((0,), (0,)), ((), ())), preferred_element_type=F32)


def _split(x):
    hi = x.astype(BF16)
    lo = (x - hi.astype(F32)).astype(BF16)
    return hi, lo


def _dot3(a, b):
    ah, al = _split(a)
    bh, bl = _split(b)
    d = functools.partial(jnp.dot, preferred_element_type=F32)
    return d(ah, bh) + (d(ah, bl) + d(al, bh))


def _dot2(a, b01):
    ah, al = _split(a)
    d = functools.partial(jnp.dot, preferred_element_type=F32)
    return d(ah, b01) + d(al, b01)


def _sigmoid(x):
    return 1.0 / (1.0 + jnp.exp(-x))


def _silu(x):
    return x * _sigmoid(x)


def _lane_iota(shape):
    return lax.broadcasted_iota(jnp.int32, shape, len(shape) - 1)


def _row_iota(shape):
    return lax.broadcasted_iota(jnp.int32, shape, len(shape) - 2)


def _ada_kernel(c_ref, w_ref, b_ref, o_ref):
    o_ref[...] = _dot(_silu(c_ref[...]), w_ref[...]) + b_ref[...]


def ada_mod(c, w_bf, b):
    n = c.shape[0]
    cols = w_bf.shape[1]
    tn = 1024
    return pl.pallas_call(
        _ada_kernel,
        grid=(cols // tn,),
        in_specs=[pl.BlockSpec((n, D_MODEL), lambda j: (0, 0)),
                  pl.BlockSpec((D_MODEL, tn), lambda j: (0, j)),
                  pl.BlockSpec((1, tn), lambda j: (0, j))],
        out_specs=pl.BlockSpec((n, tn), lambda j: (0, j)),
        out_shape=jax.ShapeDtypeStruct((n, cols), F32),
        compiler_params=_cp(("parallel",)),
        name="ada_mod",
    )(c, w_bf, b.reshape(1, cols))


def _norm_mod(x, g, sc, sh):
    y = x * lax.rsqrt(jnp.mean(x * x, axis=-1, keepdims=True) + EPS) * g
    return y * (1.0 + sc) + sh


def _in_kernel(x_ref, g_ref, sc_ref, sh_ref, w_ref, wm_ref, wt_ref, o_ref, om_ref, ot_ref, h_sc):
    @pl.when(pl.program_id(1) == 0)
    def _():
        h = _norm_mod(x_ref[...], g_ref[...], sc_ref[0], sh_ref[0])
        h_sc[...] = h.astype(BF16)
        hh, hl = _split(h)
        wmh, wml = _split(wm_ref[...])
        d = functools.partial(jnp.dot, preferred_element_type=F32)
        om_ref[...] = d(hh, wmh) + (d(hh, wml) + d(hl, wmh))
        wth, wtl = _split(wt_ref[...])
        nt = functools.partial(lax.dot_general, dimension_numbers=(((1,), (1,)), ((), ())),
                               preferred_element_type=F32)
        ot_ref[...] = nt(wth, hh) + (nt(wth, hl) + nt(wtl, hh))

    o_ref[...] = jnp.dot(h_sc[...], w_ref[...], preferred_element_type=F32)


def _mod_spec(tm, mod_rows, tiles_per_seq):
    if mod_rows == 1:
        return pl.BlockSpec((1, 1, D_MODEL), lambda i, *_: (i // tiles_per_seq, 0, 0))
    return pl.BlockSpec((1, tm, D_MODEL), lambda i, *_: (i, 0, 0))


def in_proj(x, g, sc, sh, w_bf, w_misc, w_iwt, tm, tiles_per_seq):
    m = x.shape[0]
    tn = PROJ_COLS
    mod_rows = sc.shape[1]
    ms = _mod_spec(tm, mod_rows, tiles_per_seq)
    return pl.pallas_call(
        _in_kernel,
        grid=(m // tm, PROJ_COLS // tn),
        in_specs=[pl.BlockSpec((tm, D_MODEL), lambda i, j: (i, 0)),
                  pl.BlockSpec((1, D_MODEL), lambda i, j: (0, 0)),
                  ms, ms,
                  pl.BlockSpec((D_MODEL, tn), lambda i, j: (0, j)),
                  pl.BlockSpec((D_MODEL, LANE), lambda i, j: (0, 0)),
                  pl.BlockSpec((8, D_MODEL), lambda i, j: (0, 0))],
        out_specs=[pl.BlockSpec((tm, tn), lambda i, j: (i, j)),
                   pl.BlockSpec((tm, LANE), lambda i, j: (i, 0)),
                   pl.BlockSpec((8, tm), lambda i, j: (0, i))],
        out_shape=[jax.ShapeDtypeStruct((m, PROJ_COLS), F32),
                   jax.ShapeDtypeStruct((m, LANE), F32),
                   jax.ShapeDtypeStruct((8, m), F32)],
        scratch_shapes=[pltpu.VMEM((tm, D_MODEL), BF16)],
        compiler_params=_cp(("parallel", "arbitrary")),
        name="in_proj",
    )(x, g.reshape(1, D_MODEL), sc, sh, w_bf, w_misc, w_iwt)


def _rope_tables(pos, rot_dims, theta, period, width=LANE, active=None):
    half = rot_dims // 2
    inv_freq = jnp.power(jnp.float32(theta), -jnp.arange(half, dtype=F32) / half)
    ang = pos.astype(F32)[:, None] * inv_freq
    cos, sin = jnp.cos(ang), jnp.sin(ang)
    t = pos.shape[0]
    c = jnp.concatenate([cos, cos, jnp.ones((t, period - rot_dims), F32)], axis=1)
    s_up = jnp.concatenate([-sin, jnp.zeros((t, period - half), F32)], axis=1)
    s_dn = jnp.concatenate([jnp.zeros((t, half), F32), sin, jnp.zeros((t, period - rot_dims), F32)], axis=1)
    reps = width // period
    tab = jnp.stack([jnp.tile(a, (1, reps)) for a in (c, s_up, s_dn)], axis=0)
    if active is not None:
        ident = jnp.stack([jnp.ones((t, width), F32), jnp.zeros((t, width), F32),
                           jnp.zeros((t, width), F32)], axis=0)
        tab = jnp.where(jnp.arange(width) < active, tab, ident)
    return tab


def _rope(x, tab_ref, half):
    w = x.shape[1]
    reps = w // LANE

    def wide(k):
        t = tab_ref[k]
        return t if reps == 1 else jnp.concatenate([t] * reps, axis=1)

    return (x * wide(0) + pltpu.roll(x, w - half, axis=1) * wide(1)
            + pltpu.roll(x, half, axis=1) * wide(2))


def _prep_kernel(p_ref, m_ref, ta_ref, ti_ref, tk_ref, q_ref, k_ref, v_ref, iq_ref, ik_ref, ik4_ref, vt_ref):
    q_ref[...] = _rope(p_ref[:, 0:ATT_WIDTH], ta_ref, ROT_DIMS // 2)
    k_ref[...] = _rope(p_ref[:, 512:640], ta_ref, ROT_DIMS // 2)
    v_ref[...] = p_ref[:, 640:768]
    vt_ref[0] = p_ref[:, 640:768].T
    iq_ref[...] = _rope(p_ref[:, 768:896], ti_ref, IDX_ROT // 2)
    ikr = _rope(m_ref[...], tk_ref, IDX_ROT // 2)
    ik_ref[...] = ikr[:, 0:IDX_DIM]
    m = jnp.where(_lane_iota(ikr.shape) < IDX_DIM, ikr, 0.0)
    ik4_ref[...] = (m + pltpu.roll(m, 32, axis=1)) + (pltpu.roll(m, 64, axis=1) + pltpu.roll(m, 96, axis=1))


def attn_prep(proj, misc, tab_a, tab_i, tab_k, tm, tiles_per_seq):
    m = proj.shape[0]
    tspec = pl.BlockSpec((3, tm, LANE), lambda i: (0, i % tiles_per_seq, 0))
    widths = (ATT_WIDTH, KVW, KVW, LANE, IDX_DIM, LANE)
    seq = tm * tiles_per_seq
    return pl.pallas_call(
        _prep_kernel,
        grid=(m // tm,),
        in_specs=[pl.BlockSpec((tm, 3 * BW), lambda i: (i, ATT_COL0 // (3 * BW))),
                  pl.BlockSpec((tm, LANE), lambda i: (i, 0)), tspec, tspec, tspec],
        out_specs=[pl.BlockSpec((tm, w), lambda i: (i, 0)) for w in widths]
        + [pl.BlockSpec((1, KVW, tm), lambda i: (i // tiles_per_seq, 0, i % tiles_per_seq))],
        out_shape=[jax.ShapeDtypeStruct((m, w), F32) for w in widths]
        + [jax.ShapeDtypeStruct((m // seq, KVW, seq), F32)],
        compiler_params=_cp(("parallel",)),
        name="attn_prep",
    )(proj, misc, tab_a, tab_i, tab_k)


BISECT_MAX_ITERS = 48
BISECT_UNROLL = 4
FAR = 2.0 ** 126


def _count(ones, axis):
    return jnp.sum(ones, axis=axis, keepdims=True)


COUNT_ROWS = 32
COUNT_ACCS = 4


def _count_where(s_ref, pred, axis):
    n = s_ref.shape[0]
    if axis != 0 or n % (COUNT_ROWS * COUNT_ACCS) != 0:
        return _count(pred(s_ref[...]), axis)
    accs = [None] * COUNT_ACCS
    for j, i in enumerate(range(0, n, COUNT_ROWS)):
        part = pred(s_ref[i:i + COUNT_ROWS, :])
        a = j % COUNT_ACCS
        accs[a] = part if accs[a] is None else accs[a] + part
    return jnp.sum((accs[0] + accs[1]) + (accs[2] + accs[3]), axis=0, keepdims=True)


def _bisect_topk(s_ref, axis, topk):
    kshape = tuple(1 if a == axis else n for a, n in enumerate(s_ref.shape))
    s = s_ref[...]
    lo0 = jnp.min(jnp.where(s > -FAR, s, FAR), axis=axis, keepdims=True)
    mx = jnp.max(s, axis=axis, keepdims=True)
    hi0 = mx + (jnp.abs(mx) * 2.0 ** -20 + 1e-30)
    cnt_lo0 = _count_where(s_ref, lambda t: jnp.where(t >= lo0, 1, 0), axis)
    n_zero = _count_where(s_ref, lambda t: jnp.where(t == 0.0, 1, 0), axis)

    def pending(lo, hi, cnt_lo, cnt_hi):
        only_zeros = jnp.where(lo <= 0.0, jnp.where(hi > 0.0, jnp.where(cnt_lo - cnt_hi == n_zero, 1, 0), 0), 0)
        return jnp.max(jnp.where(cnt_lo <= topk, 0, 1 - only_zeros))

    def cond(c):
        return jnp.logical_and(c[0] < BISECT_MAX_ITERS, c[1] > 0)

    def body(c):
        it, _, lo, hi, cnt_lo, cnt_hi = c
        for _ in range(BISECT_UNROLL):
            mid = 0.5 * lo + 0.5 * hi
            cm = _count_where(s_ref, lambda t: jnp.where(t >= mid, 1, 0), axis)
            ge = cm >= topk
            lo, hi = jnp.where(ge, mid, lo), jnp.where(ge, hi, mid)
            cnt_lo, cnt_hi = jnp.where(ge, cm, cnt_lo), jnp.where(ge, cnt_hi, cm)
        return it + BISECT_UNROLL, pending(lo, hi, cnt_lo, cnt_hi), lo, hi, cnt_lo, cnt_hi

    cnt_hi0 = jnp.zeros(kshape, jnp.int32)
    init = (jnp.int32(0), pending(lo0, hi0, cnt_lo0, cnt_hi0), lo0, hi0, cnt_lo0, cnt_hi0)
    _, _, lo, hi, cnt_lo, cnt_hi = lax.while_loop(cond, body, init)
    return lo, hi, cnt_lo, cnt_hi


def _topk_select(s_ref, idx, axis, n_idx_bits, p_sc, topk):
    kshape = tuple(1 if a == axis else n for a, n in enumerate(s_ref.shape))
    lo, hi, cnt_lo, cnt_hi = _bisect_topk(s_ref, axis, topk)
    need = topk - cnt_hi
    p_sc[...] = jnp.full(kshape, (1 << n_idx_bits) - 1, jnp.int32)

    @pl.when(jnp.max(cnt_lo - cnt_hi - need) > 0)
    def _():
        tied = jnp.where(s_ref[...] >= lo, jnp.where(s_ref[...] >= hi, 0, 1), 0)

        def ibody(i, p):
            cand = p + jnp.left_shift(jnp.int32(1), n_idx_bits - 1 - i)
            taken = _count(jnp.where(idx < cand, tied, 0), axis)
            return jnp.where(taken < need, cand, p)

        p_sc[...] = lax.fori_loop(0, n_idx_bits, ibody, jnp.zeros(kshape, jnp.int32))

    s = s_ref[...]
    return jnp.where(s >= hi, 1, jnp.where(s >= lo, jnp.where(idx <= p_sc[...], 1, 0), 0))


def _topk_bias_keys_major(s_ref, tri_ref, topk):
    tk, tq = s_ref.shape
    lo, hi, _, cnt_hi = _bisect_topk(s_ref, 0, topk)
    need = (topk - cnt_hi).astype(F32)
    tri = tri_ref[...]
    offset = jnp.zeros((1, tq), F32)
    parts = []
    for c in range(tk // tri.shape[0]):
        s = s_ref[c * tri.shape[0]:(c + 1) * tri.shape[0], :]
        cand = jnp.where(s >= lo, jnp.where(s >= hi, 0.0, 1.0), 0.0)
        rank = jnp.dot(tri, _bf(cand), preferred_element_type=F32) + offset
        offset = rank[tri.shape[0] - 1:tri.shape[0], :]
        parts.append(jnp.where(s >= hi, 0.0, jnp.where(cand * rank > 0.5, jnp.where(rank <= need, 0.0, NEG), NEG)))
    return jnp.concatenate(parts, axis=0)


def _group_queries(q, g):
    tiles = []
    keep = (_lane_iota((q.shape[0], LANE)) // HEAD_DIM) == g
    for hl in range(ATT_HEADS // KV_HEADS):
        h = g * (ATT_HEADS // KV_HEADS) + hl
        t = q[:, (h // 2) * LANE:(h // 2 + 1) * LANE]
        if h % 2 != g:
            t = pltpu.roll(t, HEAD_DIM, axis=1)
        tiles.append(jnp.where(keep, t, 0.0))
    return jnp.concatenate(tiles, axis=0)


def _ungroup_outputs(o_groups, tq):
    low = _lane_iota((tq, LANE)) < HEAD_DIM
    tiles = []
    for j in range(ATT_HEADS // 2):
        halves = []
        for h in (2 * j, 2 * j + 1):
            g, hl = divmod(h, ATT_HEADS // KV_HEADS)
            t = o_groups[g][hl * tq:(hl + 1) * tq]
            if h % 2 != g:
                t = pltpu.roll(t, HEAD_DIM, axis=1)
            halves.append(t)
        tiles.append(jnp.where(low, halves[0], halves[1]))
    return jnp.concatenate(tiles, axis=1)


def _masked_attention_keys_major(q, kb, vt, bias_t):
    tq = q.shape[0]
    heads = ATT_HEADS // KV_HEADS
    tk = kb.shape[0]
    bias4 = jnp.concatenate([bias_t] * heads, axis=1)
    vrow_group = _row_iota((LANE, tk)) // HEAD_DIM
    q = q * HEAD_DIM ** -0.5
    normed = []
    for g in range(KV_HEADS):
        st = _dot_nt(kb, _group_queries(q, g)) + bias4
        p = jnp.exp(_bf(st - jnp.max(st, axis=0, keepdims=True)))
        ot = jnp.dot(_bf(jnp.where(vrow_group == g, vt, 1.0)), p, preferred_element_type=F32)
        other = (1 - g) * HEAD_DIM
        normed.append(ot[g * HEAD_DIM:(g + 1) * HEAD_DIM, :] / ot[other:other + 1, :])
    tiles = []
    for j in range(ATT_HEADS // 2):
        g, hl = divmod(2 * j, heads)
        pair = jnp.concatenate([normed[g][:, hl * tq:(hl + 1) * tq], normed[g][:, (hl + 1) * tq:(hl + 2) * tq]], axis=0)
        tiles.append(pair.T)
    return jnp.concatenate(tiles, axis=1)


def _attn_prompt_kernel(q_ref, iq_ref, iwt_ref, k_ref, vt_ref, ik4_ref, tri_ref, o_ref, s_sc, *, qblk0, tq, topk):
    tk = k_ref.shape[1]
    q0 = (qblk0 + pl.program_id(1)) * tq
    iq = iq_ref[0]
    head_of_lane = _lane_iota((tq, LANE)) // IDX_DIM
    iq4 = jnp.concatenate([jnp.where(head_of_lane == h, iq, 0.0) for h in range(IDX_HEADS)], axis=0)
    lg = _dot_nt(ik4_ref[0], iq4)
    iwt = iwt_ref[...] * (IDX_HEADS ** -0.5 * IDX_DIM ** -0.5)
    score = None
    for h in range(IDX_HEADS):
        part = jnp.maximum(lg[:, h * tq:(h + 1) * tq], 0.0) * iwt[h:h + 1, :]
        score = part if score is None else score + part
    adm = _row_iota((tk, tq)) <= q0 + _lane_iota((tk, tq))
    s_sc[...] = jnp.where(adm, score, -FAR)
    bias_t = _topk_bias_keys_major(s_sc, tri_ref, topk)
    o_ref[0] = _masked_attention_keys_major(q_ref[0], _bf(k_ref[0]), vt_ref[0], bias_t)


PREFIX_ROWS = 256


def attn_prompt(qr, kr, vt, iqr, ik4, iwt, bsz, seq, n_classes=4, tq=128):
    topk = min(TOPK_MAX, seq // 4)
    nq = seq // tq
    per = max(1, nq // n_classes)
    q3 = qr.reshape(bsz, seq, ATT_WIDTH)
    iq3 = iqr.reshape(bsz, seq, LANE)
    k3, ik3 = (a.reshape(bsz, seq, LANE) for a in (kr, ik4))
    outs = []
    for c in range(nq // per):
        tk = (c + 1) * per * tq
        qb0 = c * per
        r = np.arange(math.gcd(PREFIX_ROWS, tk))
        tri = jnp.asarray(r[:, None] >= r[None, :], BF16)
        out = pl.pallas_call(
            functools.partial(_attn_prompt_kernel, qblk0=qb0, tq=tq, topk=topk),
            grid=(bsz, per),
            in_specs=[pl.BlockSpec((1, tq, ATT_WIDTH), lambda b, j, qb0=qb0: (b, qb0 + j, 0)),
                      pl.BlockSpec((1, tq, LANE), lambda b, j, qb0=qb0: (b, qb0 + j, 0)),
                      pl.BlockSpec((8, tq), lambda b, j, qb0=qb0: (0, b * nq + qb0 + j)),
                      pl.BlockSpec((1, tk, LANE), lambda b, j: (b, 0, 0)),
                      pl.BlockSpec((1, LANE, tk), lambda b, j: (b, 0, 0)),
                      pl.BlockSpec((1, tk, LANE), lambda b, j: (b, 0, 0)),
                      pl.BlockSpec(tri.shape, lambda b, j: (0, 0))],
            out_specs=pl.BlockSpec((1, tq, ATT_WIDTH), lambda b, j: (b, j, 0)),
            out_shape=jax.ShapeDtypeStruct((bsz, per * tq, ATT_WIDTH), F32),
            scratch_shapes=[pltpu.VMEM((tk, tq), F32)],
            compiler_params=_cp(("parallel", "arbitrary")),
            name=f"attn_prompt_{tk}",
        )(q3, iq3, iwt, k3, vt, ik3, tri)
        outs.append(out)
    return jnp.concatenate(outs, axis=1).reshape(bsz * seq, ATT_WIDTH)


def _attn_sample_kernel(pt_ref, q_ref, iq_ref, misc_ref, kn_ref, vn_ref, ik4n_ref, *rest, npg, topk):
    del pt_ref
    kp, vp, ikp = rest[0:npg], rest[npg:2 * npg], rest[2 * npg:3 * npg]
    o_ref, key_sc, p_sc = rest[3 * npg:]
    tq = q_ref.shape[1]
    past = npg * PAGE_SIZE
    lk = past + LANE
    k_tile = lambda j: kp[j][0] if j < npg else _pad_rows(kn_ref[0], LANE).T
    v_tile = lambda j: vp[j][0] if j < npg else _pad_rows(vn_ref[0], LANE).T
    ik_tile = lambda j: ikp[j][0] if j < npg else _pad_rows(ik4n_ref[0], LANE).T[0:IDX_DIM, :]
    tile = lambda j: slice(j * LANE, (j + 1) * LANE)
    iq = iq_ref[0]
    iqs = _bf(jnp.concatenate([iq[:, h * IDX_DIM:(h + 1) * IDX_DIM] for h in range(IDX_HEADS)], axis=0))
    misc = misc_ref[0]
    iw = jnp.concatenate([misc[:, IDX_DIM + h:IDX_DIM + h + 1] for h in range(IDX_HEADS)], axis=0)
    iw = iw * (IDX_HEADS ** -0.5 * IDX_DIM ** -0.5)
    for j in range(npg + 1):
        wl = jnp.maximum(_dot(iqs, ik_tile(j)), 0.0) * iw
        score = (wl[0:tq] + wl[tq:2 * tq]) + (wl[2 * tq:3 * tq] + wl[3 * tq:4 * tq])
        if j == npg:
            score = jnp.where(_lane_iota((tq, LANE)) <= _row_iota((tq, LANE)), score, -FAR)
        key_sc[:, tile(j)] = score
    sel = _topk_select(key_sc, _lane_iota((tq, lk)), 1, (lk - 1).bit_length(), p_sc, topk)
    bias = jnp.where(sel > 0, 0.0, NEG)
    heads = ATT_HEADS // KV_HEADS
    q = q_ref[0] * HEAD_DIM ** -0.5
    qg = _bf(jnp.concatenate([_group_queries(q, g) for g in range(KV_HEADS)], axis=0))
    s = jnp.concatenate([_dot(qg, k_tile(j)) for j in range(npg + 1)], axis=1)
    s = s + jnp.concatenate([bias] * ATT_HEADS, axis=0)
    pr = jnp.exp(s - jnp.max(s, axis=-1, keepdims=True))
    o = None
    for j in range(npg + 1):
        part = _dot_nt(pr[:, tile(j)], v_tile(j))
        o = part if o is None else o + part
    o = o / jnp.sum(pr, axis=-1, keepdims=True)
    o_ref[0] = _ungroup_outputs([o[g * heads * tq:(g + 1) * heads * tq] for g in range(KV_HEADS)], tq)


def attn_sample(qr, kr, v, iqr, ik4, misc, cache_kt, cache_vt, cache_ikt, page_table, layer, dseq):
    db, npg = page_table.shape
    n_pool = cache_kt.shape[0] // DEPTH
    lk = npg * PAGE_SIZE + LANE
    topk = min(TOPK_MAX, (npg * PAGE_SIZE + dseq) // 4)
    base = layer * n_pool
    r3 = lambda a: a.reshape(db, dseq, a.shape[-1])
    row_spec = lambda w: pl.BlockSpec((1, dseq, w), lambda b, pt: (b, 0, 0))

    def page_spec(w, j):
        return pl.BlockSpec((1, w, PAGE_SIZE), lambda b, pt, j=j: (pt[b, j] + base, 0, 0))

    in_specs = [row_spec(ATT_WIDTH), row_spec(LANE), row_spec(LANE),
                row_spec(LANE), row_spec(LANE), row_spec(LANE)]
    in_specs += [page_spec(KVW, j) for j in range(npg)]
    in_specs += [page_spec(KVW, j) for j in range(npg)]
    in_specs += [page_spec(IDX_DIM, j) for j in range(npg)]
    out = pl.pallas_call(
        functools.partial(_attn_sample_kernel, npg=npg, topk=topk),
        grid_spec=pltpu.PrefetchScalarGridSpec(
            num_scalar_prefetch=1,
            grid=(db,),
            in_specs=in_specs,
            out_specs=pl.BlockSpec((1, dseq, ATT_WIDTH), lambda b, pt: (b, 0, 0)),
            scratch_shapes=[pltpu.VMEM((dseq, lk), F32), pltpu.VMEM((dseq, 1), jnp.int32)]),
        out_shape=jax.ShapeDtypeStruct((db, dseq, ATT_WIDTH), F32),
        compiler_params=_cp(("parallel",)),
        name="attn_sample",
    )(page_table, r3(qr), r3(iqr), r3(misc), r3(kr), r3(v), r3(ik4),
      *([cache_kt] * npg), *([cache_vt] * npg), *([cache_ikt] * npg))
    return out.reshape(db * dseq, ATT_WIDTH)


def _s5_param_kernel(ldt_ref, are_ref, aim_ref, bre_ref, bim_ref, abar_ref, win_ref):
    dt = jnp.exp(ldt_ref[...])
    a_re, a_im = are_ref[...], aim_ref[...]
    mag = jnp.exp(dt * a_re)
    abar_re = mag * jnp.cos(dt * a_im)
    abar_im = mag * jnp.sin(dt * a_im)
    den = a_re * a_re + a_im * a_im
    num_re = abar_re - 1.0
    coef_re = (num_re * a_re + abar_im * a_im) / den
    coef_im = (abar_im * a_re - num_re * a_im) / den
    abar_ref[:, 0:S5_NS] = jnp.broadcast_to(abar_re, (8, S5_NS))
    abar_ref[:, S5_NS:2 * S5_NS] = jnp.broadcast_to(abar_im, (8, S5_NS))
    b_re, b_im = bre_ref[...], bim_ref[...]
    win_ref[:, 0:S5_NS] = _bf(coef_re * b_re - coef_im * b_im)
    win_ref[:, S5_NS:2 * S5_NS] = _bf(coef_re * b_im + coef_im * b_re)


def _block_diag_in(b):
    eye = jnp.eye(S5_GROUPS, dtype=b.dtype)
    return jnp.einsum('gnc,gh->gchn', b, eye).reshape(S5_WIDTH, S5_NS)


def _block_diag_out(c):
    eye = jnp.eye(S5_GROUPS, dtype=c.dtype)
    return jnp.einsum('gcn,gh->gnhc', c, eye).reshape(S5_NS, S5_WIDTH)


def s5_params(log_dt, a_re, a_im, b_re, b_im):
    per_state = lambda a: a.reshape(1, S5_NS)
    ldt = per_state(jnp.broadcast_to(log_dt[:, None], (S5_GROUPS, S5_STATE)))
    return pl.pallas_call(
        _s5_param_kernel,
        out_shape=[jax.ShapeDtypeStruct((8, 2 * S5_NS), F32),
                   jax.ShapeDtypeStruct((S5_WIDTH, 2 * S5_NS), BF16)],
        compiler_params=pltpu.CompilerParams(vmem_limit_bytes=VMEM_LIMIT),
        name="s5_params",
    )(ldt, per_state(a_re), per_state(a_im), _block_diag_in(b_re), _block_diag_in(b_im))


def _gelu_tanh(x):
    return 0.5 * x * (1.0 + jnp.tanh(math.sqrt(2.0 / math.pi) * (x + 0.044715 * (x * x * x))))


S5_LANES = 512


def _s5_kernel(u_ref, abar_ref, win_ref, h0_ref, wout_ref, d_ref, wglu_ref, bglu_ref,
               y_ref, hn_ref, s_sc, *, bsz, tc):
    c = pl.program_id(0)

    @pl.when(c == 0)
    def _():
        hn_ref[...] = h0_ref[...]

    u = u_ref[...]
    s_sc[...] = jnp.dot(_bf(u), win_ref[...], preferred_element_type=F32)
    nchunk = S5_NS // S5_LANES
    for rg in range(bsz // 8):
        rows = slice(rg * 8, rg * 8 + 8)

        def body(t, carry):
            row0 = pl.multiple_of(t * bsz + rg * 8, 8)
            new = []
            for cc in range(nchunk):
                lre = slice(cc * S5_LANES, (cc + 1) * S5_LANES)
                lim = slice(S5_NS + cc * S5_LANES, S5_NS + (cc + 1) * S5_LANES)
                xr, xi = carry[2 * cc], carry[2 * cc + 1]
                ar, ai = abar_ref[:, lre], abar_ref[:, lim]
                nr = (ar * xr - ai * xi) + s_sc[pl.ds(row0, 8), lre]
                ni = (ar * xi + ai * xr) + s_sc[pl.ds(row0, 8), lim]
                s_sc[pl.ds(row0, 8), lre] = nr
                s_sc[pl.ds(row0, 8), lim] = ni
                new += [nr, ni]
            return tuple(new)

        init = []
        for cc in range(nchunk):
            init += [hn_ref[rows, cc * S5_LANES:(cc + 1) * S5_LANES],
                     hn_ref[rows, S5_NS + cc * S5_LANES:S5_NS + (cc + 1) * S5_LANES]]
        fin = lax.fori_loop(0, tc, body, tuple(init))
        for cc in range(nchunk):
            hn_ref[rows, cc * S5_LANES:(cc + 1) * S5_LANES] = fin[2 * cc]
            hn_ref[rows, S5_NS + cc * S5_LANES:S5_NS + (cc + 1) * S5_LANES] = fin[2 * cc + 1]

    y = (jnp.dot(_bf(s_sc[:, 0:S5_NS]), wout_ref[0:S5_NS, :], preferred_element_type=F32)
         - jnp.dot(_bf(s_sc[:, S5_NS:2 * S5_NS]), wout_ref[S5_NS:2 * S5_NS, :], preferred_element_type=F32)
         + d_ref[...] * u)
    z = _gelu_tanh(y)
    y_ref[...] = z * _sigmoid(jnp.dot(_bf(z), wglu_ref[...], preferred_element_type=F32) + bglu_ref[...])


def s5_branch(u_tm, h0, abar8, win, wout_bf, d, wglu_bf, bglu, bsz, seq, tc):
    rows = tc * bsz
    const = lambda shape: pl.BlockSpec(shape, lambda c: (0,) * len(shape))
    return pl.pallas_call(
        functools.partial(_s5_kernel, bsz=bsz, tc=tc),
        grid=(seq // tc,),
        in_specs=[pl.BlockSpec((rows, S5_WIDTH), lambda c: (c, 0)),
                  const((8, 2 * S5_NS)), const((S5_WIDTH, 2 * S5_NS)), const((bsz, 2 * S5_NS)),
                  const((2 * S5_NS, S5_WIDTH)), const((1, S5_WIDTH)), const((S5_WIDTH, S5_WIDTH)),
                  const((1, S5_WIDTH))],
        out_specs=[pl.BlockSpec((rows, S5_WIDTH), lambda c: (c, 0)), const((bsz, 2 * S5_NS))],
        out_shape=[jax.ShapeDtypeStruct((seq * bsz, S5_WIDTH), F32),
                   jax.ShapeDtypeStruct((bsz, 2 * S5_NS), F32)],
        scratch_shapes=[pltpu.VMEM((rows, 2 * S5_NS), F32)],
        compiler_params=_cp(("arbitrary",)),
        name="s5_branch",
    )(u_tm, abar8, win, h0, wout_bf, d.reshape(1, S5_WIDTH), wglu_bf, bglu.reshape(1, S5_WIDTH))


def _pad_rows(x, rows):
    n = x.shape[0]
    return x if n == rows else jnp.concatenate([x, jnp.zeros((rows - n, x.shape[1]), x.dtype)], axis=0)


def _head_mean(x, amat_bf):
    return _dot2(x, amat_bf)


def _split3(x):
    a = x.astype(BF16)
    r = x - a.astype(F32)
    b = r.astype(BF16)
    return a, b, (r - b.astype(F32)).astype(BF16)


def _expand_state(tall, rep_bf, bdm):
    d = functools.partial(jnp.dot, preferred_element_type=F32)
    a, b, c = _split3(tall)
    return (d(a, rep_bf) + (d(b, rep_bf) + d(c, rep_bf))) * bdm


def _collapse_state(bd, rept_bf):
    d = functools.partial(jnp.dot, preferred_element_type=F32)
    a, b, c = _split3(bd)
    return d(a, rept_bf) + (d(b, rept_bf) + d(c, rept_bf))


def _rep_consts():
    rep = np.tile(np.eye(64, dtype=np.float32), (1, BW // 64))
    return jnp.asarray(rep, BF16), jnp.asarray(rep.T, BF16)


def _ret_kernel(q_ref, k_ref, v_ref, g_ref, tab_ref, s0_ref, dmat_ref, qdec_ref, kdec_ref, decm_ref,
                bdm_ref, amat_ref, rep_ref, rept_ref, y_ref, so_ref, s_ref, *, rows):
    n = q_ref.shape[0]

    @pl.when(pl.program_id(1) == 0)
    def _():
        s_ref[...] = _expand_state(s0_ref[0], rep_ref[...], bdm_ref[...])

    q = _pad_rows(_rope(q_ref[...], tab_ref, RET_DK // 2), rows)
    k = _pad_rows(_rope(k_ref[...], tab_ref, RET_DK // 2) * RET_DK ** -0.5, rows)
    v = _pad_rows(v_ref[...], rows)
    state = s_ref[...]
    inter = _dot(q * qdec_ref[...], state)
    lane = _lane_iota((rows, LANE))
    tiles = []
    for p in range(RET_HEADS // 2):
        lanes = slice(p * LANE, (p + 1) * LANE)
        qp, kp, vp = q[:, lanes], _bf(k[:, lanes]), v[:, lanes]
        acc = None
        for hh in range(2):
            mine = (lane < RET_DK) if hh == 0 else (lane >= RET_DK)
            s = _dot_nt(jnp.where(mine, qp, 0.0), kp) * dmat_ref[2 * p + hh]
            part = _dot(s, jnp.where(mine, vp, 0.0))
            acc = part if acc is None else acc + part
        tiles.append(acc)
    o = jnp.concatenate(tiles, axis=1) + inter
    new_state = state * decm_ref[...] + _dot_tn(k * kdec_ref[...], v) * bdm_ref[...]
    s_ref[...] = new_state

    @pl.when(pl.program_id(1) == pl.num_programs(1) - 1)
    def _():
        so_ref[0] = _collapse_state(new_state, rept_ref[...])

    amat = amat_ref[...]
    mu = _head_mean(o, amat)
    d = o - mu
    var = _head_mean(d * d, amat)
    on = d * lax.rsqrt(var + 1e-5)
    y_ref[...] = (_silu(g_ref[...]) * on[0:n]).astype(y_ref.dtype)


def _head_block_mask():
    h = np.arange(BW) // 64
    return (h[:, None] == h[None, :]).astype(np.float32)


def _ret_consts(rows, n_true):
    lg = np.log(1.0 - np.exp2(-5.0 - np.arange(RET_HEADS, dtype=np.float64)))
    i = np.arange(rows, dtype=np.float64)
    rel = i[:, None] - i[None, :]
    dmat = np.where(rel[None] >= 0, np.exp(np.minimum(rel[None], rows) * lg[:, None, None]), 0.0)
    lane_lg = np.repeat(lg, 64)[None, :]
    qdec = np.exp((i[:, None] + 1.0) * lane_lg)
    kdec = np.where(i[:, None] < n_true, np.exp((n_true - 1.0 - i[:, None]) * lane_lg), 0.0)
    bdm = _head_block_mask()
    decm = bdm * np.exp(n_true * np.repeat(lg, 64))[:, None]
    f = lambda a: jnp.asarray(a, F32)
    return f(dmat), f(qdec), f(kdec), f(decm), f(bdm), jnp.asarray(bdm / 64.0, BF16)


def ret_branch(proj, tab_r, s0, bsz, seq, rows, n):
    nch = seq // n
    dmat, qdec, kdec, decm, bdm, amat = _ret_consts(rows, n)
    rep, rept = _rep_consts()
    col = lambda j: pl.BlockSpec((n, BW), lambda b, c, j=j: (b * nch + c, j))
    const = lambda shape: pl.BlockSpec(shape, lambda b, c: (0,) * len(shape))
    return pl.pallas_call(
        functools.partial(_ret_kernel, rows=rows),
        grid=(bsz, nch),
        in_specs=[col(0), col(1), col(2), col(3),
                  pl.BlockSpec((3, n, LANE), lambda b, c: (0, c, 0)),
                  pl.BlockSpec((1, BW, RET_DK), lambda b, c: (b, 0, 0)),
                  const((RET_HEADS, rows, rows)), const((rows, BW)), const((rows, BW)),
                  const((BW, BW)), const((BW, BW)), const((BW, BW)), const((RET_DK, BW)), const((BW, RET_DK))],
        out_specs=[pl.BlockSpec((n, BW), lambda b, c: (b * nch + c, 0)),
                   pl.BlockSpec((1, BW, RET_DK), lambda b, c: (b, 0, 0))],
        out_shape=[jax.ShapeDtypeStruct((bsz * seq, BW), F32),
                   jax.ShapeDtypeStruct((bsz, BW, RET_DK), F32)],
        scratch_shapes=[pltpu.VMEM((BW, BW), F32)],
        compiler_params=_cp(("parallel", "arbitrary")),
        name="ret_branch",
    )(proj, proj, proj, proj, tab_r, s0, dmat, qdec, kdec, decm, bdm, amat, rep, rept)


def _softplus(x):
    return jnp.maximum(x, 0.0) + jnp.log(1.0 + jnp.exp(-jnp.abs(x)))


GDN_SUPER = ((0, 4 * GDN_CHUNK), (4 * GDN_CHUNK, 6 * GDN_CHUNK))
GM_BLOCK, GM_INCL, GM_STRICT, GM_EYE, GM_LEVEL0 = 0, 1, 2, 3, 4


def _stack_heads(a, bdm_rows):
    return jnp.concatenate([a] * (bdm_rows.shape[0] // GDN_CHUNK), axis=0) * bdm_rows


def _unstack_heads(parts):
    blocks = [p[i:i + GDN_CHUNK] for p in parts for i in range(0, p.shape[0], GDN_CHUNK)]
    out = blocks[0]
    for b in blocks[1:]:
        out = out + b
    return out


def _col_of_heads(a, s, e):
    return jnp.concatenate([jnp.broadcast_to(a[:, h * GDN_DK:h * GDN_DK + 1], (GDN_CHUNK, e - s))
                            for h in range(s // GDN_CHUNK, e // GDN_CHUNK)], axis=0)


def _row_of_heads(a_t, s, e):
    return jnp.concatenate([a_t[h * GDN_DK:h * GDN_DK + 1, :] for h in range(s // GDN_CHUNK, e // GDN_CHUNK)],
                           axis=1)


def _gdn_prepare(chunks, gm_ref):
    items = [(ci, s, e) for ci in range(len(chunks)) for s, e in GDN_SUPER]
    g_ts = [gc.T for _, _, _, _, gc in chunks]
    nmats, decs, qks = [], [], []
    for ci, s, e in items:
        qc, kc, _, bc, gc = chunks[ci]
        bdm_rows = gm_ref[GM_BLOCK, s:e, :]
        ks = _bf(_stack_heads(kc, bdm_rows))
        kk = _dot_nt(ks, ks)
        qks.append(_dot_nt(_stack_heads(qc, bdm_rows), ks))
        diff = _col_of_heads(gc, s, e) - _row_of_heads(g_ts[ci], s, e)
        dec = jnp.exp(jnp.where(gm_ref[GM_INCL, s:e, s:e] > 0.5, diff, NEG))
        decs.append(dec)
        nmats.append(_col_of_heads(bc, s, e) * (dec * gm_ref[GM_STRICT, s:e, s:e]) * kk)
    invs = [gm_ref[GM_EYE, s:e, s:e] - nm * gm_ref[GM_LEVEL0, s:e, s:e] for nm, (_, s, e) in zip(nmats, items)]
    for lvl in range(1, (GDN_CHUNK - 1).bit_length()):
        right = [_dot(nm * gm_ref[GM_LEVEL0 + lvl, s:e, s:e], inv) for nm, inv, (_, s, e) in zip(nmats, invs, items)]
        invs = [inv - _dot(inv, r) for inv, r in zip(invs, right)]
    w_st, uv_st = [[] for _ in chunks], [[] for _ in chunks]
    for inv, (ci, s, e) in zip(invs, items):
        _, kc, vc, bc, gc = chunks[ci]
        bdm_rows = gm_ref[GM_BLOCK, s:e, :]
        wu = _dot(inv, jnp.concatenate([_stack_heads(bc * jnp.exp(gc) * kc, bdm_rows),
                                        _stack_heads(bc * vc, bdm_rows)], axis=1))
        w_st[ci].append(wu[:, 0:BW])
        uv_st[ci].append(wu[:, BW:2 * BW])
    n_sb = len(GDN_SUPER)
    return [(_unstack_heads(w_st[ci]), _unstack_heads(uv_st[ci]),
             [qks[ci * n_sb + j] * decs[ci * n_sb + j] for j in range(n_sb)]) for ci in range(len(chunks))]


def _gdn_apply(chunk, prepared, hbd, gm_ref):
    qc, kc, _, _, gc = chunk
    w, uv, a_mats = prepared
    g_last = gc[GDN_CHUNK - 1:GDN_CHUNK, :]
    u = uv - _dot(w, hbd)
    o_st = [_dot(a_mats[i], _stack_heads(u, gm_ref[GM_BLOCK, s:e, :])) for i, (s, e) in enumerate(GDN_SUPER)]
    o = jnp.exp(gc) * _dot(qc, hbd) + _unstack_heads(o_st)
    h_new = jnp.exp(g_last) * hbd + _dot_tn(kc * jnp.exp(g_last - gc), u) * gm_ref[GM_BLOCK]
    return o, h_new


def _gdn_kernel(q_ref, k_ref, v_ref, g_ref, misc_ref, cs0_ref, cw_ref, alog_ref, dtb_ref, ng_ref, h0_ref,
                ea_ref, eb_ref, gm_ref, tri_ref, amat_ref, rep_ref, rept_ref, y_ref, ho_ref, cs_ref,
                xp_sc, h_sc, *, rows):
    n = q_ref.shape[0]
    cw = 3 * BW

    @pl.when(pl.program_id(1) == 0)
    def _():
        xp_sc[...] = jnp.zeros(xp_sc.shape, F32)
        xp_sc[5:8, :] = cs0_ref[0]
        h_sc[...] = _expand_state(h0_ref[0], rep_ref[...], gm_ref[GM_BLOCK])

    for j, r in enumerate((q_ref, k_ref, v_ref)):
        xp_sc[8:8 + n, j * BW:(j + 1) * BW] = r[...]
    conv = xp_sc[5:5 + rows, :] * cw_ref[0:1, :]
    for i in range(1, GDN_CONV):
        conv = conv + xp_sc[5 + i:5 + i + rows, :] * cw_ref[i:i + 1, :]
    tail = xp_sc[8 + n - 3:8 + n, :]
    xp_sc[5:8, :] = tail
    cs_ref[0] = tail
    xc = _silu(conv)
    valid = _row_iota((rows, BW)) < n
    bdm_bf = _bf(gm_ref[GM_BLOCK])
    q, k, v = xc[:, 0:BW], xc[:, BW:2 * BW], xc[:, 2 * BW:cw]
    q = q * lax.rsqrt(_dot2(q * q, bdm_bf) + EPS) * GDN_DK ** -0.5
    k = k * lax.rsqrt(_dot2(k * k, bdm_bf) + EPS)
    misc = _pad_rows(misc_ref[...], rows)
    beta = _sigmoid(_dot2(misc, eb_ref[...]))
    la = -jnp.exp(alog_ref[...]) * _softplus(_dot2(misc, ea_ref[...]) + dtb_ref[...])
    k = jnp.where(valid, k, 0.0)
    v = jnp.where(valid, v, 0.0)
    la = jnp.where(valid, la, 0.0)
    la_hi, la_lo = _split(la)
    tri = tri_ref[...]
    gall = (jnp.dot(tri, la_hi, preferred_element_type=F32)
            + jnp.dot(tri, la_lo, preferred_element_type=F32))
    c = GDN_CHUNK
    chunks = [(q[r], k[r], v[r], beta[r], gall[r]) for r in (slice(i, i + c) for i in range(0, rows, c))]
    prepared = _gdn_prepare(chunks, gm_ref)
    outs = []
    hbd = h_sc[...]
    for chunk, prep in zip(chunks, prepared):
        o, hbd = _gdn_apply(chunk, prep, hbd, gm_ref)
        outs.append(o)
    h_sc[...] = hbd

    @pl.when(pl.program_id(1) == pl.num_programs(1) - 1)
    def _():
        ho_ref[0] = _collapse_state(hbd, rept_ref[...])

    o = outs[0] if len(outs) == 1 else jnp.concatenate(outs, axis=0)
    on = o * lax.rsqrt(_dot2(o * o, amat_ref[...]) + EPS) * ng_ref[...]
    y_ref[...] = on[0:n] * _silu(g_ref[...])


def _gdn_consts(rows):
    lanes = np.arange(BW) // 64
    ea = np.zeros((LANE, BW), np.float32)
    eb = np.zeros((LANE, BW), np.float32)
    ea[MISC_DA + lanes, np.arange(BW)] = 1.0
    eb[MISC_DB + lanes, np.arange(BW)] = 1.0
    i = np.arange(rows)
    tri = ((i[:, None] // GDN_CHUNK == i[None, :] // GDN_CHUNK) & (i[:, None] >= i[None, :])).astype(np.float32)
    bdm = _head_block_mask()
    r = np.arange(BW)
    ri, ci = r[:, None] % GDN_CHUNK, r[None, :] % GDN_CHUNK
    gm = [bdm, bdm * (ri >= ci), bdm * (ri > ci), np.eye(BW, dtype=np.float32)]
    s = 1
    while s < GDN_CHUNK:
        gm.append(bdm * ((ri // (2 * s)) == (ci // (2 * s))) * ((ri // s) % 2 == 1) * ((ci // s) % 2 == 0))
        s *= 2
    return (jnp.asarray(ea, BF16), jnp.asarray(eb, BF16), jnp.asarray(np.stack(gm), F32), jnp.asarray(tri, BF16),
            jnp.asarray(bdm / 64.0, BF16))


def gdn_branch(proj, misc, cs0, conv_w, a_log, dt_bias, norm_g, h0, bsz, seq, rows, n):
    nblk = seq // n
    ea, eb, gm, tri, amat = _gdn_consts(rows)
    rep, rept = _rep_consts()
    per_lane = lambda a, reps: jnp.repeat(a, reps).reshape(1, BW) if reps > 1 else jnp.tile(a, BW // a.shape[0]).reshape(1, BW)
    col = lambda j: pl.BlockSpec((n, BW), lambda b, c, j=j: (b * nblk + c, j))
    const = lambda shape: pl.BlockSpec(shape, lambda b, c: (0,) * len(shape))
    per_b = lambda shape: pl.BlockSpec(shape, lambda b, c: (b,) + (0,) * (len(shape) - 1))
    cw = 3 * BW
    return pl.pallas_call(
        functools.partial(_gdn_kernel, rows=rows),
        grid=(bsz, nblk),
        in_specs=[col(4), col(5), col(6), col(7),
                  pl.BlockSpec((n, LANE), lambda b, c: (b * nblk + c, 0)),
                  per_b((1, GDN_CONV - 1, cw)), const((GDN_CONV, cw)),
                  const((1, BW)), const((1, BW)), const((1, BW)), per_b((1, BW, GDN_DK)),
                  const((LANE, BW)), const((LANE, BW)), const(tuple(gm.shape)), const((rows, rows)), const((BW, BW)),
                  const((GDN_DK, BW)), const((BW, GDN_DK))],
        out_specs=[pl.BlockSpec((n, BW), lambda b, c: (b * nblk + c, 0)),
                   per_b((1, BW, GDN_DK)), per_b((1, GDN_CONV - 1, cw))],
        out_shape=[jax.ShapeDtypeStruct((bsz * seq, BW), F32),
                   jax.ShapeDtypeStruct((bsz, BW, GDN_DK), F32),
                   jax.ShapeDtypeStruct((bsz, GDN_CONV - 1, cw), F32)],
        scratch_shapes=[pltpu.VMEM((rows + 8, cw), F32), pltpu.VMEM((BW, BW), F32)],
        compiler_params=_cp(("parallel", "arbitrary")),
        name="gdn_branch",
    )(proj, proj, proj, proj, misc, cs0, conv_w, per_lane(a_log, 64), per_lane(dt_bias, 64),
      per_lane(norm_g, 1), h0, ea, eb, gm, tri, amat, rep, rept)


_BRANCH_OFFS = (0, ATT_WIDTH, ATT_WIDTH + BW, ATT_WIDTH + 2 * BW, MIX_WIDTH)


def _merge_kernel(x_ref, g_ref, sc_ref, sh_ref, gm_ref, ya_ref, yb_ref, yc_ref, yd_ref,
                  wg_ref, wb_ref, wo_ref, o_ref):
    x = x_ref[...]
    h = _bf(_norm_mod(x, g_ref[...], sc_ref[0], sh_ref[0]))
    merged = None
    for b, y_ref in enumerate((ya_ref, yb_ref, yc_ref, yd_ref)):
        gate = _sigmoid(jnp.dot(h, wg_ref[:, b * D_MODEL:(b + 1) * D_MODEL], preferred_element_type=F32))
        term = gate * jnp.dot(_bf(y_ref[...]), wb_ref[_BRANCH_OFFS[b]:_BRANCH_OFFS[b + 1], :],
                              preferred_element_type=F32)
        merged = term if merged is None else merged + term
    y = jnp.dot(_bf(merged), wo_ref[...], preferred_element_type=F32)
    o_ref[...] = x + gm_ref[0] * y


def merge_out(x, g, sc, sh, gm, ya, yb, yc, yd, wg_bf, wb_bf, wo_bf, tm, tiles_per_seq):
    m = x.shape[0]
    ms = _mod_spec(tm, sc.shape[1], tiles_per_seq)
    row = lambda w: pl.BlockSpec((tm, w), lambda i: (i, 0))
    const = lambda shape: pl.BlockSpec(shape, lambda i: (0,) * len(shape))
    return pl.pallas_call(
        _merge_kernel,
        grid=(m // tm,),
        in_specs=[row(D_MODEL), const((1, D_MODEL)), ms, ms, ms, row(ATT_WIDTH), row(BW), row(BW), row(BW),
                  const((D_MODEL, N_BRANCH * D_MODEL)), const((MIX_WIDTH, D_MODEL)), const((D_MODEL, D_MODEL))],
        out_specs=row(D_MODEL),
        out_shape=jax.ShapeDtypeStruct((m, D_MODEL), F32),
        compiler_params=_cp(("parallel",)),
        name="merge_out",
    )(x, g.reshape(1, D_MODEL), sc, sh, gm, ya, yb, yc, yd, wg_bf, wb_bf, wo_bf)


def _top2(masked, lane):
    m1 = jnp.max(masked, axis=-1, keepdims=True)
    i1 = jnp.min(jnp.where(masked == m1, lane, LANE), axis=-1, keepdims=True)
    rest = jnp.where(lane == i1, -jnp.inf, masked)
    m2 = jnp.max(rest, axis=-1, keepdims=True)
    i2 = jnp.min(jnp.where(rest == m2, lane, LANE), axis=-1, keepdims=True)
    return m1, i1, m2, i2


def _route(scores, biased):
    lane = _lane_iota(scores.shape)
    grp = lane // EXPERTS_PER_GROUP
    best_val, best_grp = None, None
    for g in range(N_GROUPS):
        m1, _, m2, _ = _top2(jnp.where(grp == g, biased, -jnp.inf), lane)
        gs = m1 + m2
        if g == 0:
            best_val, best_grp = gs, jnp.zeros(gs.shape, jnp.int32)
        else:
            better = gs > best_val
            best_val = jnp.where(better, gs, best_val)
            best_grp = jnp.where(better, g, best_grp)
    _, e1, _, e2 = _top2(jnp.where(grp == best_grp, biased, -jnp.inf), lane)
    s1 = jnp.sum(jnp.where(lane == e1, scores, 0.0), axis=-1, keepdims=True)
    s2 = jnp.sum(jnp.where(lane == e2, scores, 0.0), axis=-1, keepdims=True)
    tot = s1 + s2
    return jnp.where(lane == e1, s1 / tot, 0.0) + jnp.where(lane == e2, s2 / tot, 0.0), best_grp


MOE_BLOCK = 128
MOE_ALIGN = 16


def _moe_kernel(x_ref, g_ref, sc_ref, sh_ref, gm_ref, wr_ref, rb_ref, tri_ref, upper_ref, w1_ref, w3_ref, w2_ref,
                fg_ref, o_ref, hs_sc, comb_sc, acc_sc, pt_sc, seg_sc, *, final):
    grp_id = pl.program_id(1)
    tm = x_ref.shape[0]

    @pl.when(grp_id == 0)
    def _():
        h = _norm_mod(x_ref[...], g_ref[...], sc_ref[0], sh_ref[0])
        scores = _sigmoid(_dot3(h, wr_ref[...]))
        comb, best = _route(scores, scores + rb_ref[...])
        lane = _lane_iota((tm, LANE))
        onehot = jnp.where(lane == best, 1.0, 0.0)
        incl = jnp.dot(tri_ref[...], _bf(onehot), preferred_element_type=F32)
        counts = incl[tm - 8:tm, :]
        offs = _dot2(counts, upper_ref[...])
        rank = jnp.sum(onehot * (offs[7:8, :] + incl), axis=-1, keepdims=True) - 1.0
        perm_t = jnp.where(_lane_iota((tm, tm)).astype(F32) == rank, 1.0, 0.0).astype(BF16)
        pt_sc[...] = perm_t
        hs_sc[0:tm, :] = _dot_tn(perm_t, _bf(h)).astype(BF16)
        hs_sc[tm:tm + MOE_BLOCK, :] = jnp.zeros((MOE_BLOCK, D_MODEL), BF16)
        ca, cb, cc = _split3(comb)
        tn = functools.partial(lax.dot_general, dimension_numbers=(((0,), (0,)), ((), ())),
                               preferred_element_type=F32)
        comb_sc[0:tm, :] = tn(perm_t, ca) + (tn(perm_t, cb) + tn(perm_t, cc))
        comb_sc[tm:tm + MOE_BLOCK, :] = jnp.zeros((MOE_BLOCK, LANE), F32)
        acc_sc[...] = jnp.zeros(acc_sc.shape, F32)
        for gi in range(N_GROUPS):
            seg_sc[gi] = offs[7, gi].astype(jnp.int32)
            seg_sc[N_GROUPS + gi] = counts[7, gi].astype(jnp.int32)

    off = seg_sc[grp_id]
    cnt = seg_sc[N_GROUPS + grp_id]
    start = (off // MOE_ALIGN) * MOE_ALIGN
    nblk = jnp.where(cnt > 0, (off + cnt - start + MOE_BLOCK - 1) // MOE_BLOCK, 0)

    def block(i, carry):
        r0 = pl.multiple_of(start + i * MOE_BLOCK, MOE_ALIGN)
        hb = hs_sc[pl.ds(r0, MOE_BLOCK), :]
        cblk = comb_sc[pl.ds(r0, MOE_BLOCK), :]
        lane = _lane_iota(cblk.shape)
        out = None
        for e in range(EXPERTS_PER_GROUP):
            ce = jnp.sum(jnp.where(lane == grp_id * EXPERTS_PER_GROUP + e, cblk, 0.0), axis=-1, keepdims=True)
            hid = (_silu(jnp.dot(hb, w1_ref[0, e], preferred_element_type=F32))
                   * jnp.dot(hb, w3_ref[0, e], preferred_element_type=F32))
            part = jnp.dot(_bf(hid * ce), w2_ref[0, e], preferred_element_type=F32)
            out = part if out is None else out + part
        acc_sc[pl.ds(r0, MOE_BLOCK), :] += out
        return carry

    lax.fori_loop(0, nblk, block, 0)

    @pl.when(grp_id == pl.num_programs(1) - 1)
    def _():
        d = functools.partial(jnp.dot, preferred_element_type=F32)
        a, b, c = _split3(acc_sc[0:tm, :])
        perm_t = pt_sc[...]
        out = x_ref[...] + gm_ref[0] * (d(perm_t, a) + (d(perm_t, b) + d(perm_t, c)))
        if final:
            out = out * lax.rsqrt(jnp.mean(out * out, axis=-1, keepdims=True) + EPS) * fg_ref[...]
        o_ref[...] = out


def moe_out(x, g, sc, sh, gm, wr_pad, rb_pad, w1_bf, w3_bf, w2_bf, final_g, final, tm, tiles_per_seq):
    m = x.shape[0]
    ms = _mod_spec(tm, sc.shape[1], tiles_per_seq)
    const = lambda shape: pl.BlockSpec(shape, lambda i, e: (0,) * len(shape))
    r = np.arange(tm)
    tri = jnp.asarray(r[:, None] >= r[None, :], BF16)
    u = np.arange(LANE)
    upper = jnp.asarray(u[:, None] < u[None, :], BF16)
    grouped = lambda w: w.reshape((N_GROUPS, EXPERTS_PER_GROUP) + w.shape[1:])
    wspec = lambda a, b: pl.BlockSpec((1, EXPERTS_PER_GROUP, a, b), lambda i, e: (e, 0, 0, 0))
    return pl.pallas_call(
        functools.partial(_moe_kernel, final=final),
        grid=(m // tm, N_GROUPS),
        in_specs=[pl.BlockSpec((tm, D_MODEL), lambda i, e: (i, 0)), const((1, D_MODEL)), ms, ms, ms,
                  const((D_MODEL, LANE)), const((1, LANE)), const((tm, tm)), const((LANE, LANE)),
                  wspec(D_MODEL, EXPERT_FF), wspec(D_MODEL, EXPERT_FF), wspec(EXPERT_FF, D_MODEL),
                  const((1, D_MODEL))],
        out_specs=pl.BlockSpec((tm, D_MODEL), lambda i, e: (i, 0)),
        out_shape=jax.ShapeDtypeStruct((m, D_MODEL), F32),
        scratch_shapes=[pltpu.VMEM((tm + MOE_BLOCK, D_MODEL), BF16), pltpu.VMEM((tm + MOE_BLOCK, LANE), F32),
                        pltpu.VMEM((tm + MOE_BLOCK, D_MODEL), F32), pltpu.VMEM((tm, tm), BF16),
                        pltpu.SMEM((2 * N_GROUPS,), jnp.int32)],
        compiler_params=_cp(("parallel", "arbitrary")),
        name="moe_out",
    )(x, g.reshape(1, D_MODEL), sc, sh, gm, wr_pad, rb_pad, tri, upper, grouped(w1_bf), grouped(w3_bf),
      grouped(w2_bf), final_g.reshape(1, D_MODEL))


_REF_SPLITS = (ATT_WIDTH, KVW, KVW, IDX_HEADS * IDX_DIM, IDX_DIM, IDX_HEADS, BW,
               BW, BW, BW, BW, BW, BW, BW, GDN_HEADS, GDN_HEADS, BW)


def pack_w_in(w_in):
    offs = np.concatenate([[0], np.cumsum(_REF_SPLITS)])
    seg = [w_in[:, int(offs[i]):int(offs[i + 1])] for i in range(len(_REF_SPLITS))]
    (aq, ak, av, aiq, aik, aiw, bu, cq, ck, cv, cg, dq, dk, dv, da, db, dg) = seg
    zeros = lambda n: jnp.zeros((D_MODEL, n), w_in.dtype)
    misc = jnp.concatenate([aik, aiw, da, db, zeros(LANE - IDX_DIM - IDX_HEADS - 2 * GDN_HEADS)], axis=1)
    packed = jnp.concatenate([cq, ck, cv, cg, dq, dk, dv, dg, bu, aq, ak, av, aiq, zeros(2 * LANE)], axis=1)
    wt = jnp.concatenate([aiw.T, jnp.zeros((8 - IDX_HEADS, D_MODEL), w_in.dtype)], axis=0)
    return _bf(packed), misc, wt


def _time_major(a, bsz, seq):
    return a.reshape(bsz, seq, a.shape[-1]).transpose(1, 0, 2).reshape(seq * bsz, a.shape[-1])


def _batch_major(a, bsz, seq):
    return a.reshape(seq, bsz, a.shape[-1]).transpose(1, 0, 2).reshape(bsz * seq, a.shape[-1])


def _trunk_layer(x, mods, geom, attend, st, lw, final_g, final):
    bsz, seq, tm, tps, s5_tc, ret_rows, ret_n, gdn_rows, gdn_n = geom
    sh1, sc1, g1, sh2, sc2, g2 = mods
    proj, misc, iwt = in_proj(x, lw['norm1'], sc1, sh1, lw['w_in'], lw['w_misc'], lw['w_iwt'], tm, tps)
    qr, kr, v, iqr, ikr, ik4, vt = attn_prep(proj, misc, lw['tab_a'], lw['tab_i'], lw['tab_k'], tm, tps)
    ya = attend(qr, kr, v, vt, iqr, ikr, ik4, iwt, misc)
    u_tm = _time_major(proj[:, 8 * BW:9 * BW], bsz, seq)
    y_tm, s5_h = s5_branch(u_tm, st['s5'], lw['s5_abar'], lw['s5_win'], lw['s5_wout'], lw['s5_d'],
                           lw['s5_w_glu'], lw['s5_b_glu'], bsz, seq, s5_tc)
    yb = _batch_major(y_tm, bsz, seq)
    yc, ret_s = ret_branch(proj, lw['tab_r'], st['ret'], bsz, seq, ret_rows, ret_n)
    yd, gdn_s, conv_s = gdn_branch(proj, misc, st['conv'], lw['gdn_conv_w'], lw['gdn_a_log'], lw['gdn_dt_bias'],
                                   lw['gdn_norm_g'], st['gdn'], bsz, seq, gdn_rows, gdn_n)
    x = merge_out(x, lw['norm1'], sc1, sh1, g1, ya, yb, yc, yd, lw['w_gate'], lw['w_br'], lw['w_out'], tm, tps)
    x = moe_out(x, lw['norm2'], sc2, sh2, g2, lw['w_router'], lw['router_bias'], lw['w_e1'], lw['w_e3'],
                lw['w_e2'], final_g, final, tm, tps)
    new_st = {'k': kr, 'v': v, 'ik': ikr, 's5': s5_h, 'ret': ret_s, 'gdn': gdn_s, 'conv': conv_s}
    return x, new_st


def kernel(x_prompt, x_sample, c_prompt, c_sample, cache_k, cache_v, cache_idx_k, page_table,
           state_s5_re, state_s5_im, state_ret, state_gdn, state_gdn_conv,
           norm1_g, norm2_g, final_g, w_ada, b_ada, w_in,
           s5_a_re, s5_a_im, s5_b_re, s5_b_im, s5_c_re, s5_c_im, s5_d, s5_log_dt, s5_w_glu, s5_b_glu,
           gdn_conv_w, gdn_a_log, gdn_dt_bias, gdn_norm_g,
           w_br, w_gate, w_out, w_router, router_bias, w_e1, w_e3, w_e2):
    bsz, seq, _ = x_prompt.shape
    dbs, dseq, _ = x_sample.shape
    depth = w_in.shape[0]
    n_pool = cache_k.shape[1]
    past = page_table.shape[1] * PAGE_SIZE
    mp, ms = bsz * seq, dbs * dseq
    tm_p = 512
    pos_p = jnp.arange(seq, dtype=jnp.int32)
    pos_s = past + jnp.arange(dseq, dtype=jnp.int32)
    pos_s_tok = jnp.tile(pos_s, dbs)

    def tables(pos):
        return {'tab_a': _rope_tables(pos, ROT_DIMS, ROPE_THETA, HEAD_DIM),
                'tab_i': _rope_tables(pos, IDX_ROT, ROPE_THETA, IDX_DIM),
                'tab_k': _rope_tables(pos, IDX_ROT, ROPE_THETA, IDX_DIM, active=IDX_DIM)}

    tabs_p = dict(tables(pos_p), tab_r=_rope_tables(pos_p, RET_DK, RET_THETA, RET_DK))
    tabs_s = dict(tables(pos_s_tok), tab_r=_rope_tables(pos_s, RET_DK, RET_THETA, RET_DK))
    ck = cache_k.transpose(0, 1, 3, 4, 2).reshape(depth * n_pool, KVW, PAGE_SIZE)
    cv = cache_v.transpose(0, 1, 3, 4, 2).reshape(depth * n_pool, KVW, PAGE_SIZE)
    cik = cache_idx_k.transpose(0, 1, 3, 2).reshape(depth * n_pool, IDX_DIM, PAGE_SIZE)
    wr_pad = jnp.pad(w_router, ((0, 0), (0, LANE - N_EXPERTS)))
    rb_pad = jnp.pad(router_bias, (0, LANE - N_EXPERTS)).reshape(1, LANE)
    c_all = jnp.concatenate([c_prompt, c_sample], axis=0)

    geom_p = (bsz, seq, tm_p, seq // tm_p, 64, 256, 256, 256, 256)
    geom_s = (dbs, dseq, ms, 1, dseq, LANE, dseq, GDN_CHUNK, dseq)
    zero_st = {'s5': jnp.zeros((bsz, 2 * S5_NS), F32), 'ret': jnp.zeros((bsz, BW, RET_DK), F32),
               'gdn': jnp.zeros((bsz, BW, GDN_DK), F32), 'conv': jnp.zeros((bsz, GDN_CONV - 1, 3 * BW), F32)}

    xp = x_prompt.reshape(mp, D_MODEL)
    xs = x_sample.reshape(ms, D_MODEL)
    outs_p, outs_s = [], []
    for l in range(depth):
        w_in_p, w_misc, w_iwt = pack_w_in(w_in[l])
        abar8, win = s5_params(s5_log_dt[l], s5_a_re[l], s5_a_im[l], s5_b_re[l], s5_b_im[l])
        lw = {'norm1': norm1_g[l], 'norm2': norm2_g[l], 'w_in': w_in_p, 'w_misc': w_misc, 'w_iwt': w_iwt,
              's5_abar': abar8, 's5_win': win,
              's5_wout': _bf(jnp.concatenate([_block_diag_out(s5_c_re[l]), _block_diag_out(s5_c_im[l])], axis=0)),
              's5_d': s5_d[l], 's5_w_glu': _bf(s5_w_glu[l]), 's5_b_glu': s5_b_glu[l],
              'gdn_conv_w': gdn_conv_w[l], 'gdn_a_log': gdn_a_log[l], 'gdn_dt_bias': gdn_dt_bias[l],
              'gdn_norm_g': gdn_norm_g[l],
              'w_br': _bf(w_br[l]), 'w_gate': _bf(w_gate[l]), 'w_out': _bf(w_out[l]),
              'w_router': wr_pad, 'router_bias': rb_pad,
              'w_e1': _bf(w_e1[l]), 'w_e3': _bf(w_e3[l]), 'w_e2': _bf(w_e2[l])}
        mod = ada_mod(c_all, _bf(w_ada[l]), b_ada[l])
        mods = [mod[:, i * D_MODEL:(i + 1) * D_MODEL] for i in range(6)]
        mods_p = [m[:bsz].reshape(bsz, 1, D_MODEL) for m in mods]
        mods_s = [jnp.repeat(m[bsz:], dseq, axis=0).reshape(1, ms, D_MODEL) for m in mods]
        final = l == depth - 1

        def attend_p(qr, kr, v, vt, iqr, ikr, ik4, iwt, misc):
            return attn_prompt(qr, kr, vt, iqr, ik4, iwt, bsz, seq)

        def attend_s(qr, kr, v, vt, iqr, ikr, ik4, iwt, misc, l=l):
            return attn_sample(qr, kr, v, iqr, ik4, misc, ck, cv, cik, page_table, l, dseq)

        st_s = {'s5': jnp.concatenate([state_s5_re[l].reshape(dbs, S5_NS), state_s5_im[l].reshape(dbs, S5_NS)], axis=1),
                'ret': state_ret[l].reshape(dbs, BW, RET_DK), 'gdn': state_gdn[l].reshape(dbs, BW, GDN_DK),
                'conv': state_gdn_conv[l]}
        xp, ns_p = _trunk_layer(xp, mods_p, geom_p, attend_p, zero_st, dict(lw, **tabs_p), final_g, final)
        xs, ns_s = _trunk_layer(xs, mods_s, geom_s, attend_s, st_s, dict(lw, **tabs_s), final_g, final)
        outs_p.append(ns_p)
        outs_s.append(ns_s)

    def stack(outs, name, shape):
        return jnp.stack([o[name] for o in outs], axis=0).reshape((depth,) + shape)

    def states(outs, b, t):
        re = jnp.stack([o['s5'][:, :S5_NS] for o in outs], axis=0).reshape(depth, b, S5_GROUPS, S5_STATE)
        im = jnp.stack([o['s5'][:, S5_NS:] for o in outs], axis=0).reshape(depth, b, S5_GROUPS, S5_STATE)
        ret = stack(outs, 'ret', (b, RET_HEADS, RET_DK, RET_DK))
        gdn = stack(outs, 'gdn', (b, GDN_HEADS, GDN_DK, GDN_DK))
        return (stack(outs, 'k', (b, t, KV_HEADS, HEAD_DIM)), stack(outs, 'v', (b, t, KV_HEADS, HEAD_DIM)),
                stack(outs, 'ik', (b, t, IDX_DIM)), re, im, ret, gdn,
                stack(outs, 'conv', (b, GDN_CONV - 1, 3 * BW)))

    kp, vp, ikp, rep, imp, retp, gdnp, convp = states(outs_p, bsz, seq)
    ks_, vs_, iks, res, ims, rets, gdns, convs = states(outs_s, dbs, dseq)
    return (xp.reshape(bsz, seq, D_MODEL), xs.reshape(dbs, dseq, D_MODEL), kp, vp, ikp, ks_, vs_, iks,
            rep, imp, res, ims, retp, rets, gdnp, gdns, convp, convs)
```

```python
import functools
import math

import numpy as np
import jax
import jax.numpy as jnp
from jax import lax
from jax.experimental import pallas as pl
from jax.experimental.pallas import tpu as pltpu

F32 = jnp.float32
BF16 = jnp.bfloat16

D_MODEL = 1024
DEPTH = 2
PAST_LEN = 8192
PAGE_SIZE = 128
ATT_HEADS = 8
KV_HEADS = 2
HEAD_DIM = 64
ROT_DIMS = HEAD_DIM // 4
ROPE_THETA = 500000.0
IDX_HEADS = 4
IDX_DIM = 32
IDX_ROT = IDX_DIM // 4
TOPK_MAX = 256
S5_GROUPS = 24
S5_GROUP_CH = 16
S5_STATE = 64
S5_WIDTH = S5_GROUPS * S5_GROUP_CH
S5_NS = S5_GROUPS * S5_STATE
RET_HEADS = 6
RET_DK = 64
RET_THETA = 10000.0
GDN_HEADS = 6
GDN_DK = 64
GDN_CONV = 4
GDN_CHUNK = 64
ATT_WIDTH = ATT_HEADS * HEAD_DIM
BW = 384
KVW = KV_HEADS * HEAD_DIM
N_BRANCH = 4
MIX_WIDTH = ATT_WIDTH + 3 * BW
N_EXPERTS = 16
N_GROUPS = 4
EXPERTS_PER_GROUP = 4
EXPERT_FF = 256
EPS = 1e-6

PROJ_COLS = 12 * BW
ATT_COL0 = 9 * BW
MISC_DA = 36
MISC_DB = 42
LANE = 128
VMEM_LIMIT = 56 * 1024 * 1024
NEG = -1e30


def _cp(sem):
    return pltpu.CompilerParams(dimension_semantics=sem, vmem_limit_bytes=VMEM_LIMIT)


def _bf(x):
    return x.astype(BF16)


def _dot(a, b):
    return jnp.dot(_bf(a), _bf(b), preferred_element_type=F32)


def _dot_nt(a, b):
    return lax.dot_general(_bf(a), _bf(b), (((1,), (1,)), ((), ())), preferred_element_type=F32)


def _dot_tn(a, b):
    return lax.dot_general(_bf(a), _bf(b), (((0,), (0,)), ((), ())), preferred_element_type=F32)


def _split(x):
    hi = x.astype(BF16)
    lo = (x - hi.astype(F32)).astype(BF16)
    return hi, lo


def _dot3(a, b):
    ah, al = _split(a)
    bh, bl = _split(b)
    d = functools.partial(jnp.dot, preferred_element_type=F32)
    return d(ah, bh) + (d(ah, bl) + d(al, bh))


def _dot2(a, b01):
    ah, al = _split(a)
    d = functools.partial(jnp.dot, preferred_element_type=F32)
    return d(ah, b01) + d(al, b01)


def _sigmoid(x):
    return 1.0 / (1.0 + jnp.exp(-x))


def _silu(x):
    return x * _sigmoid(x)


def _lane_iota(shape):
    return lax.broadcasted_iota(jnp.int32, shape, len(shape) - 1)


def _row_iota(shape):
    return lax.broadcasted_iota(jnp.int32, shape, len(shape) - 2)


def _ada_kernel(c_ref, w_ref, b_ref, o_ref):
    o_ref[...] = _dot(_silu(c_ref[...]), w_ref[...]) + b_ref[...]


def ada_mod(c, w_bf, b):
    n = c.shape[0]
    cols = w_bf.shape[1]
    tn = 1024
    return pl.pallas_call(
        _ada_kernel,
        grid=(cols // tn,),
        in_specs=[pl.BlockSpec((n, D_MODEL), lambda j: (0, 0)),
                  pl.BlockSpec((D_MODEL, tn), lambda j: (0, j)),
                  pl.BlockSpec((1, tn), lambda j: (0, j))],
        out_specs=pl.BlockSpec((n, tn), lambda j: (0, j)),
        out_shape=jax.ShapeDtypeStruct((n, cols), F32),
        compiler_params=_cp(("parallel",)),
        name="ada_mod",
    )(c, w_bf, b.reshape(1, cols))


def _norm_mod(x, g, sc, sh):
    y = x * lax.rsqrt(jnp.mean(x * x, axis=-1, keepdims=True) + EPS) * g
    return y * (1.0 + sc) + sh


def _in_kernel(x_ref, g_ref, sc_ref, sh_ref, w_ref, wm_ref, wt_ref, o_ref, om_ref, ot_ref, h_sc):
    @pl.when(pl.program_id(1) == 0)
    def _():
        h = _norm_mod(x_ref[...], g_ref[...], sc_ref[0], sh_ref[0])
        h_sc[...] = h.astype(BF16)
        hh, hl = _split(h)
        wmh, wml = _split(wm_ref[...])
        d = functools.partial(jnp.dot, preferred_element_type=F32)
        om_ref[...] = d(hh, wmh) + (d(hh, wml) + d(hl, wmh))
        wth, wtl = _split(wt_ref[...])
        nt = functools.partial(lax.dot_general, dimension_numbers=(((1,), (1,)), ((), ())),
                               preferred_element_type=F32)
        ot_ref[...] = nt(wth, hh) + (nt(wth, hl) + nt(wtl, hh))

    o_ref[...] = jnp.dot(h_sc[...], w_ref[...], preferred_element_type=F32)


def _mod_spec(tm, mod_rows, tiles_per_seq):
    if mod_rows == 1:
        return pl.BlockSpec((1, 1, D_MODEL), lambda i, *_: (i // tiles_per_seq, 0, 0))
    return pl.BlockSpec((1, tm, D_MODEL), lambda i, *_: (i, 0, 0))


def in_proj(x, g, sc, sh, w_bf, w_misc, w_iwt, tm, tiles_per_seq):
    m = x.shape[0]
    tn = PROJ_COLS
    mod_rows = sc.shape[1]
    ms = _mod_spec(tm, mod_rows, tiles_per_seq)
    return pl.pallas_call(
        _in_kernel,
        grid=(m // tm, PROJ_COLS // tn),
        in_specs=[pl.BlockSpec((tm, D_MODEL), lambda i, j: (i, 0)),
                  pl.BlockSpec((1, D_MODEL), lambda i, j: (0, 0)),
                  ms, ms,
                  pl.BlockSpec((D_MODEL, tn), lambda i, j: (0, j)),
                  pl.BlockSpec((D_MODEL, LANE), lambda i, j: (0, 0)),
                  pl.BlockSpec((8, D_MODEL), lambda i, j: (0, 0))],
        out_specs=[pl.BlockSpec((tm, tn), lambda i, j: (i, j)),
                   pl.BlockSpec((tm, LANE), lambda i, j: (i, 0)),
                   pl.BlockSpec((8, tm), lambda i, j: (0, i))],
        out_shape=[jax.ShapeDtypeStruct((m, PROJ_COLS), F32),
                   jax.ShapeDtypeStruct((m, LANE), F32),
                   jax.ShapeDtypeStruct((8, m), F32)],
        scratch_shapes=[pltpu.VMEM((tm, D_MODEL), BF16)],
        compiler_params=_cp(("parallel", "arbitrary")),
        name="in_proj",
    )(x, g.reshape(1, D_MODEL), sc, sh, w_bf, w_misc, w_iwt)


def _rope_tables(pos, rot_dims, theta, period, width=LANE, active=None):
    half = rot_dims // 2
    inv_freq = jnp.power(jnp.float32(theta), -jnp.arange(half, dtype=F32) / half)
    ang = pos.astype(F32)[:, None] * inv_freq
    cos, sin = jnp.cos(ang), jnp.sin(ang)
    t = pos.shape[0]
    c = jnp.concatenate([cos, cos, jnp.ones((t, period - rot_dims), F32)], axis=1)
    s_up = jnp.concatenate([-sin, jnp.zeros((t, period - half), F32)], axis=1)
    s_dn = jnp.concatenate([jnp.zeros((t, half), F32), sin, jnp.zeros((t, period - rot_dims), F32)], axis=1)
    reps = width // period
    tab = jnp.stack([jnp.tile(a, (1, reps)) for a in (c, s_up, s_dn)], axis=0)
    if active is not None:
        ident = jnp.stack([jnp.ones((t, width), F32), jnp.zeros((t, width), F32),
                           jnp.zeros((t, width), F32)], axis=0)
        tab = jnp.where(jnp.arange(width) < active, tab, ident)
    return tab


def _rope(x, tab_ref, half):
    w = x.shape[1]
    reps = w // LANE

    def wide(k):
        t = tab_ref[k]
        return t if reps == 1 else jnp.concatenate([t] * reps, axis=1)

    return (x * wide(0) + pltpu.roll(x, w - half, axis=1) * wide(1)
            + pltpu.roll(x, half, axis=1) * wide(2))


def _prep_kernel(p_ref, m_ref, ta_ref, ti_ref, tk_ref, q_ref, k_ref, v_ref, iq_ref, ik_ref, ik4_ref, vt_ref):
    q_ref[...] = _rope(p_ref[:, 0:ATT_WIDTH], ta_ref, ROT_DIMS // 2)
    k_ref[...] = _rope(p_ref[:, 512:640], ta_ref, ROT_DIMS // 2)
    v_ref[...] = p_ref[:, 640:768]
    vt_ref[0] = p_ref[:, 640:768].T
    iq_ref[...] = _rope(p_ref[:, 768:896], ti_ref, IDX_ROT // 2)
    ikr = _rope(m_ref[...], tk_ref, IDX_ROT // 2)
    ik_ref[...] = ikr[:, 0:IDX_DIM]
    m = jnp.where(_lane_iota(ikr.shape) < IDX_DIM, ikr, 0.0)
    ik4_ref[...] = (m + pltpu.roll(m, 32, axis=1)) + (pltpu.roll(m, 64, axis=1) + pltpu.roll(m, 96, axis=1))


def attn_prep(proj, misc, tab_a, tab_i, tab_k, tm, tiles_per_seq):
    m = proj.shape[0]
    tspec = pl.BlockSpec((3, tm, LANE), lambda i: (0, i % tiles_per_seq, 0))
    widths = (ATT_WIDTH, KVW, KVW, LANE, IDX_DIM, LANE)
    seq = tm * tiles_per_seq
    return pl.pallas_call(
        _prep_kernel,
        grid=(m // tm,),
        in_specs=[pl.BlockSpec((tm, 3 * BW), lambda i: (i, ATT_COL0 // (3 * BW))),
                  pl.BlockSpec((tm, LANE), lambda i: (i, 0)), tspec, tspec, tspec],
        out_specs=[pl.BlockSpec((tm, w), lambda i: (i, 0)) for w in widths]
        + [pl.BlockSpec((1, KVW, tm), lambda i: (i // tiles_per_seq, 0, i % tiles_per_seq))],
        out_shape=[jax.ShapeDtypeStruct((m, w), F32) for w in widths]
        + [jax.ShapeDtypeStruct((m // seq, KVW, seq), F32)],
        compiler_params=_cp(("parallel",)),
        name="attn_prep",
    )(proj, misc, tab_a, tab_i, tab_k)


BISECT_MAX_ITERS = 48
BISECT_UNROLL = 4
FAR = 2.0 ** 126


def _count(ones, axis):
    return jnp.sum(ones, axis=axis, keepdims=True)


COUNT_ROWS = 32
COUNT_ACCS = 4


def _count_where(s_ref, pred, axis):
    n = s_ref.shape[0]
    if axis != 0 or n % (COUNT_ROWS * COUNT_ACCS) != 0:
        return _count(pred(s_ref[...]), axis)
    accs = [None] * COUNT_ACCS
    for j, i in enumerate(range(0, n, COUNT_ROWS)):
        part = pred(s_ref[i:i + COUNT_ROWS, :])
        a = j % COUNT_ACCS
        accs[a] = part if accs[a] is None else accs[a] + part
    return jnp.sum((accs[0] + accs[1]) + (accs[2] + accs[3]), axis=0, keepdims=True)


def _bisect_topk(s_ref, axis, topk):
    kshape = tuple(1 if a == axis else n for a, n in enumerate(s_ref.shape))
    s = s_ref[...]
    lo0 = jnp.min(jnp.where(s > -FAR, s, FAR), axis=axis, keepdims=True)
    mx = jnp.max(s, axis=axis, keepdims=True)
    hi0 = mx + (jnp.abs(mx) * 2.0 ** -20 + 1e-30)
    cnt_lo0 = _count_where(s_ref, lambda t: jnp.where(t >= lo0, 1, 0), axis)
    n_zero = _count_where(s_ref, lambda t: jnp.where(t == 0.0, 1, 0), axis)

    def pending(lo, hi, cnt_lo, cnt_hi):
        only_zeros = jnp.where(lo <= 0.0, jnp.where(hi > 0.0, jnp.where(cnt_lo - cnt_hi == n_zero, 1, 0), 0), 0)
        return jnp.max(jnp.where(cnt_lo <= topk, 0, 1 - only_zeros))

    def cond(c):
        return jnp.logical_and(c[0] < BISECT_MAX_ITERS, c[1] > 0)

    def body(c):
        it, _, lo, hi, cnt_lo, cnt_hi = c
        for _ in range(BISECT_UNROLL):
            mid = 0.5 * lo + 0.5 * hi
            cm = _count_where(s_ref, lambda t: jnp.where(t >= mid, 1, 0), axis)
            ge = cm >= topk
            lo, hi = jnp.where(ge, mid, lo), jnp.where(ge, hi, mid)
            cnt_lo, cnt_hi = jnp.where(ge, cm, cnt_lo), jnp.where(ge, cnt_hi, cm)
        return it + BISECT_UNROLL, pending(lo, hi, cnt_lo, cnt_hi), lo, hi, cnt_lo, cnt_hi

    cnt_hi0 = jnp.zeros(kshape, jnp.int32)
    init = (jnp.int32(0), pending(lo0, hi0, cnt_lo0, cnt_hi0), lo0, hi0, cnt_lo0, cnt_hi0)
    _, _, lo, hi, cnt_lo, cnt_hi = lax.while_loop(cond, body, init)
    return lo, hi, cnt_lo, cnt_hi


def _topk_select(s_ref, idx, axis, n_idx_bits, p_sc, topk):
    kshape = tuple(1 if a == axis else n for a, n in enumerate(s_ref.shape))
    lo, hi, cnt_lo, cnt_hi = _bisect_topk(s_ref, axis, topk)
    need = topk - cnt_hi
    p_sc[...] = jnp.full(kshape, (1 << n_idx_bits) - 1, jnp.int32)

    @pl.when(jnp.max(cnt_lo - cnt_hi - need) > 0)
    def _():
        tied = jnp.where(s_ref[...] >= lo, jnp.where(s_ref[...] >= hi, 0, 1), 0)

        def ibody(i, p):
            cand = p + jnp.left_shift(jnp.int32(1), n_idx_bits - 1 - i)
            taken = _count(jnp.where(idx < cand, tied, 0), axis)
            return jnp.where(taken < need, cand, p)

        p_sc[...] = lax.fori_loop(0, n_idx_bits, ibody, jnp.zeros(kshape, jnp.int32))

    s = s_ref[...]
    return jnp.where(s >= hi, 1, jnp.where(s >= lo, jnp.where(idx <= p_sc[...], 1, 0), 0))


def _topk_bias_keys_major(s_ref, tri_ref, topk):
    tk, tq = s_ref.shape
    lo, hi, _, cnt_hi = _bisect_topk(s_ref, 0, topk)
    need = (topk - cnt_hi).astype(F32)
    tri = tri_ref[...]
    offset = jnp.zeros((1, tq), F32)
    parts = []
    for c in range(tk // tri.shape[0]):
        s = s_ref[c * tri.shape[0]:(c + 1) * tri.shape[0], :]
        cand = jnp.where(s >= lo, jnp.where(s >= hi, 0.0, 1.0), 0.0)
        rank = jnp.dot(tri, _bf(cand), preferred_element_type=F32) + offset
        offset = rank[tri.shape[0] - 1:tri.shape[0], :]
        parts.append(jnp.where(s >= hi, 0.0, jnp.where(cand * rank > 0.5, jnp.where(rank <= need, 0.0, NEG), NEG)))
    return jnp.concatenate(parts, axis=0)


def _group_queries(q, g):
    tiles = []
    keep = (_lane_iota((q.shape[0], LANE)) // HEAD_DIM) == g
    for hl in range(ATT_HEADS // KV_HEADS):
        h = g * (ATT_HEADS // KV_HEADS) + hl
        t = q[:, (h // 2) * LANE:(h // 2 + 1) * LANE]
        if h % 2 != g:
            t = pltpu.roll(t, HEAD_DIM, axis=1)
        tiles.append(jnp.where(keep, t, 0.0))
    return jnp.concatenate(tiles, axis=0)


def _ungroup_outputs(o_groups, tq):
    low = _lane_iota((tq, LANE)) < HEAD_DIM
    tiles = []
    for j in range(ATT_HEADS // 2):
        halves = []
        for h in (2 * j, 2 * j + 1):
            g, hl = divmod(h, ATT_HEADS // KV_HEADS)
            t = o_groups[g][hl * tq:(hl + 1) * tq]
            if h % 2 != g:
                t = pltpu.roll(t, HEAD_DIM, axis=1)
            halves.append(t)
        tiles.append(jnp.where(low, halves[0], halves[1]))
    return jnp.concatenate(tiles, axis=1)


def _masked_attention_keys_major(q, kb, vt, bias_t):
    tq = q.shape[0]
    heads = ATT_HEADS // KV_HEADS
    tk = kb.shape[0]
    bias4 = jnp.concatenate([bias_t] * heads, axis=1)
    vrow_group = _row_iota((LANE, tk)) // HEAD_DIM
    q = q * HEAD_DIM ** -0.5
    normed = []
    for g in range(KV_HEADS):
        st = _dot_nt(kb, _group_queries(q, g)) + bias4
        p = jnp.exp(_bf(st - jnp.max(st, axis=0, keepdims=True)))
        ot = jnp.dot(_bf(jnp.where(vrow_group == g, vt, 1.0)), p, preferred_element_type=F32)
        other = (1 - g) * HEAD_DIM
        normed.append(ot[g * HEAD_DIM:(g + 1) * HEAD_DIM, :] / ot[other:other + 1, :])
    tiles = []
    for j in range(ATT_HEADS // 2):
        g, hl = divmod(2 * j, heads)
        pair = jnp.concatenate([normed[g][:, hl * tq:(hl + 1) * tq], normed[g][:, (hl + 1) * tq:(hl + 2) * tq]], axis=0)
        tiles.append(pair.T)
    return jnp.concatenate(tiles, axis=1)


def _attn_prompt_kernel(q_ref, iq_ref, iwt_ref, k_ref, vt_ref, ik4_ref, tri_ref, o_ref, s_sc, *, qblk0, tq, topk):
    tk = k_ref.shape[1]
    q0 = (qblk0 + pl.program_id(1)) * tq
    iq = iq_ref[0]
    head_of_lane = _lane_iota((tq, LANE)) // IDX_DIM
    iq4 = jnp.concatenate([jnp.where(head_of_lane == h, iq, 0.0) for h in range(IDX_HEADS)], axis=0)
    lg = _dot_nt(ik4_ref[0], iq4)
    iwt = iwt_ref[...] * (IDX_HEADS ** -0.5 * IDX_DIM ** -0.5)
    score = None
    for h in range(IDX_HEADS):
        part = jnp.maximum(lg[:, h * tq:(h + 1) * tq], 0.0) * iwt[h:h + 1, :]
        score = part if score is None else score + part
    adm = _row_iota((tk, tq)) <= q0 + _lane_iota((tk, tq))
    s_sc[...] = jnp.where(adm, score, -FAR)
    bias_t = _topk_bias_keys_major(s_sc, tri_ref, topk)
    o_ref[0] = _masked_attention_keys_major(q_ref[0], _bf(k_ref[0]), vt_ref[0], bias_t)


PREFIX_ROWS = 256


def attn_prompt(qr, kr, vt, iqr, ik4, iwt, bsz, seq, n_classes=8, tq=128):
    topk = min(TOPK_MAX, seq // 4)
    nq = seq // tq
    per = max(1, nq // n_classes)
    q3 = qr.reshape(bsz, seq, ATT_WIDTH)
    iq3 = iqr.reshape(bsz, seq, LANE)
    k3, ik3 = (a.reshape(bsz, seq, LANE) for a in (kr, ik4))
    outs = []
    for c in range(nq // per):
        tk = (c + 1) * per * tq
        qb0 = c * per
        r = np.arange(math.gcd(PREFIX_ROWS, tk))
        tri = jnp.asarray(r[:, None] >= r[None, :], BF16)
        out = pl.pallas_call(
            functools.partial(_attn_prompt_kernel, qblk0=qb0, tq=tq, topk=topk),
            grid=(bsz, per),
            in_specs=[pl.BlockSpec((1, tq, ATT_WIDTH), lambda b, j, qb0=qb0: (b, qb0 + j, 0)),
                      pl.BlockSpec((1, tq, LANE), lambda b, j, qb0=qb0: (b, qb0 + j, 0)),
                      pl.BlockSpec((8, tq), lambda b, j, qb0=qb0: (0, b * nq + qb0 + j)),
                      pl.BlockSpec((1, tk, LANE), lambda b, j: (b, 0, 0)),
                      pl.BlockSpec((1, LANE, tk), lambda b, j: (b, 0, 0)),
                      pl.BlockSpec((1, tk, LANE), lambda b, j: (b, 0, 0)),
                      pl.BlockSpec(tri.shape, lambda b, j: (0, 0))],
            out_specs=pl.BlockSpec((1, tq, ATT_WIDTH), lambda b, j: (b, j, 0)),
            out_shape=jax.ShapeDtypeStruct((bsz, per * tq, ATT_WIDTH), F32),
            scratch_shapes=[pltpu.VMEM((tk, tq), F32)],
            compiler_params=_cp(("parallel", "arbitrary")),
            name=f"attn_prompt_{tk}",
        )(q3, iq3, iwt, k3, vt, ik3, tri)
        outs.append(out)
    return jnp.concatenate(outs, axis=1).reshape(bsz * seq, ATT_WIDTH)


def _attn_sample_kernel(pt_ref, q_ref, iq_ref, misc_ref, kn_ref, vn_ref, ik4n_ref, *rest, npg, topk):
    del pt_ref
    kp, vp, ikp = rest[0:npg], rest[npg:2 * npg], rest[2 * npg:3 * npg]
    o_ref, key_sc, p_sc = rest[3 * npg:]
    tq = q_ref.shape[1]
    past = npg * PAGE_SIZE
    lk = past + LANE
    k_tile = lambda j: kp[j][0] if j < npg else _pad_rows(kn_ref[0], LANE).T
    v_tile = lambda j: vp[j][0] if j < npg else _pad_rows(vn_ref[0], LANE).T
    ik_tile = lambda j: ikp[j][0] if j < npg else _pad_rows(ik4n_ref[0], LANE).T[0:IDX_DIM, :]
    tile = lambda j: slice(j * LANE, (j + 1) * LANE)
    iq = iq_ref[0]
    iqs = _bf(jnp.concatenate([iq[:, h * IDX_DIM:(h + 1) * IDX_DIM] for h in range(IDX_HEADS)], axis=0))
    misc = misc_ref[0]
    iw = jnp.concatenate([misc[:, IDX_DIM + h:IDX_DIM + h + 1] for h in range(IDX_HEADS)], axis=0)
    iw = iw * (IDX_HEADS ** -0.5 * IDX_DIM ** -0.5)
    for j in range(npg + 1):
        wl = jnp.maximum(_dot(iqs, ik_tile(j)), 0.0) * iw
        score = (wl[0:tq] + wl[tq:2 * tq]) + (wl[2 * tq:3 * tq] + wl[3 * tq:4 * tq])
        if j == npg:
            score = jnp.where(_lane_iota((tq, LANE)) <= _row_iota((tq, LANE)), score, -FAR)
        key_sc[:, tile(j)] = score
    sel = _topk_select(key_sc, _lane_iota((tq, lk)), 1, (lk - 1).bit_length(), p_sc, topk)
    bias = jnp.where(sel > 0, 0.0, NEG)
    heads = ATT_HEADS // KV_HEADS
    q = q_ref[0] * HEAD_DIM ** -0.5
    qg = _bf(jnp.concatenate([_group_queries(q, g) for g in range(KV_HEADS)], axis=0))
    s = jnp.concatenate([_dot(qg, k_tile(j)) for j in range(npg + 1)], axis=1)
    s = s + jnp.concatenate([bias] * ATT_HEADS, axis=0)
    pr = jnp.exp(s - jnp.max(s, axis=-1, keepdims=True))
    o = None
    for j in range(npg + 1):
        part = _dot_nt(pr[:, tile(j)], v_tile(j))
        o = part if o is None else o + part
    o = o / jnp.sum(pr, axis=-1, keepdims=True)
    o_ref[0] = _ungroup_outputs([o[g * heads * tq:(g + 1) * heads * tq] for g in range(KV_HEADS)], tq)


def attn_sample(qr, kr, v, iqr, ik4, misc, cache_kt, cache_vt, cache_ikt, page_table, layer, dseq):
    db, npg = page_table.shape
    n_pool = cache_kt.shape[0] // DEPTH
    lk = npg * PAGE_SIZE + LANE
    topk = min(TOPK_MAX, (npg * PAGE_SIZE + dseq) // 4)
    base = layer * n_pool
    r3 = lambda a: a.reshape(db, dseq, a.shape[-1])
    row_spec = lambda w: pl.BlockSpec((1, dseq, w), lambda b, pt: (b, 0, 0))

    def page_spec(w, j):
        return pl.BlockSpec((1, w, PAGE_SIZE), lambda b, pt, j=j: (pt[b, j] + base, 0, 0))

    in_specs = [row_spec(ATT_WIDTH), row_spec(LANE), row_spec(LANE),
                row_spec(LANE), row_spec(LANE), row_spec(LANE)]
    in_specs += [page_spec(KVW, j) for j in range(npg)]
    in_specs += [page_spec(KVW, j) for j in range(npg)]
    in_specs += [page_spec(IDX_DIM, j) for j in range(npg)]
    out = pl.pallas_call(
        functools.partial(_attn_sample_kernel, npg=npg, topk=topk),
        grid_spec=pltpu.PrefetchScalarGridSpec(
            num_scalar_prefetch=1,
            grid=(db,),
            in_specs=in_specs,
            out_specs=pl.BlockSpec((1, dseq, ATT_WIDTH), lambda b, pt: (b, 0, 0)),
            scratch_shapes=[pltpu.VMEM((dseq, lk), F32), pltpu.VMEM((dseq, 1), jnp.int32)]),
        out_shape=jax.ShapeDtypeStruct((db, dseq, ATT_WIDTH), F32),
        compiler_params=_cp(("parallel",)),
        name="attn_sample",
    )(page_table, r3(qr), r3(iqr), r3(misc), r3(kr), r3(v), r3(ik4),
      *([cache_kt] * npg), *([cache_vt] * npg), *([cache_ikt] * npg))
    return out.reshape(db * dseq, ATT_WIDTH)


def _s5_param_kernel(ldt_ref, are_ref, aim_ref, bre_ref, bim_ref, abar_ref, win_ref):
    dt = jnp.exp(ldt_ref[...])
    a_re, a_im = are_ref[...], aim_ref[...]
    mag = jnp.exp(dt * a_re)
    abar_re = mag * jnp.cos(dt * a_im)
    abar_im = mag * jnp.sin(dt * a_im)
    den = a_re * a_re + a_im * a_im
    num_re = abar_re - 1.0
    coef_re = (num_re * a_re + abar_im * a_im) / den
    coef_im = (abar_im * a_re - num_re * a_im) / den
    abar_ref[:, 0:S5_NS] = jnp.broadcast_to(abar_re, (8, S5_NS))
    abar_ref[:, S5_NS:2 * S5_NS] = jnp.broadcast_to(abar_im, (8, S5_NS))
    b_re, b_im = bre_ref[...], bim_ref[...]
    win_ref[:, 0:S5_NS] = _bf(coef_re * b_re - coef_im * b_im)
    win_ref[:, S5_NS:2 * S5_NS] = _bf(coef_re * b_im + coef_im * b_re)


def _block_diag_in(b):
    eye = jnp.eye(S5_GROUPS, dtype=b.dtype)
    return jnp.einsum('gnc,gh->gchn', b, eye).reshape(S5_WIDTH, S5_NS)


def _block_diag_out(c):
    eye = jnp.eye(S5_GROUPS, dtype=c.dtype)
    return jnp.einsum('gcn,gh->gnhc', c, eye).reshape(S5_NS, S5_WIDTH)


def s5_params(log_dt, a_re, a_im, b_re, b_im):
    per_state = lambda a: a.reshape(1, S5_NS)
    ldt = per_state(jnp.broadcast_to(log_dt[:, None], (S5_GROUPS, S5_STATE)))
    return pl.pallas_call(
        _s5_param_kernel,
        out_shape=[jax.ShapeDtypeStruct((8, 2 * S5_NS), F32),
                   jax.ShapeDtypeStruct((S5_WIDTH, 2 * S5_NS), BF16)],
        compiler_params=pltpu.CompilerParams(vmem_limit_bytes=VMEM_LIMIT),
        name="s5_params",
    )(ldt, per_state(a_re), per_state(a_im), _block_diag_in(b_re), _block_diag_in(b_im))


def _gelu_tanh(x):
    return 0.5 * x * (1.0 + jnp.tanh(math.sqrt(2.0 / math.pi) * (x + 0.044715 * (x * x * x))))


S5_LANES = 512
S5_SLAB = (LANE // S5_GROUP_CH) * S5_STATE


def _s5_kernel(u_ref, abar_ref, win_ref, h0_ref, wout_ref, d_ref, wglu_ref, bglu_ref,
               y_ref, hn_ref, s_sc, *, bsz, tc):
    c = pl.program_id(0)

    @pl.when(c == 0)
    def _():
        hn_ref[...] = h0_ref[...]

    u = u_ref[...]
    ub = _bf(u)
    for kt in range(S5_WIDTH // LANE):
        ch = slice(kt * LANE, (kt + 1) * LANE)
        for part in range(2):
            st = slice(part * S5_NS + kt * S5_SLAB, part * S5_NS + (kt + 1) * S5_SLAB)
            s_sc[:, st] = jnp.dot(ub[:, ch], win_ref[ch, st], preferred_element_type=F32)
    nchunk = S5_NS // S5_LANES
    for rg in range(bsz // 8):
        rows = slice(rg * 8, rg * 8 + 8)

        def body(t, carry):
            row0 = pl.multiple_of(t * bsz + rg * 8, 8)
            new = []
            for cc in range(nchunk):
                lre = slice(cc * S5_LANES, (cc + 1) * S5_LANES)
                lim = slice(S5_NS + cc * S5_LANES, S5_NS + (cc + 1) * S5_LANES)
                xr, xi = carry[2 * cc], carry[2 * cc + 1]
                ar, ai = abar_ref[:, lre], abar_ref[:, lim]
                nr = (ar * xr - ai * xi) + s_sc[pl.ds(row0, 8), lre]
                ni = (ar * xi + ai * xr) + s_sc[pl.ds(row0, 8), lim]
                s_sc[pl.ds(row0, 8), lre] = nr
                s_sc[pl.ds(row0, 8), lim] = ni
                new += [nr, ni]
            return tuple(new)

        init = []
        for cc in range(nchunk):
            init += [hn_ref[rows, cc * S5_LANES:(cc + 1) * S5_LANES],
                     hn_ref[rows, S5_NS + cc * S5_LANES:S5_NS + (cc + 1) * S5_LANES]]
        fin = lax.fori_loop(0, tc, body, tuple(init))
        for cc in range(nchunk):
            hn_ref[rows, cc * S5_LANES:(cc + 1) * S5_LANES] = fin[2 * cc]
            hn_ref[rows, S5_NS + cc * S5_LANES:S5_NS + (cc + 1) * S5_LANES] = fin[2 * cc + 1]

    y_tiles = []
    for kt in range(S5_WIDTH // LANE):
        ch = slice(kt * LANE, (kt + 1) * LANE)
        re = slice(kt * S5_SLAB, (kt + 1) * S5_SLAB)
        im = slice(S5_NS + kt * S5_SLAB, S5_NS + (kt + 1) * S5_SLAB)
        y_tiles.append(jnp.dot(_bf(s_sc[:, re]), wout_ref[re, ch], preferred_element_type=F32)
                       - jnp.dot(_bf(s_sc[:, im]), wout_ref[im, ch], preferred_element_type=F32))
    y = jnp.concatenate(y_tiles, axis=1) + d_ref[...] * u
    z = _gelu_tanh(y)
    y_ref[...] = z * _sigmoid(jnp.dot(_bf(z), wglu_ref[...], preferred_element_type=F32) + bglu_ref[...])


def s5_branch(u_tm, h0, abar8, win, wout_bf, d, wglu_bf, bglu, bsz, seq, tc):
    rows = tc * bsz
    const = lambda shape: pl.BlockSpec(shape, lambda c: (0,) * len(shape))
    return pl.pallas_call(
        functools.partial(_s5_kernel, bsz=bsz, tc=tc),
        grid=(seq // tc,),
        in_specs=[pl.BlockSpec((rows, S5_WIDTH), lambda c: (c, 0)),
                  const((8, 2 * S5_NS)), const((S5_WIDTH, 2 * S5_NS)), const((bsz, 2 * S5_NS)),
                  const((2 * S5_NS, S5_WIDTH)), const((1, S5_WIDTH)), const((S5_WIDTH, S5_WIDTH)),
                  const((1, S5_WIDTH))],
        out_specs=[pl.BlockSpec((rows, S5_WIDTH), lambda c: (c, 0)), const((bsz, 2 * S5_NS))],
        out_shape=[jax.ShapeDtypeStruct((seq * bsz, S5_WIDTH), F32),
                   jax.ShapeDtypeStruct((bsz, 2 * S5_NS), F32)],
        scratch_shapes=[pltpu.VMEM((rows, 2 * S5_NS), F32)],
        compiler_params=_cp(("arbitrary",)),
        name="s5_branch",
    )(u_tm, abar8, win, h0, wout_bf, d.reshape(1, S5_WIDTH), wglu_bf, bglu.reshape(1, S5_WIDTH))


def _pad_rows(x, rows):
    n = x.shape[0]
    return x if n == rows else jnp.concatenate([x, jnp.zeros((rows - n, x.shape[1]), x.dtype)], axis=0)


def _head_mean(x, amat_bf):
    return _dot2(x, amat_bf)


def _split3(x):
    a = x.astype(BF16)
    r = x - a.astype(F32)
    b = r.astype(BF16)
    return a, b, (r - b.astype(F32)).astype(BF16)


def _expand_state(tall, rep_bf, bdm):
    d = functools.partial(jnp.dot, preferred_element_type=F32)
    a, b, c = _split3(tall)
    return (d(a, rep_bf) + (d(b, rep_bf) + d(c, rep_bf))) * bdm


def _collapse_state(bd, rept_bf):
    d = functools.partial(jnp.dot, preferred_element_type=F32)
    a, b, c = _split3(bd)
    return d(a, rept_bf) + (d(b, rept_bf) + d(c, rept_bf))


def _rep_consts():
    rep = np.tile(np.eye(64, dtype=np.float32), (1, BW // 64))
    return jnp.asarray(rep, BF16), jnp.asarray(rep.T, BF16)


def _ret_kernel(q_ref, k_ref, v_ref, g_ref, tab_ref, s0_ref, dmat_ref, qdec_ref, kdec_ref, decm_ref,
                bdm_ref, amat_ref, rep_ref, rept_ref, y_ref, so_ref, s_ref, *, rows):
    n = q_ref.shape[0]

    @pl.when(pl.program_id(1) == 0)
    def _():
        s_ref[...] = _expand_state(s0_ref[0], rep_ref[...], bdm_ref[...])

    q = _pad_rows(_rope(q_ref[...], tab_ref, RET_DK // 2), rows)
    k = _pad_rows(_rope(k_ref[...], tab_ref, RET_DK // 2) * RET_DK ** -0.5, rows)
    v = _pad_rows(v_ref[...], rows)
    state = s_ref[...]
    inter = _dot(q * qdec_ref[...], state)
    lane = _lane_iota((rows, LANE))
    tiles = []
    for p in range(RET_HEADS // 2):
        lanes = slice(p * LANE, (p + 1) * LANE)
        qp, kp, vp = q[:, lanes], _bf(k[:, lanes]), v[:, lanes]
        acc = None
        for hh in range(2):
            mine = (lane < RET_DK) if hh == 0 else (lane >= RET_DK)
            s = _dot_nt(jnp.where(mine, qp, 0.0), kp) * dmat_ref[2 * p + hh]
            part = _dot(s, jnp.where(mine, vp, 0.0))
            acc = part if acc is None else acc + part
        tiles.append(acc)
    o = jnp.concatenate(tiles, axis=1) + inter
    new_state = state * decm_ref[...] + _dot_tn(k * kdec_ref[...], v) * bdm_ref[...]
    s_ref[...] = new_state

    @pl.when(pl.program_id(1) == pl.num_programs(1) - 1)
    def _():
        so_ref[0] = _collapse_state(new_state, rept_ref[...])

    amat = amat_ref[...]
    mu = _head_mean(o, amat)
    d = o - mu
    var = _head_mean(d * d, amat)
    on = d * lax.rsqrt(var + 1e-5)
    y_ref[...] = (_silu(g_ref[...]) * on[0:n]).astype(y_ref.dtype)


def _head_block_mask():
    h = np.arange(BW) // 64
    return (h[:, None] == h[None, :]).astype(np.float32)


def _ret_consts(rows, n_true):
    lg = np.log(1.0 - np.exp2(-5.0 - np.arange(RET_HEADS, dtype=np.float64)))
    i = np.arange(rows, dtype=np.float64)
    rel = i[:, None] - i[None, :]
    dmat = np.where(rel[None] >= 0, np.exp(np.minimum(rel[None], rows) * lg[:, None, None]), 0.0)
    lane_lg = np.repeat(lg, 64)[None, :]
    qdec = np.exp((i[:, None] + 1.0) * lane_lg)
    kdec = np.where(i[:, None] < n_true, np.exp((n_true - 1.0 - i[:, None]) * lane_lg), 0.0)
    bdm = _head_block_mask()
    decm = bdm * np.exp(n_true * np.repeat(lg, 64))[:, None]
    f = lambda a: jnp.asarray(a, F32)
    return f(dmat), f(qdec), f(kdec), f(decm), f(bdm), jnp.asarray(bdm / 64.0, BF16)


def ret_branch(proj, tab_r, s0, bsz, seq, rows, n):
    nch = seq // n
    dmat, qdec, kdec, decm, bdm, amat = _ret_consts(rows, n)
    rep, rept = _rep_consts()
    col = lambda j: pl.BlockSpec((n, BW), lambda b, c, j=j: (b * nch + c, j))
    const = lambda shape: pl.BlockSpec(shape, lambda b, c: (0,) * len(shape))
    return pl.pallas_call(
        functools.partial(_ret_kernel, rows=rows),
        grid=(bsz, nch),
        in_specs=[col(0), col(1), col(2), col(3),
                  pl.BlockSpec((3, n, LANE), lambda b, c: (0, c, 0)),
                  pl.BlockSpec((1, BW, RET_DK), lambda b, c: (b, 0, 0)),
                  const((RET_HEADS, rows, rows)), const((rows, BW)), const((rows, BW)),
                  const((BW, BW)), const((BW, BW)), const((BW, BW)), const((RET_DK, BW)), const((BW, RET_DK))],
        out_specs=[pl.BlockSpec((n, BW), lambda b, c: (b * nch + c, 0)),
                   pl.BlockSpec((1, BW, RET_DK), lambda b, c: (b, 0, 0))],
        out_shape=[jax.ShapeDtypeStruct((bsz * seq, BW), F32),
                   jax.ShapeDtypeStruct((bsz, BW, RET_DK), F32)],
        scratch_shapes=[pltpu.VMEM((BW, BW), F32)],
        compiler_params=_cp(("parallel", "arbitrary")),
        name="ret_branch",
    )(proj, proj, proj, proj, tab_r, s0, dmat, qdec, kdec, decm, bdm, amat, rep, rept)


def _softplus(x):
    return jnp.maximum(x, 0.0) + jnp.log(1.0 + jnp.exp(-jnp.abs(x)))


GDN_SUPER = ((0, 4 * GDN_CHUNK), (4 * GDN_CHUNK, 6 * GDN_CHUNK))
GM_BLOCK, GM_INCL, GM_STRICT, GM_EYE, GM_LEVEL0 = 0, 1, 2, 3, 4


def _stack_heads(a, bdm_rows):
    return jnp.concatenate([a] * (bdm_rows.shape[0] // GDN_CHUNK), axis=0) * bdm_rows


def _unstack_heads(parts):
    blocks = [p[i:i + GDN_CHUNK] for p in parts for i in range(0, p.shape[0], GDN_CHUNK)]
    out = blocks[0]
    for b in blocks[1:]:
        out = out + b
    return out


def _col_of_heads(a, s, e):
    return jnp.concatenate([jnp.broadcast_to(a[:, h * GDN_DK:h * GDN_DK + 1], (GDN_CHUNK, e - s))
                            for h in range(s // GDN_CHUNK, e // GDN_CHUNK)], axis=0)


def _row_of_heads(a_t, s, e):
    return jnp.concatenate([a_t[h * GDN_DK:h * GDN_DK + 1, :] for h in range(s // GDN_CHUNK, e // GDN_CHUNK)],
                           axis=1)


def _gdn_prepare(chunks, gm_ref, n_real):
    items = [(ci, s, e) for ci in range(len(chunks)) for s, e in GDN_SUPER]
    g_ts = [gc.T for _, _, _, _, gc in chunks]
    nmats, decs, qks = [], [], []
    for ci, s, e in items:
        qc, kc, _, bc, gc = chunks[ci]
        bdm_rows = gm_ref[GM_BLOCK, s:e, :]
        ks = _bf(_stack_heads(kc, bdm_rows))
        kk = _dot_nt(ks, ks)
        qks.append(_dot_nt(_stack_heads(qc, bdm_rows), ks))
        diff = _col_of_heads(gc, s, e) - _row_of_heads(g_ts[ci], s, e)
        dec = jnp.exp(jnp.where(gm_ref[GM_INCL, s:e, s:e] > 0.5, diff, NEG))
        decs.append(dec)
        nmats.append(_col_of_heads(bc, s, e) * (dec * gm_ref[GM_STRICT, s:e, s:e]) * kk)
    invs = [gm_ref[GM_EYE, s:e, s:e] - nm * gm_ref[GM_LEVEL0, s:e, s:e] for nm, (_, s, e) in zip(nmats, items)]
    for lvl in range(1, (min(n_real, GDN_CHUNK) - 1).bit_length()):
        right = [_dot(nm * gm_ref[GM_LEVEL0 + lvl, s:e, s:e], inv) for nm, inv, (_, s, e) in zip(nmats, invs, items)]
        invs = [inv - _dot(inv, r) for inv, r in zip(invs, right)]
    w_st, uv_st = [[] for _ in chunks], [[] for _ in chunks]
    for inv, (ci, s, e) in zip(invs, items):
        _, kc, vc, bc, gc = chunks[ci]
        bdm_rows = gm_ref[GM_BLOCK, s:e, :]
        wu = _dot(inv, jnp.concatenate([_stack_heads(bc * jnp.exp(gc) * kc, bdm_rows),
                                        _stack_heads(bc * vc, bdm_rows)], axis=1))
        w_st[ci].append(wu[:, 0:BW])
        uv_st[ci].append(wu[:, BW:2 * BW])
    n_sb = len(GDN_SUPER)
    return [(_unstack_heads(w_st[ci]), _unstack_heads(uv_st[ci]),
             [qks[ci * n_sb + j] * decs[ci * n_sb + j] for j in range(n_sb)]) for ci in range(len(chunks))]


def _gdn_apply(chunk, prepared, hbd, gm_ref):
    qc, kc, _, _, gc = chunk
    w, uv, a_mats = prepared
    g_last = gc[GDN_CHUNK - 1:GDN_CHUNK, :]
    u = uv - _dot(w, hbd)
    o_st = [_dot(a_mats[i], _stack_heads(u, gm_ref[GM_BLOCK, s:e, :])) for i, (s, e) in enumerate(GDN_SUPER)]
    o = jnp.exp(gc) * _dot(qc, hbd) + _unstack_heads(o_st)
    h_new = jnp.exp(g_last) * hbd + _dot_tn(kc * jnp.exp(g_last - gc), u) * gm_ref[GM_BLOCK]
    return o, h_new


def _gdn_kernel(q_ref, k_ref, v_ref, g_ref, misc_ref, cs0_ref, cw_ref, alog_ref, dtb_ref, ng_ref, h0_ref,
                ea_ref, eb_ref, gm_ref, tri_ref, amat_ref, rep_ref, rept_ref, y_ref, ho_ref, cs_ref,
                xp_sc, h_sc, *, rows):
    n = q_ref.shape[0]
    cw = 3 * BW

    @pl.when(pl.program_id(1) == 0)
    def _():
        xp_sc[...] = jnp.zeros(xp_sc.shape, F32)
        xp_sc[5:8, :] = cs0_ref[0]
        h_sc[...] = _expand_state(h0_ref[0], rep_ref[...], gm_ref[GM_BLOCK])

    for j, r in enumerate((q_ref, k_ref, v_ref)):
        xp_sc[8:8 + n, j * BW:(j + 1) * BW] = r[...]
    conv = xp_sc[5:5 + rows, :] * cw_ref[0:1, :]
    for i in range(1, GDN_CONV):
        conv = conv + xp_sc[5 + i:5 + i + rows, :] * cw_ref[i:i + 1, :]
    tail = xp_sc[8 + n - 3:8 + n, :]
    xp_sc[5:8, :] = tail
    cs_ref[0] = tail
    xc = _silu(conv)
    valid = _row_iota((rows, BW)) < n
    bdm_bf = _bf(gm_ref[GM_BLOCK])
    q, k, v = xc[:, 0:BW], xc[:, BW:2 * BW], xc[:, 2 * BW:cw]
    q = q * lax.rsqrt(_dot2(q * q, bdm_bf) + EPS) * GDN_DK ** -0.5
    k = k * lax.rsqrt(_dot2(k * k, bdm_bf) + EPS)
    misc = _pad_rows(misc_ref[...], rows)
    beta = _sigmoid(_dot2(misc, eb_ref[...]))
    la = -jnp.exp(alog_ref[...]) * _softplus(_dot2(misc, ea_ref[...]) + dtb_ref[...])
    k = jnp.where(valid, k, 0.0)
    v = jnp.where(valid, v, 0.0)
    la = jnp.where(valid, la, 0.0)
    la_hi, la_lo = _split(la)
    tri = tri_ref[...]
    gall = (jnp.dot(tri, la_hi, preferred_element_type=F32)
            + jnp.dot(tri, la_lo, preferred_element_type=F32))
    c = GDN_CHUNK
    chunks = [(q[r], k[r], v[r], beta[r], gall[r]) for r in (slice(i, i + c) for i in range(0, rows, c))]
    prepared = _gdn_prepare(chunks, gm_ref, n)
    outs = []
    hbd = h_sc[...]
    for chunk, prep in zip(chunks, prepared):
        o, hbd = _gdn_apply(chunk, prep, hbd, gm_ref)
        outs.append(o)
    h_sc[...] = hbd

    @pl.when(pl.program_id(1) == pl.num_programs(1) - 1)
    def _():
        ho_ref[0] = _collapse_state(hbd, rept_ref[...])

    o = outs[0] if len(outs) == 1 else jnp.concatenate(outs, axis=0)
    on = o * lax.rsqrt(_dot2(o * o, amat_ref[...]) + EPS) * ng_ref[...]
    y_ref[...] = on[0:n] * _silu(g_ref[...])


def _gdn_consts(rows):
    lanes = np.arange(BW) // 64
    ea = np.zeros((LANE, BW), np.float32)
    eb = np.zeros((LANE, BW), np.float32)
    ea[MISC_DA + lanes, np.arange(BW)] = 1.0
    eb[MISC_DB + lanes, np.arange(BW)] = 1.0
    i = np.arange(rows)
    tri = ((i[:, None] // GDN_CHUNK == i[None, :] // GDN_CHUNK) & (i[:, None] >= i[None, :])).astype(np.float32)
    bdm = _head_block_mask()
    r = np.arange(BW)
    ri, ci = r[:, None] % GDN_CHUNK, r[None, :] % GDN_CHUNK
    gm = [bdm, bdm * (ri >= ci), bdm * (ri > ci), np.eye(BW, dtype=np.float32)]
    s = 1
    while s < GDN_CHUNK:
        gm.append(bdm * ((ri // (2 * s)) == (ci // (2 * s))) * ((ri // s) % 2 == 1) * ((ci // s) % 2 == 0))
        s *= 2
    return (jnp.asarray(ea, BF16), jnp.asarray(eb, BF16), jnp.asarray(np.stack(gm), F32), jnp.asarray(tri, BF16),
            jnp.asarray(bdm / 64.0, BF16))


def gdn_branch(proj, misc, cs0, conv_w, a_log, dt_bias, norm_g, h0, bsz, seq, rows, n):
    nblk = seq // n
    ea, eb, gm, tri, amat = _gdn_consts(rows)
    rep, rept = _rep_consts()
    per_lane = lambda a, reps: jnp.repeat(a, reps).reshape(1, BW) if reps > 1 else jnp.tile(a, BW // a.shape[0]).reshape(1, BW)
    col = lambda j: pl.BlockSpec((n, BW), lambda b, c, j=j: (b * nblk + c, j))
    const = lambda shape: pl.BlockSpec(shape, lambda b, c: (0,) * len(shape))
    per_b = lambda shape: pl.BlockSpec(shape, lambda b, c: (b,) + (0,) * (len(shape) - 1))
    cw = 3 * BW
    return pl.pallas_call(
        functools.partial(_gdn_kernel, rows=rows),
        grid=(bsz, nblk),
        in_specs=[col(4), col(5), col(6), col(7),
                  pl.BlockSpec((n, LANE), lambda b, c: (b * nblk + c, 0)),
                  per_b((1, GDN_CONV - 1, cw)), const((GDN_CONV, cw)),
                  const((1, BW)), const((1, BW)), const((1, BW)), per_b((1, BW, GDN_DK)),
                  const((LANE, BW)), const((LANE, BW)), const(tuple(gm.shape)), const((rows, rows)), const((BW, BW)),
                  const((GDN_DK, BW)), const((BW, GDN_DK))],
        out_specs=[pl.BlockSpec((n, BW), lambda b, c: (b * nblk + c, 0)),
                   per_b((1, BW, GDN_DK)), per_b((1, GDN_CONV - 1, cw))],
        out_shape=[jax.ShapeDtypeStruct((bsz * seq, BW), F32),
                   jax.ShapeDtypeStruct((bsz, BW, GDN_DK), F32),
                   jax.ShapeDtypeStruct((bsz, GDN_CONV - 1, cw), F32)],
        scratch_shapes=[pltpu.VMEM((rows + 8, cw), F32), pltpu.VMEM((BW, BW), F32)],
        compiler_params=_cp(("parallel", "arbitrary")),
        name="gdn_branch",
    )(proj, proj, proj, proj, misc, cs0, conv_w, per_lane(a_log, 64), per_lane(dt_bias, 64),
      per_lane(norm_g, 1), h0, ea, eb, gm, tri, amat, rep, rept)


_BRANCH_OFFS = (0, ATT_WIDTH, ATT_WIDTH + BW, ATT_WIDTH + 2 * BW, MIX_WIDTH)


def _merge_kernel(x_ref, g_ref, sc_ref, sh_ref, gm_ref, ya_ref, yb_ref, yc_ref, yd_ref,
                  wg_ref, wb_ref, wo_ref, o_ref):
    x = x_ref[...]
    h = _bf(_norm_mod(x, g_ref[...], sc_ref[0], sh_ref[0]))
    merged = None
    for b, y_ref in enumerate((ya_ref, yb_ref, yc_ref, yd_ref)):
        gate = _sigmoid(jnp.dot(h, wg_ref[:, b * D_MODEL:(b + 1) * D_MODEL], preferred_element_type=F32))
        term = gate * jnp.dot(_bf(y_ref[...]), wb_ref[_BRANCH_OFFS[b]:_BRANCH_OFFS[b + 1], :],
                              preferred_element_type=F32)
        merged = term if merged is None else merged + term
    y = jnp.dot(_bf(merged), wo_ref[...], preferred_element_type=F32)
    o_ref[...] = x + gm_ref[0] * y


def merge_out(x, g, sc, sh, gm, ya, yb, yc, yd, wg_bf, wb_bf, wo_bf, tm, tiles_per_seq):
    m = x.shape[0]
    ms = _mod_spec(tm, sc.shape[1], tiles_per_seq)
    row = lambda w: pl.BlockSpec((tm, w), lambda i: (i, 0))
    const = lambda shape: pl.BlockSpec(shape, lambda i: (0,) * len(shape))
    return pl.pallas_call(
        _merge_kernel,
        grid=(m // tm,),
        in_specs=[row(D_MODEL), const((1, D_MODEL)), ms, ms, ms, row(ATT_WIDTH), row(BW), row(BW), row(BW),
                  const((D_MODEL, N_BRANCH * D_MODEL)), const((MIX_WIDTH, D_MODEL)), const((D_MODEL, D_MODEL))],
        out_specs=row(D_MODEL),
        out_shape=jax.ShapeDtypeStruct((m, D_MODEL), F32),
        compiler_params=_cp(("parallel",)),
        name="merge_out",
    )(x, g.reshape(1, D_MODEL), sc, sh, gm, ya, yb, yc, yd, wg_bf, wb_bf, wo_bf)


def _top2(masked, lane):
    m1 = jnp.max(masked, axis=-1, keepdims=True)
    i1 = jnp.min(jnp.where(masked == m1, lane, LANE), axis=-1, keepdims=True)
    rest = jnp.where(lane == i1, -jnp.inf, masked)
    m2 = jnp.max(rest, axis=-1, keepdims=True)
    i2 = jnp.min(jnp.where(rest == m2, lane, LANE), axis=-1, keepdims=True)
    return m1, i1, m2, i2


def _route(scores, biased):
    lane = _lane_iota(scores.shape)
    grp = lane // EXPERTS_PER_GROUP
    best_val, best_grp = None, None
    for g in range(N_GROUPS):
        m1, _, m2, _ = _top2(jnp.where(grp == g, biased, -jnp.inf), lane)
        gs = m1 + m2
        if g == 0:
            best_val, best_grp = gs, jnp.zeros(gs.shape, jnp.int32)
        else:
            better = gs > best_val
            best_val = jnp.where(better, gs, best_val)
            best_grp = jnp.where(better, g, best_grp)
    _, e1, _, e2 = _top2(jnp.where(grp == best_grp, biased, -jnp.inf), lane)
    s1 = jnp.sum(jnp.where(lane == e1, scores, 0.0), axis=-1, keepdims=True)
    s2 = jnp.sum(jnp.where(lane == e2, scores, 0.0), axis=-1, keepdims=True)
    tot = s1 + s2
    return jnp.where(lane == e1, s1 / tot, 0.0) + jnp.where(lane == e2, s2 / tot, 0.0), best_grp


MOE_BLOCK = 128
MOE_ALIGN = 16


def _moe_kernel(x_ref, g_ref, sc_ref, sh_ref, gm_ref, wr_ref, rb_ref, tri_ref, upper_ref, w1_ref, w3_ref, w2_ref,
                fg_ref, o_ref, hs_sc, comb_sc, acc_sc, pt_sc, seg_sc, *, final):
    grp_id = pl.program_id(1)
    tm = x_ref.shape[0]

    @pl.when(grp_id == 0)
    def _():
        h = _norm_mod(x_ref[...], g_ref[...], sc_ref[0], sh_ref[0])
        scores = _sigmoid(_dot3(h, wr_ref[...]))
        comb, best = _route(scores, scores + rb_ref[...])
        lane = _lane_iota((tm, LANE))
        onehot = jnp.where(lane == best, 1.0, 0.0)
        incl = jnp.dot(tri_ref[...], _bf(onehot), preferred_element_type=F32)
        counts = incl[tm - 8:tm, :]
        offs = _dot2(counts, upper_ref[...])
        rank = jnp.sum(onehot * (offs[7:8, :] + incl), axis=-1, keepdims=True) - 1.0
        perm_t = jnp.where(_lane_iota((tm, tm)).astype(F32) == rank, 1.0, 0.0).astype(BF16)
        pt_sc[...] = perm_t
        hs_sc[0:tm, :] = _dot_tn(perm_t, _bf(h)).astype(BF16)
        hs_sc[tm:tm + MOE_BLOCK, :] = jnp.zeros((MOE_BLOCK, D_MODEL), BF16)
        ca, cb, cc = _split3(comb)
        tn = functools.partial(lax.dot_general, dimension_numbers=(((0,), (0,)), ((), ())),
                               preferred_element_type=F32)
        comb_sc[0:tm, :] = tn(perm_t, ca) + (tn(perm_t, cb) + tn(perm_t, cc))
        comb_sc[tm:tm + MOE_BLOCK, :] = jnp.zeros((MOE_BLOCK, LANE), F32)
        acc_sc[...] = jnp.zeros(acc_sc.shape, F32)
        for gi in range(N_GROUPS):
            seg_sc[gi] = offs[7, gi].astype(jnp.int32)
            seg_sc[N_GROUPS + gi] = counts[7, gi].astype(jnp.int32)

    off = seg_sc[grp_id]
    cnt = seg_sc[N_GROUPS + grp_id]
    start = (off // MOE_ALIGN) * MOE_ALIGN
    nblk = jnp.where(cnt > 0, (off + cnt - start + MOE_BLOCK - 1) // MOE_BLOCK, 0)

    def block(i, carry):
        r0 = pl.multiple_of(start + i * MOE_BLOCK, MOE_ALIGN)
        hb = hs_sc[pl.ds(r0, MOE_BLOCK), :]
        cblk = comb_sc[pl.ds(r0, MOE_BLOCK), :]
        lane = _lane_iota(cblk.shape)
        out = None
        for e in range(EXPERTS_PER_GROUP):
            ce = jnp.sum(jnp.where(lane == grp_id * EXPERTS_PER_GROUP + e, cblk, 0.0), axis=-1, keepdims=True)
            hid = (_silu(jnp.dot(hb, w1_ref[0, e], preferred_element_type=F32))
                   * jnp.dot(hb, w3_ref[0, e], preferred_element_type=F32))
            part = jnp.dot(_bf(hid * ce), w2_ref[0, e], preferred_element_type=F32)
            out = part if out is None else out + part
        acc_sc[pl.ds(r0, MOE_BLOCK), :] += out
        return carry

    lax.fori_loop(0, nblk, block, 0)

    @pl.when(grp_id == pl.num_programs(1) - 1)
    def _():
        d = functools.partial(jnp.dot, preferred_element_type=F32)
        a, b = _split(acc_sc[0:tm, :])
        perm_t = pt_sc[...]
        out = x_ref[...] + gm_ref[0] * (d(perm_t, a) + d(perm_t, b))
        if final:
            out = out * lax.rsqrt(jnp.mean(out * out, axis=-1, keepdims=True) + EPS) * fg_ref[...]
        o_ref[...] = out


def moe_out(x, g, sc, sh, gm, wr_pad, rb_pad, w1_bf, w3_bf, w2_bf, final_g, final, tm, tiles_per_seq):
    m = x.shape[0]
    ms = _mod_spec(tm, sc.shape[1], tiles_per_seq)
    const = lambda shape: pl.BlockSpec(shape, lambda i, e: (0,) * len(shape))
    r = np.arange(tm)
    tri = jnp.asarray(r[:, None] >= r[None, :], BF16)
    u = np.arange(LANE)
    upper = jnp.asarray(u[:, None] < u[None, :], BF16)
    grouped = lambda w: w.reshape((N_GROUPS, EXPERTS_PER_GROUP) + w.shape[1:])
    wspec = lambda a, b: pl.BlockSpec((1, EXPERTS_PER_GROUP, a, b), lambda i, e: (e, 0, 0, 0))
    return pl.pallas_call(
        functools.partial(_moe_kernel, final=final),
        grid=(m // tm, N_GROUPS),
        in_specs=[pl.BlockSpec((tm, D_MODEL), lambda i, e: (i, 0)), const((1, D_MODEL)), ms, ms, ms,
                  const((D_MODEL, LANE)), const((1, LANE)), const((tm, tm)), const((LANE, LANE)),
                  wspec(D_MODEL, EXPERT_FF), wspec(D_MODEL, EXPERT_FF), wspec(EXPERT_FF, D_MODEL),
                  const((1, D_MODEL))],
        out_specs=pl.BlockSpec((tm, D_MODEL), lambda i, e: (i, 0)),
        out_shape=jax.ShapeDtypeStruct((m, D_MODEL), F32),
        scratch_shapes=[pltpu.VMEM((tm + MOE_BLOCK, D_MODEL), BF16), pltpu.VMEM((tm + MOE_BLOCK, LANE), F32),
                        pltpu.VMEM((tm + MOE_BLOCK, D_MODEL), F32), pltpu.VMEM((tm, tm), BF16),
                        pltpu.SMEM((2 * N_GROUPS,), jnp.int32)],
        compiler_params=_cp(("parallel", "arbitrary")),
        name="moe_out",
    )(x, g.reshape(1, D_MODEL), sc, sh, gm, wr_pad, rb_pad, tri, upper, grouped(w1_bf), grouped(w3_bf),
      grouped(w2_bf), final_g.reshape(1, D_MODEL))


_REF_SPLITS = (ATT_WIDTH, KVW, KVW, IDX_HEADS * IDX_DIM, IDX_DIM, IDX_HEADS, BW,
               BW, BW, BW, BW, BW, BW, BW, GDN_HEADS, GDN_HEADS, BW)


def pack_w_in(w_in):
    offs = np.concatenate([[0], np.cumsum(_REF_SPLITS)])
    seg = [w_in[:, int(offs[i]):int(offs[i + 1])] for i in range(len(_REF_SPLITS))]
    (aq, ak, av, aiq, aik, aiw, bu, cq, ck, cv, cg, dq, dk, dv, da, db, dg) = seg
    zeros = lambda n: jnp.zeros((D_MODEL, n), w_in.dtype)
    misc = jnp.concatenate([aik, aiw, da, db, zeros(LANE - IDX_DIM - IDX_HEADS - 2 * GDN_HEADS)], axis=1)
    packed = jnp.concatenate([cq, ck, cv, cg, dq, dk, dv, dg, bu, aq, ak, av, aiq, zeros(2 * LANE)], axis=1)
    wt = jnp.concatenate([aiw.T, jnp.zeros((8 - IDX_HEADS, D_MODEL), w_in.dtype)], axis=0)
    return _bf(packed), misc, wt


def _time_major(a, bsz, seq):
    return a.reshape(bsz, seq, a.shape[-1]).transpose(1, 0, 2).reshape(seq * bsz, a.shape[-1])


def _batch_major(a, bsz, seq):
    return a.reshape(seq, bsz, a.shape[-1]).transpose(1, 0, 2).reshape(bsz * seq, a.shape[-1])


def _trunk_layer(x, mods, geom, attend, st, lw, final_g, final):
    bsz, seq, tm, tps, s5_tc, ret_rows, ret_n, gdn_rows, gdn_n = geom
    sh1, sc1, g1, sh2, sc2, g2 = mods
    proj, misc, iwt = in_proj(x, lw['norm1'], sc1, sh1, lw['w_in'], lw['w_misc'], lw['w_iwt'], tm, tps)
    qr, kr, v, iqr, ikr, ik4, vt = attn_prep(proj, misc, lw['tab_a'], lw['tab_i'], lw['tab_k'], tm, tps)
    ya = attend(qr, kr, v, vt, iqr, ikr, ik4, iwt, misc)
    u_tm = _time_major(proj[:, 8 * BW:9 * BW], bsz, seq)
    y_tm, s5_h = s5_branch(u_tm, st['s5'], lw['s5_abar'], lw['s5_win'], lw['s5_wout'], lw['s5_d'],
                           lw['s5_w_glu'], lw['s5_b_glu'], bsz, seq, s5_tc)
    yb = _batch_major(y_tm, bsz, seq)
    yc, ret_s = ret_branch(proj, lw['tab_r'], st['ret'], bsz, seq, ret_rows, ret_n)
    yd, gdn_s, conv_s = gdn_branch(proj, misc, st['conv'], lw['gdn_conv_w'], lw['gdn_a_log'], lw['gdn_dt_bias'],
                                   lw['gdn_norm_g'], st['gdn'], bsz, seq, gdn_rows, gdn_n)
    x = merge_out(x, lw['norm1'], sc1, sh1, g1, ya, yb, yc, yd, lw['w_gate'], lw['w_br'], lw['w_out'], tm, tps)
    x = moe_out(x, lw['norm2'], sc2, sh2, g2, lw['w_router'], lw['router_bias'], lw['w_e1'], lw['w_e3'],
                lw['w_e2'], final_g, final, tm, tps)
    new_st = {'k': kr, 'v': v, 'ik': ikr, 's5': s5_h, 'ret': ret_s, 'gdn': gdn_s, 'conv': conv_s}
    return x, new_st


def kernel(x_prompt, x_sample, c_prompt, c_sample, cache_k, cache_v, cache_idx_k, page_table,
           state_s5_re, state_s5_im, state_ret, state_gdn, state_gdn_conv,
           norm1_g, norm2_g, final_g, w_ada, b_ada, w_in,
           s5_a_re, s5_a_im, s5_b_re, s5_b_im, s5_c_re, s5_c_im, s5_d, s5_log_dt, s5_w_glu, s5_b_glu,
           gdn_conv_w, gdn_a_log, gdn_dt_bias, gdn_norm_g,
           w_br, w_gate, w_out, w_router, router_bias, w_e1, w_e3, w_e2):
    bsz, seq, _ = x_prompt.shape
    dbs, dseq, _ = x_sample.shape
    depth = w_in.shape[0]
    n_pool = cache_k.shape[1]
    past = page_table.shape[1] * PAGE_SIZE
    mp, ms = bsz * seq, dbs * dseq
    tm_p = 512
    pos_p = jnp.arange(seq, dtype=jnp.int32)
    pos_s = past + jnp.arange(dseq, dtype=jnp.int32)
    pos_s_tok = jnp.tile(pos_s, dbs)

    def tables(pos):
        return {'tab_a': _rope_tables(pos, ROT_DIMS, ROPE_THETA, HEAD_DIM),
                'tab_i': _rope_tables(pos, IDX_ROT, ROPE_THETA, IDX_DIM),
                'tab_k': _rope_tables(pos, IDX_ROT, ROPE_THETA, IDX_DIM, active=IDX_DIM)}

    tabs_p = dict(tables(pos_p), tab_r=_rope_tables(pos_p, RET_DK, RET_THETA, RET_DK))
    tabs_s = dict(tables(pos_s_tok), tab_r=_rope_tables(pos_s, RET_DK, RET_THETA, RET_DK))
    ck = cache_k.transpose(0, 1, 3, 4, 2).reshape(depth * n_pool, KVW, PAGE_SIZE)
    cv = cache_v.transpose(0, 1, 3, 4, 2).reshape(depth * n_pool, KVW, PAGE_SIZE)
    cik = cache_idx_k.transpose(0, 1, 3, 2).reshape(depth * n_pool, IDX_DIM, PAGE_SIZE)
    wr_pad = jnp.pad(w_router, ((0, 0), (0, LANE - N_EXPERTS)))
    rb_pad = jnp.pad(router_bias, (0, LANE - N_EXPERTS)).reshape(1, LANE)
    c_all = jnp.concatenate([c_prompt, c_sample], axis=0)

    geom_p = (bsz, seq, tm_p, seq // tm_p, 64, 256, 256, 256, 256)
    geom_s = (dbs, dseq, ms, 1, dseq, LANE, dseq, GDN_CHUNK, dseq)
    zero_st = {'s5': jnp.zeros((bsz, 2 * S5_NS), F32), 'ret': jnp.zeros((bsz, BW, RET_DK), F32),
               'gdn': jnp.zeros((bsz, BW, GDN_DK), F32), 'conv': jnp.zeros((bsz, GDN_CONV - 1, 3 * BW), F32)}

    xp = x_prompt.reshape(mp, D_MODEL)
    xs = x_sample.reshape(ms, D_MODEL)
    outs_p, outs_s = [], []
    for l in range(depth):
        w_in_p, w_misc, w_iwt = pack_w_in(w_in[l])
        abar8, win = s5_params(s5_log_dt[l], s5_a_re[l], s5_a_im[l], s5_b_re[l], s5_b_im[l])
        lw = {'norm1': norm1_g[l], 'norm2': norm2_g[l], 'w_in': w_in_p, 'w_misc': w_misc, 'w_iwt': w_iwt,
              's5_abar': abar8, 's5_win': win,
              's5_wout': _bf(jnp.concatenate([_block_diag_out(s5_c_re[l]), _block_diag_out(s5_c_im[l])], axis=0)),
              's5_d': s5_d[l], 's5_w_glu': _bf(s5_w_glu[l]), 's5_b_glu': s5_b_glu[l],
              'gdn_conv_w': gdn_conv_w[l], 'gdn_a_log': gdn_a_log[l], 'gdn_dt_bias': gdn_dt_bias[l],
              'gdn_norm_g': gdn_norm_g[l],
              'w_br': _bf(w_br[l]), 'w_gate': _bf(w_gate[l]), 'w_out': _bf(w_out[l]),
              'w_router': wr_pad, 'router_bias': rb_pad,
              'w_e1': _bf(w_e1[l]), 'w_e3': _bf(w_e3[l]), 'w_e2': _bf(w_e2[l])}
        mod = ada_mod(c_all, _bf(w_ada[l]), b_ada[l])
        mods = [mod[:, i * D_MODEL:(i + 1) * D_MODEL] for i in range(6)]
        mods_p = [m[:bsz].reshape(bsz, 1, D_MODEL) for m in mods]
        mods_s = [jnp.repeat(m[bsz:], dseq, axis=0).reshape(1, ms, D_MODEL) for m in mods]
        final = l == depth - 1

        def attend_p(qr, kr, v, vt, iqr, ikr, ik4, iwt, misc):
            return attn_prompt(qr, kr, vt, iqr, ik4, iwt, bsz, seq)

        def attend_s(qr, kr, v, vt, iqr, ikr, ik4, iwt, misc, l=l):
            return attn_sample(qr, kr, v, iqr, ik4, misc, ck, cv, cik, page_table, l, dseq)

        st_s = {'s5': jnp.concatenate([state_s5_re[l].reshape(dbs, S5_NS), state_s5_im[l].reshape(dbs, S5_NS)], axis=1),
                'ret': state_ret[l].reshape(dbs, BW, RET_DK), 'gdn': state_gdn[l].reshape(dbs, BW, GDN_DK),
                'conv': state_gdn_conv[l]}
        xp, ns_p = _trunk_layer(xp, mods_p, geom_p, attend_p, zero_st, dict(lw, **tabs_p), final_g, final)
        xs, ns_s = _trunk_layer(xs, mods_s, geom_s, attend_s, st_s, dict(lw, **tabs_s), final_g, final)
        outs_p.append(ns_p)
        outs_s.append(ns_s)

    def stack(outs, name, shape):
        return jnp.stack([o[name] for o in outs], axis=0).reshape((depth,) + shape)

    def states(outs, b, t):
        re = jnp.stack([o['s5'][:, :S5_NS] for o in outs], axis=0).reshape(depth, b, S5_GROUPS, S5_STATE)
        im = jnp.stack([o['s5'][:, S5_NS:] for o in outs], axis=0).reshape(depth, b, S5_GROUPS, S5_STATE)
        ret = stack(outs, 'ret', (b, RET_HEADS, RET_DK, RET_DK))
        gdn = stack(outs, 'gdn', (b, GDN_HEADS, GDN_DK, GDN_DK))
        return (stack(outs, 'k', (b, t, KV_HEADS, HEAD_DIM)), stack(outs, 'v', (b, t, KV_HEADS, HEAD_DIM)),
                stack(outs, 'ik', (b, t, IDX_DIM)), re, im, ret, gdn,
                stack(outs, 'conv', (b, GDN_CONV - 1, 3 * BW)))

    kp, vp, ikp, rep, imp, retp, gdnp, convp = states(outs_p, bsz, seq)
    ks_, vs_, iks, res, ims, rets, gdns, convs = states(outs_s, dbs, dseq)
    return (xp.reshape(bsz, seq, D_MODEL), xs.reshape(dbs, dseq, D_MODEL), kp, vp, ikp, ks_, vs_, iks,
            rep, imp, res, ims, retp, rets, gdnp, gdns, convp, convs)
```

```python
import functools
import math

import numpy as np
import jax
import jax.numpy as jnp
from jax import lax
from jax.experimental import pallas as pl
from jax.experimental.pallas import tpu as pltpu

F32 = jnp.float32
BF16 = jnp.bfloat16

D_MODEL = 1024
DEPTH = 2
PAST_LEN = 8192
PAGE_SIZE = 128
ATT_HEADS = 8
KV_HEADS = 2
HEAD_DIM = 64
ROT_DIMS = HEAD_DIM // 4
ROPE_THETA = 500000.0
IDX_HEADS = 4
IDX_DIM = 32
IDX_ROT = IDX_DIM // 4
TOPK_MAX = 256
S5_GROUPS = 24
S5_GROUP_CH = 16
S5_STATE = 64
S5_WIDTH = S5_GROUPS * S5_GROUP_CH
S5_NS = S5_GROUPS * S5_STATE
RET_HEADS = 6
RET_DK = 64
RET_THETA = 10000.0
GDN_HEADS = 6
GDN_DK = 64
GDN_CONV = 4
GDN_CHUNK = 64
ATT_WIDTH = ATT_HEADS * HEAD_DIM
BW = 384
KVW = KV_HEADS * HEAD_DIM
N_BRANCH = 4
MIX_WIDTH = ATT_WIDTH + 3 * BW
N_EXPERTS = 16
N_GROUPS = 4
EXPERTS_PER_GROUP = 4
EXPERT_FF = 256
EPS = 1e-6

PROJ_COLS = 12 * BW
ATT_COL0 = 9 * BW
S5_COL = 8
MISC_DA = 36
MISC_DB = 42
LANE = 128
VMEM_LIMIT = 56 * 1024 * 1024
NEG = -1e30


def _cp(sem):
    return pltpu.CompilerParams(dimension_semantics=sem, vmem_limit_bytes=VMEM_LIMIT)


def _bf(x):
    return x.astype(BF16)


def _dot(a, b):
    return jnp.dot(_bf(a), _bf(b), preferred_element_type=F32)


def _dot_nt(a, b):
    return lax.dot_general(_bf(a), _bf(b), (((1,), (1,)), ((), ())), preferred_element_type=F32)


def _dot_tn(a, b):
    return lax.dot_general(_bf(a), _bf(b), (((0,), (0,)), ((), ())), preferred_element_type=F32)


def _split(x):
    hi = x.astype(BF16)
    lo = (x - hi.astype(F32)).astype(BF16)
    return hi, lo


def _dot3(a, b):
    ah, al = _split(a)
    bh, bl = _split(b)
    d = functools.partial(jnp.dot, preferred_element_type=F32)
    return d(ah, bh) + (d(ah, bl) + d(al, bh))


def _dot2(a, b01):
    ah, al = _split(a)
    d = functools.partial(jnp.dot, preferred_element_type=F32)
    return d(ah, b01) + d(al, b01)


def _sigmoid(x):
    return 1.0 / (1.0 + jnp.exp(-x))


def _silu(x):
    return x * _sigmoid(x)


def _lane_iota(shape):
    return lax.broadcasted_iota(jnp.int32, shape, len(shape) - 1)


def _row_iota(shape):
    return lax.broadcasted_iota(jnp.int32, shape, len(shape) - 2)


def _ada_kernel(c_ref, w_ref, b_ref, o_ref):
    o_ref[...] = _dot(_silu(c_ref[...]), w_ref[...]) + b_ref[...]


def ada_mod(c, w_bf, b):
    n = c.shape[0]
    cols = w_bf.shape[1]
    tn = 1024
    return pl.pallas_call(
        _ada_kernel,
        grid=(cols // tn,),
        in_specs=[pl.BlockSpec((n, D_MODEL), lambda j: (0, 0)),
                  pl.BlockSpec((D_MODEL, tn), lambda j: (0, j)),
                  pl.BlockSpec((1, tn), lambda j: (0, j))],
        out_specs=pl.BlockSpec((n, tn), lambda j: (0, j)),
        out_shape=jax.ShapeDtypeStruct((n, cols), F32),
        compiler_params=_cp(("parallel",)),
        name="ada_mod",
    )(c, w_bf, b.reshape(1, cols))


def _norm_mod(x, g, sc, sh):
    y = x * lax.rsqrt(jnp.mean(x * x, axis=-1, keepdims=True) + EPS) * g
    return y * (1.0 + sc) + sh


def _in_kernel(x_ref, g_ref, sc_ref, sh_ref, w_ref, wm_ref, wt_ref, o_ref, om_ref, ot_ref, ou_ref, h_sc):
    @pl.when(pl.program_id(1) == 0)
    def _():
        h = _norm_mod(x_ref[...], g_ref[...], sc_ref[0], sh_ref[0])
        h_sc[...] = h.astype(BF16)
        hh, hl = _split(h)
        wmh, wml = _split(wm_ref[...])
        d = functools.partial(jnp.dot, preferred_element_type=F32)
        om_ref[...] = d(hh, wmh) + (d(hh, wml) + d(hl, wmh))
        wth, wtl = _split(wt_ref[...])
        nt = functools.partial(lax.dot_general, dimension_numbers=(((1,), (1,)), ((), ())),
                               preferred_element_type=F32)
        ot_ref[...] = nt(wth, hh) + (nt(wth, hl) + nt(wtl, hh))

    o_ref[...] = jnp.dot(h_sc[...], w_ref[...], preferred_element_type=F32)
    ou_ref[...] = o_ref[:, S5_COL * BW:(S5_COL + 1) * BW]


def _seq_major_spec(tm, tiles_per_seq):
    return pl.BlockSpec((tm, BW), lambda i, *_: (i % tiles_per_seq, i // tiles_per_seq))


def _mod_spec(tm, mod_rows, tiles_per_seq):
    if mod_rows == 1:
        return pl.BlockSpec((1, 1, D_MODEL), lambda i, *_: (i // tiles_per_seq, 0, 0))
    return pl.BlockSpec((1, tm, D_MODEL), lambda i, *_: (i, 0, 0))


def in_proj(x, g, sc, sh, w_bf, w_misc, w_iwt, tm, tiles_per_seq):
    m = x.shape[0]
    tn = PROJ_COLS
    mod_rows = sc.shape[1]
    ms = _mod_spec(tm, mod_rows, tiles_per_seq)
    return pl.pallas_call(
        _in_kernel,
        grid=(m // tm, PROJ_COLS // tn),
        in_specs=[pl.BlockSpec((tm, D_MODEL), lambda i, j: (i, 0)),
                  pl.BlockSpec((1, D_MODEL), lambda i, j: (0, 0)),
                  ms, ms,
                  pl.BlockSpec((D_MODEL, tn), lambda i, j: (0, j)),
                  pl.BlockSpec((D_MODEL, LANE), lambda i, j: (0, 0)),
                  pl.BlockSpec((8, D_MODEL), lambda i, j: (0, 0))],
        out_specs=[pl.BlockSpec((tm, tn), lambda i, j: (i, j)),
                   pl.BlockSpec((tm, LANE), lambda i, j: (i, 0)),
                   pl.BlockSpec((8, tm), lambda i, j: (0, i)),
                   _seq_major_spec(tm, tiles_per_seq)],
        out_shape=[jax.ShapeDtypeStruct((m, PROJ_COLS), F32),
                   jax.ShapeDtypeStruct((m, LANE), F32),
                   jax.ShapeDtypeStruct((8, m), F32),
                   jax.ShapeDtypeStruct((tm * tiles_per_seq, m // (tm * tiles_per_seq) * BW), F32)],
        scratch_shapes=[pltpu.VMEM((tm, D_MODEL), BF16)],
        compiler_params=_cp(("parallel", "arbitrary")),
        name="in_proj",
    )(x, g.reshape(1, D_MODEL), sc, sh, w_bf, w_misc, w_iwt)


def _rope_tables(pos, rot_dims, theta, period, width=LANE, active=None):
    half = rot_dims // 2
    inv_freq = jnp.power(jnp.float32(theta), -jnp.arange(half, dtype=F32) / half)
    ang = pos.astype(F32)[:, None] * inv_freq
    cos, sin = jnp.cos(ang), jnp.sin(ang)
    t = pos.shape[0]
    c = jnp.concatenate([cos, cos, jnp.ones((t, period - rot_dims), F32)], axis=1)
    s_up = jnp.concatenate([-sin, jnp.zeros((t, period - half), F32)], axis=1)
    s_dn = jnp.concatenate([jnp.zeros((t, half), F32), sin, jnp.zeros((t, period - rot_dims), F32)], axis=1)
    reps = width // period
    tab = jnp.stack([jnp.tile(a, (1, reps)) for a in (c, s_up, s_dn)], axis=0)
    if active is not None:
        ident = jnp.stack([jnp.ones((t, width), F32), jnp.zeros((t, width), F32),
                           jnp.zeros((t, width), F32)], axis=0)
        tab = jnp.where(jnp.arange(width) < active, tab, ident)
    return tab


def _rope(x, tab_ref, half):
    w = x.shape[1]
    reps = w // LANE

    def wide(k):
        t = tab_ref[k]
        return t if reps == 1 else jnp.concatenate([t] * reps, axis=1)

    return (x * wide(0) + pltpu.roll(x, w - half, axis=1) * wide(1)
            + pltpu.roll(x, half, axis=1) * wide(2))


def _prep_kernel(p_ref, m_ref, ta_ref, ti_ref, tk_ref, q_ref, k_ref, v_ref, iq_ref, ik_ref, ik4_ref, vt_ref):
    q_ref[...] = _rope(p_ref[:, 0:ATT_WIDTH], ta_ref, ROT_DIMS // 2)
    k_ref[...] = _rope(p_ref[:, 512:640], ta_ref, ROT_DIMS // 2)
    v_ref[...] = p_ref[:, 640:768]
    vt_ref[0] = p_ref[:, 640:768].T
    iq_ref[...] = _rope(p_ref[:, 768:896], ti_ref, IDX_ROT // 2)
    ikr = _rope(m_ref[...], tk_ref, IDX_ROT // 2)
    ik_ref[...] = ikr[:, 0:IDX_DIM]
    m = jnp.where(_lane_iota(ikr.shape) < IDX_DIM, ikr, 0.0)
    ik4_ref[...] = (m + pltpu.roll(m, 32, axis=1)) + (pltpu.roll(m, 64, axis=1) + pltpu.roll(m, 96, axis=1))


def attn_prep(proj, misc, tab_a, tab_i, tab_k, tm, tiles_per_seq):
    m = proj.shape[0]
    tspec = pl.BlockSpec((3, tm, LANE), lambda i: (0, i % tiles_per_seq, 0))
    widths = (ATT_WIDTH, KVW, KVW, LANE, IDX_DIM, LANE)
    seq = tm * tiles_per_seq
    return pl.pallas_call(
        _prep_kernel,
        grid=(m // tm,),
        in_specs=[pl.BlockSpec((tm, 3 * BW), lambda i: (i, ATT_COL0 // (3 * BW))),
                  pl.BlockSpec((tm, LANE), lambda i: (i, 0)), tspec, tspec, tspec],
        out_specs=[pl.BlockSpec((tm, w), lambda i: (i, 0)) for w in widths]
        + [pl.BlockSpec((1, KVW, tm), lambda i: (i // tiles_per_seq, 0, i % tiles_per_seq))],
        out_shape=[jax.ShapeDtypeStruct((m, w), F32) for w in widths]
        + [jax.ShapeDtypeStruct((m // seq, KVW, seq), F32)],
        compiler_params=_cp(("parallel",)),
        name="attn_prep",
    )(proj, misc, tab_a, tab_i, tab_k)


BISECT_MAX_ITERS = 48
BISECT_UNROLL = 4
FAR = 2.0 ** 126


def _count(ones, axis):
    return jnp.sum(ones, axis=axis, keepdims=True)


COUNT_ROWS = 32
COUNT_ACCS = 4


def _count_where(s_ref, pred, axis):
    n = s_ref.shape[0]
    if axis != 0 or n % (COUNT_ROWS * COUNT_ACCS) != 0:
        return _count(pred(s_ref[...]), axis)
    accs = [None] * COUNT_ACCS
    for j, i in enumerate(range(0, n, COUNT_ROWS)):
        part = pred(s_ref[i:i + COUNT_ROWS, :])
        a = j % COUNT_ACCS
        accs[a] = part if accs[a] is None else accs[a] + part
    return jnp.sum((accs[0] + accs[1]) + (accs[2] + accs[3]), axis=0, keepdims=True)


def _bisect_topk(s_ref, axis, topk):
    kshape = tuple(1 if a == axis else n for a, n in enumerate(s_ref.shape))
    s = s_ref[...]
    lo0 = jnp.min(jnp.where(s > -FAR, s, FAR), axis=axis, keepdims=True)
    mx = jnp.max(s, axis=axis, keepdims=True)
    hi0 = mx + (jnp.abs(mx) * 2.0 ** -20 + 1e-30)
    cnt_lo0 = _count_where(s_ref, lambda t: jnp.where(t >= lo0, 1, 0), axis)
    n_zero = _count_where(s_ref, lambda t: jnp.where(t == 0.0, 1, 0), axis)

    def pending(lo, hi, cnt_lo, cnt_hi):
        only_zeros = jnp.where(lo <= 0.0, jnp.where(hi > 0.0, jnp.where(cnt_lo - cnt_hi == n_zero, 1, 0), 0), 0)
        return jnp.max(jnp.where(cnt_lo <= topk, 0, 1 - only_zeros))

    def cond(c):
        return jnp.logical_and(c[0] < BISECT_MAX_ITERS, c[1] > 0)

    def body(c):
        it, _, lo, hi, cnt_lo, cnt_hi = c
        for _ in range(BISECT_UNROLL):
            mid = 0.5 * lo + 0.5 * hi
            cm = _count_where(s_ref, lambda t: jnp.where(t >= mid, 1, 0), axis)
            ge = cm >= topk
            lo, hi = jnp.where(ge, mid, lo), jnp.where(ge, hi, mid)
            cnt_lo, cnt_hi = jnp.where(ge, cm, cnt_lo), jnp.where(ge, cnt_hi, cm)
        return it + BISECT_UNROLL, pending(lo, hi, cnt_lo, cnt_hi), lo, hi, cnt_lo, cnt_hi

    cnt_hi0 = jnp.zeros(kshape, jnp.int32)
    init = (jnp.int32(0), pending(lo0, hi0, cnt_lo0, cnt_hi0), lo0, hi0, cnt_lo0, cnt_hi0)
    _, _, lo, hi, cnt_lo, cnt_hi = lax.while_loop(cond, body, init)
    return lo, hi, cnt_lo, cnt_hi


def _topk_select(s_ref, idx, axis, n_idx_bits, p_sc, topk):
    kshape = tuple(1 if a == axis else n for a, n in enumerate(s_ref.shape))
    lo, hi, cnt_lo, cnt_hi = _bisect_topk(s_ref, axis, topk)
    need = topk - cnt_hi
    p_sc[...] = jnp.full(kshape, (1 << n_idx_bits) - 1, jnp.int32)

    @pl.when(jnp.max(cnt_lo - cnt_hi - need) > 0)
    def _():
        tied = jnp.where(s_ref[...] >= lo, jnp.where(s_ref[...] >= hi, 0, 1), 0)

        def ibody(i, p):
            cand = p + jnp.left_shift(jnp.int32(1), n_idx_bits - 1 - i)
            taken = _count(jnp.where(idx < cand, tied, 0), axis)
            return jnp.where(taken < need, cand, p)

        p_sc[...] = lax.fori_loop(0, n_idx_bits, ibody, jnp.zeros(kshape, jnp.int32))

    s = s_ref[...]
    return jnp.where(s >= hi, 1, jnp.where(s >= lo, jnp.where(idx <= p_sc[...], 1, 0), 0))


def _topk_bias_keys_major(s_ref, tri_ref, topk):
    tk, tq = s_ref.shape
    lo, hi, _, cnt_hi = _bisect_topk(s_ref, 0, topk)
    need = (topk - cnt_hi).astype(F32)
    tri = tri_ref[...]
    offset = jnp.zeros((1, tq), F32)
    parts = []
    for c in range(tk // tri.shape[0]):
        s = s_ref[c * tri.shape[0]:(c + 1) * tri.shape[0], :]
        cand = jnp.where(s >= lo, jnp.where(s >= hi, 0.0, 1.0), 0.0)
        rank = jnp.dot(tri, _bf(cand), preferred_element_type=F32) + offset
        offset = rank[tri.shape[0] - 1:tri.shape[0], :]
        parts.append(jnp.where(s >= hi, 0.0, jnp.where(cand * rank > 0.5, jnp.where(rank <= need, 0.0, NEG), NEG)))
    return jnp.concatenate(parts, axis=0)


def _group_queries(q, g):
    tiles = []
    keep = (_lane_iota((q.shape[0], LANE)) // HEAD_DIM) == g
    for hl in range(ATT_HEADS // KV_HEADS):
        h = g * (ATT_HEADS // KV_HEADS) + hl
        t = q[:, (h // 2) * LANE:(h // 2 + 1) * LANE]
        if h % 2 != g:
            t = pltpu.roll(t, HEAD_DIM, axis=1)
        tiles.append(jnp.where(keep, t, 0.0))
    return jnp.concatenate(tiles, axis=0)


def _ungroup_outputs(o_groups, tq):
    low = _lane_iota((tq, LANE)) < HEAD_DIM
    tiles = []
    for j in range(ATT_HEADS // 2):
        halves = []
        for h in (2 * j, 2 * j + 1):
            g, hl = divmod(h, ATT_HEADS // KV_HEADS)
            t = o_groups[g][hl * tq:(hl + 1) * tq]
            if h % 2 != g:
                t = pltpu.roll(t, HEAD_DIM, axis=1)
            halves.append(t)
        tiles.append(jnp.where(low, halves[0], halves[1]))
    return jnp.concatenate(tiles, axis=1)


def _masked_attention_keys_major(q, kb, vt, bias_t):
    tq = q.shape[0]
    heads = ATT_HEADS // KV_HEADS
    tk = kb.shape[0]
    bias4 = jnp.concatenate([bias_t] * heads, axis=1)
    vrow_group = _row_iota((LANE, tk)) // HEAD_DIM
    q = q * HEAD_DIM ** -0.5
    normed = []
    for g in range(KV_HEADS):
        st = _dot_nt(kb, _group_queries(q, g)) + bias4
        p = jnp.exp(_bf(st - jnp.max(st, axis=0, keepdims=True)))
        ot = jnp.dot(_bf(jnp.where(vrow_group == g, vt, 1.0)), p, preferred_element_type=F32)
        other = (1 - g) * HEAD_DIM
        normed.append(ot[g * HEAD_DIM:(g + 1) * HEAD_DIM, :] / ot[other:other + 1, :])
    tiles = []
    for j in range(ATT_HEADS // 2):
        g, hl = divmod(2 * j, heads)
        pair = jnp.concatenate([normed[g][:, hl * tq:(hl + 1) * tq], normed[g][:, (hl + 1) * tq:(hl + 2) * tq]], axis=0)
        tiles.append(pair.T)
    return jnp.concatenate(tiles, axis=1)


def _attn_prompt_kernel(q_ref, iq_ref, iwt_ref, k_ref, vt_ref, ik4_ref, tri_ref, o_ref, s_sc, *, qblk0, tq, topk):
    tk = k_ref.shape[1]
    q0 = (qblk0 + pl.program_id(1)) * tq
    iq = iq_ref[0]
    head_of_lane = _lane_iota((tq, LANE)) // IDX_DIM
    iq4 = jnp.concatenate([jnp.where(head_of_lane == h, iq, 0.0) for h in range(IDX_HEADS)], axis=0)
    lg = _dot_nt(ik4_ref[0], iq4)
    iwt = iwt_ref[...] * (IDX_HEADS ** -0.5 * IDX_DIM ** -0.5)
    score = None
    for h in range(IDX_HEADS):
        part = jnp.maximum(lg[:, h * tq:(h + 1) * tq], 0.0) * iwt[h:h + 1, :]
        score = part if score is None else score + part
    adm = _row_iota((tk, tq)) <= q0 + _lane_iota((tk, tq))
    s_sc[...] = jnp.where(adm, score, -FAR)
    bias_t = _topk_bias_keys_major(s_sc, tri_ref, topk)
    o_ref[0] = _masked_attention_keys_major(q_ref[0], _bf(k_ref[0]), vt_ref[0], bias_t)


PREFIX_ROWS = 256


def attn_prompt(qr, kr, vt, iqr, ik4, iwt, bsz, seq, n_classes=8, tq=128):
    topk = min(TOPK_MAX, seq // 4)
    nq = seq // tq
    per = max(1, nq // n_classes)
    q3 = qr.reshape(bsz, seq, ATT_WIDTH)
    iq3 = iqr.reshape(bsz, seq, LANE)
    k3, ik3 = (a.reshape(bsz, seq, LANE) for a in (kr, ik4))
    outs = []
    for c in range(nq // per):
        tk = (c + 1) * per * tq
        qb0 = c * per
        r = np.arange(math.gcd(PREFIX_ROWS, tk))
        tri = jnp.asarray(r[:, None] >= r[None, :], BF16)
        out = pl.pallas_call(
            functools.partial(_attn_prompt_kernel, qblk0=qb0, tq=tq, topk=topk),
            grid=(bsz, per),
            in_specs=[pl.BlockSpec((1, tq, ATT_WIDTH), lambda b, j, qb0=qb0: (b, qb0 + j, 0)),
                      pl.BlockSpec((1, tq, LANE), lambda b, j, qb0=qb0: (b, qb0 + j, 0)),
                      pl.BlockSpec((8, tq), lambda b, j, qb0=qb0: (0, b * nq + qb0 + j)),
                      pl.BlockSpec((1, tk, LANE), lambda b, j: (b, 0, 0)),
                      pl.BlockSpec((1, LANE, tk), lambda b, j: (b, 0, 0)),
                      pl.BlockSpec((1, tk, LANE), lambda b, j: (b, 0, 0)),
                      pl.BlockSpec(tri.shape, lambda b, j: (0, 0))],
            out_specs=pl.BlockSpec((1, tq, ATT_WIDTH), lambda b, j: (b, j, 0)),
            out_shape=jax.ShapeDtypeStruct((bsz, per * tq, ATT_WIDTH), F32),
            scratch_shapes=[pltpu.VMEM((tk, tq), F32)],
            compiler_params=_cp(("parallel", "arbitrary")),
            name=f"attn_prompt_{tk}",
        )(q3, iq3, iwt, k3, vt, ik3, tri)
        outs.append(out)
    return jnp.concatenate(outs, axis=1).reshape(bsz * seq, ATT_WIDTH)


def _attn_sample_kernel(pt_ref, q_ref, iq_ref, misc_ref, kn_ref, vn_ref, ik4n_ref, *rest, npg, topk):
    del pt_ref
    kp, vp, ikp = rest[0:npg], rest[npg:2 * npg], rest[2 * npg:3 * npg]
    o_ref, key_sc, p_sc = rest[3 * npg:]
    tq = q_ref.shape[1]
    past = npg * PAGE_SIZE
    lk = past + LANE
    k_tile = lambda j: kp[j][0] if j < npg else _pad_rows(kn_ref[0], LANE).T
    v_tile = lambda j: vp[j][0] if j < npg else _pad_rows(vn_ref[0], LANE).T
    ik_tile = lambda j: ikp[j][0] if j < npg else _pad_rows(ik4n_ref[0], LANE).T[0:IDX_DIM, :]
    tile = lambda j: slice(j * LANE, (j + 1) * LANE)
    iq = iq_ref[0]
    iqs = _bf(jnp.concatenate([iq[:, h * IDX_DIM:(h + 1) * IDX_DIM] for h in range(IDX_HEADS)], axis=0))
    misc = misc_ref[0]
    iw = jnp.concatenate([misc[:, IDX_DIM + h:IDX_DIM + h + 1] for h in range(IDX_HEADS)], axis=0)
    iw = iw * (IDX_HEADS ** -0.5 * IDX_DIM ** -0.5)
    for j in range(npg + 1):
        wl = jnp.maximum(_dot(iqs, ik_tile(j)), 0.0) * iw
        score = (wl[0:tq] + wl[tq:2 * tq]) + (wl[2 * tq:3 * tq] + wl[3 * tq:4 * tq])
        if j == npg:
            score = jnp.where(_lane_iota((tq, LANE)) <= _row_iota((tq, LANE)), score, -FAR)
        key_sc[:, tile(j)] = score
    sel = _topk_select(key_sc, _lane_iota((tq, lk)), 1, (lk - 1).bit_length(), p_sc, topk)
    bias = jnp.where(sel > 0, 0.0, NEG)
    heads = ATT_HEADS // KV_HEADS
    q = q_ref[0] * HEAD_DIM ** -0.5
    qg = _bf(jnp.concatenate([_group_queries(q, g) for g in range(KV_HEADS)], axis=0))
    s = jnp.concatenate([_dot(qg, k_tile(j)) for j in range(npg + 1)], axis=1)
    s = s + jnp.concatenate([bias] * ATT_HEADS, axis=0)
    pr = jnp.exp(s - jnp.max(s, axis=-1, keepdims=True))
    o = None
    for j in range(npg + 1):
        part = _dot_nt(pr[:, tile(j)], v_tile(j))
        o = part if o is None else o + part
    o = o / jnp.sum(pr, axis=-1, keepdims=True)
    o_ref[0] = _ungroup_outputs([o[g * heads * tq:(g + 1) * heads * tq] for g in range(KV_HEADS)], tq)


def attn_sample(qr, kr, v, iqr, ik4, misc, cache_kt, cache_vt, cache_ikt, page_table, layer, dseq):
    db, npg = page_table.shape
    n_pool = cache_kt.shape[0] // DEPTH
    lk = npg * PAGE_SIZE + LANE
    topk = min(TOPK_MAX, (npg * PAGE_SIZE + dseq) // 4)
    base = layer * n_pool
    r3 = lambda a: a.reshape(db, dseq, a.shape[-1])
    row_spec = lambda w: pl.BlockSpec((1, dseq, w), lambda b, pt: (b, 0, 0))

    def page_spec(w, j):
        return pl.BlockSpec((1, w, PAGE_SIZE), lambda b, pt, j=j: (pt[b, j] + base, 0, 0))

    in_specs = [row_spec(ATT_WIDTH), row_spec(LANE), row_spec(LANE),
                row_spec(LANE), row_spec(LANE), row_spec(LANE)]
    in_specs += [page_spec(KVW, j) for j in range(npg)]
    in_specs += [page_spec(KVW, j) for j in range(npg)]
    in_specs += [page_spec(IDX_DIM, j) for j in range(npg)]
    out = pl.pallas_call(
        functools.partial(_attn_sample_kernel, npg=npg, topk=topk),
        grid_spec=pltpu.PrefetchScalarGridSpec(
            num_scalar_prefetch=1,
            grid=(db,),
            in_specs=in_specs,
            out_specs=pl.BlockSpec((1, dseq, ATT_WIDTH), lambda b, pt: (b, 0, 0)),
            scratch_shapes=[pltpu.VMEM((dseq, lk), F32), pltpu.VMEM((dseq, 1), jnp.int32)]),
        out_shape=jax.ShapeDtypeStruct((db, dseq, ATT_WIDTH), F32),
        compiler_params=_cp(("parallel",)),
        name="attn_sample",
    )(page_table, r3(qr), r3(iqr), r3(misc), r3(kr), r3(v), r3(ik4),
      *([cache_kt] * npg), *([cache_vt] * npg), *([cache_ikt] * npg))
    return out.reshape(db * dseq, ATT_WIDTH)


def _s5_param_kernel(ldt_ref, are_ref, aim_ref, bre_ref, bim_ref, abar_ref, win_ref):
    dt = jnp.exp(ldt_ref[...])
    a_re, a_im = are_ref[...], aim_ref[...]
    mag = jnp.exp(dt * a_re)
    abar_re = mag * jnp.cos(dt * a_im)
    abar_im = mag * jnp.sin(dt * a_im)
    den = a_re * a_re + a_im * a_im
    num_re = abar_re - 1.0
    coef_re = (num_re * a_re + abar_im * a_im) / den
    coef_im = (abar_im * a_re - num_re * a_im) / den
    abar_ref[:, 0:S5_NS] = jnp.broadcast_to(abar_re, (8, S5_NS))
    abar_ref[:, S5_NS:2 * S5_NS] = jnp.broadcast_to(abar_im, (8, S5_NS))
    b_re, b_im = bre_ref[...], bim_ref[...]
    win_ref[:, 0:S5_NS] = _bf(coef_re * b_re - coef_im * b_im)
    win_ref[:, S5_NS:2 * S5_NS] = _bf(coef_re * b_im + coef_im * b_re)


def _block_diag_in(b):
    eye = jnp.eye(S5_GROUPS, dtype=b.dtype)
    return jnp.einsum('gnc,gh->gchn', b, eye).reshape(S5_WIDTH, S5_NS)


def _block_diag_out(c):
    eye = jnp.eye(S5_GROUPS, dtype=c.dtype)
    return jnp.einsum('gcn,gh->gnhc', c, eye).reshape(S5_NS, S5_WIDTH)


def s5_params(log_dt, a_re, a_im, b_re, b_im):
    per_state = lambda a: a.reshape(1, S5_NS)
    ldt = per_state(jnp.broadcast_to(log_dt[:, None], (S5_GROUPS, S5_STATE)))
    return pl.pallas_call(
        _s5_param_kernel,
        out_shape=[jax.ShapeDtypeStruct((8, 2 * S5_NS), F32),
                   jax.ShapeDtypeStruct((S5_WIDTH, 2 * S5_NS), BF16)],
        compiler_params=pltpu.CompilerParams(vmem_limit_bytes=VMEM_LIMIT),
        name="s5_params",
    )(ldt, per_state(a_re), per_state(a_im), _block_diag_in(b_re), _block_diag_in(b_im))


def _gelu_tanh(x):
    return 0.5 * x * (1.0 + jnp.tanh(math.sqrt(2.0 / math.pi) * (x + 0.044715 * (x * x * x))))


S5_LANES = 512
S5_SLAB = (LANE // S5_GROUP_CH) * S5_STATE


def _s5_kernel(u_ref, abar_ref, win_ref, h0_ref, wout_ref, d_ref, wglu_ref, bglu_ref,
               y_ref, hn_ref, s_sc, *, bsz, tc):
    c = pl.program_id(0)

    @pl.when(c == 0)
    def _():
        hn_ref[...] = h0_ref[...]

    u = u_ref[...]
    ub = _bf(u)
    for kt in range(S5_WIDTH // LANE):
        ch = slice(kt * LANE, (kt + 1) * LANE)
        for part in range(2):
            st = slice(part * S5_NS + kt * S5_SLAB, part * S5_NS + (kt + 1) * S5_SLAB)
            s_sc[:, st] = jnp.dot(ub[:, ch], win_ref[ch, st], preferred_element_type=F32)
    nchunk = S5_NS // S5_LANES
    for rg in range(bsz // 8):
        rows = slice(rg * 8, rg * 8 + 8)

        def body(t, carry):
            row0 = pl.multiple_of(t * bsz + rg * 8, 8)
            new = []
            for cc in range(nchunk):
                lre = slice(cc * S5_LANES, (cc + 1) * S5_LANES)
                lim = slice(S5_NS + cc * S5_LANES, S5_NS + (cc + 1) * S5_LANES)
                xr, xi = carry[2 * cc], carry[2 * cc + 1]
                ar, ai = abar_ref[:, lre], abar_ref[:, lim]
                nr = (ar * xr - ai * xi) + s_sc[pl.ds(row0, 8), lre]
                ni = (ar * xi + ai * xr) + s_sc[pl.ds(row0, 8), lim]
                s_sc[pl.ds(row0, 8), lre] = nr
                s_sc[pl.ds(row0, 8), lim] = ni
                new += [nr, ni]
            return tuple(new)

        init = []
        for cc in range(nchunk):
            init += [hn_ref[rows, cc * S5_LANES:(cc + 1) * S5_LANES],
                     hn_ref[rows, S5_NS + cc * S5_LANES:S5_NS + (cc + 1) * S5_LANES]]
        fin = lax.fori_loop(0, tc, body, tuple(init))
        for cc in range(nchunk):
            hn_ref[rows, cc * S5_LANES:(cc + 1) * S5_LANES] = fin[2 * cc]
            hn_ref[rows, S5_NS + cc * S5_LANES:S5_NS + (cc + 1) * S5_LANES] = fin[2 * cc + 1]

    y_tiles = []
    for kt in range(S5_WIDTH // LANE):
        ch = slice(kt * LANE, (kt + 1) * LANE)
        re = slice(kt * S5_SLAB, (kt + 1) * S5_SLAB)
        im = slice(S5_NS + kt * S5_SLAB, S5_NS + (kt + 1) * S5_SLAB)
        y_tiles.append(jnp.dot(_bf(s_sc[:, re]), wout_ref[re, ch], preferred_element_type=F32)
                       - jnp.dot(_bf(s_sc[:, im]), wout_ref[im, ch], preferred_element_type=F32))
    y = jnp.concatenate(y_tiles, axis=1) + d_ref[...] * u
    z = _gelu_tanh(y)
    y_ref[...] = z * _sigmoid(jnp.dot(_bf(z), wglu_ref[...], preferred_element_type=F32) + bglu_ref[...])


def s5_branch(u_tm, h0, abar8, win, wout_bf, d, wglu_bf, bglu, bsz, seq, tc):
    rows = tc * bsz
    const = lambda shape: pl.BlockSpec(shape, lambda c: (0,) * len(shape))
    return pl.pallas_call(
        functools.partial(_s5_kernel, bsz=bsz, tc=tc),
        grid=(seq // tc,),
        in_specs=[pl.BlockSpec((rows, S5_WIDTH), lambda c: (c, 0)),
                  const((8, 2 * S5_NS)), const((S5_WIDTH, 2 * S5_NS)), const((bsz, 2 * S5_NS)),
                  const((2 * S5_NS, S5_WIDTH)), const((1, S5_WIDTH)), const((S5_WIDTH, S5_WIDTH)),
                  const((1, S5_WIDTH))],
        out_specs=[pl.BlockSpec((rows, S5_WIDTH), lambda c: (c, 0)), const((bsz, 2 * S5_NS))],
        out_shape=[jax.ShapeDtypeStruct((seq * bsz, S5_WIDTH), F32),
                   jax.ShapeDtypeStruct((bsz, 2 * S5_NS), F32)],
        scratch_shapes=[pltpu.VMEM((rows, 2 * S5_NS), F32)],
        compiler_params=_cp(("arbitrary",)),
        name="s5_branch",
    )(u_tm, abar8, win, h0, wout_bf, d.reshape(1, S5_WIDTH), wglu_bf, bglu.reshape(1, S5_WIDTH))


def _pad_rows(x, rows):
    n = x.shape[0]
    return x if n == rows else jnp.concatenate([x, jnp.zeros((rows - n, x.shape[1]), x.dtype)], axis=0)


def _head_mean(x, amat_bf):
    return _dot2(x, amat_bf)


def _split3(x):
    a = x.astype(BF16)
    r = x - a.astype(F32)
    b = r.astype(BF16)
    return a, b, (r - b.astype(F32)).astype(BF16)


def _expand_state(tall, rep_bf, bdm):
    d = functools.partial(jnp.dot, preferred_element_type=F32)
    a, b, c = _split3(tall)
    return (d(a, rep_bf) + (d(b, rep_bf) + d(c, rep_bf))) * bdm


def _collapse_state(bd, rept_bf):
    d = functools.partial(jnp.dot, preferred_element_type=F32)
    a, b, c = _split3(bd)
    return d(a, rept_bf) + (d(b, rept_bf) + d(c, rept_bf))


def _rep_consts():
    rep = np.tile(np.eye(64, dtype=np.float32), (1, BW // 64))
    return jnp.asarray(rep, BF16), jnp.asarray(rep.T, BF16)


def _ret_kernel(q_ref, k_ref, v_ref, g_ref, tab_ref, s0_ref, dmat_ref, qdec_ref, kdec_ref, decm_ref,
                bdm_ref, amat_ref, rep_ref, rept_ref, y_ref, so_ref, s_ref, *, rows):
    n = q_ref.shape[0]

    @pl.when(pl.program_id(1) == 0)
    def _():
        s_ref[...] = _expand_state(s0_ref[0], rep_ref[...], bdm_ref[...])

    q = _pad_rows(_rope(q_ref[...], tab_ref, RET_DK // 2), rows)
    k = _pad_rows(_rope(k_ref[...], tab_ref, RET_DK // 2) * RET_DK ** -0.5, rows)
    v = _pad_rows(v_ref[...], rows)
    state = s_ref[...]
    inter = _dot(q * qdec_ref[...], state)
    lane = _lane_iota((rows, LANE))
    tiles = []
    for p in range(RET_HEADS // 2):
        lanes = slice(p * LANE, (p + 1) * LANE)
        qp, kp, vp = q[:, lanes], _bf(k[:, lanes]), v[:, lanes]
        acc = None
        for hh in range(2):
            mine = (lane < RET_DK) if hh == 0 else (lane >= RET_DK)
            s = _dot_nt(jnp.where(mine, qp, 0.0), kp) * dmat_ref[2 * p + hh]
            part = _dot(s, jnp.where(mine, vp, 0.0))
            acc = part if acc is None else acc + part
        tiles.append(acc)
    o = jnp.concatenate(tiles, axis=1) + inter
    new_state = state * decm_ref[...] + _dot_tn(k * kdec_ref[...], v) * bdm_ref[...]
    s_ref[...] = new_state

    @pl.when(pl.program_id(1) == pl.num_programs(1) - 1)
    def _():
        so_ref[0] = _collapse_state(new_state, rept_ref[...])

    amat = amat_ref[...]
    mu = _head_mean(o, amat)
    d = o - mu
    var = _head_mean(d * d, amat)
    on = d * lax.rsqrt(var + 1e-5)
    y_ref[...] = (_silu(g_ref[...]) * on[0:n]).astype(y_ref.dtype)


def _head_block_mask():
    h = np.arange(BW) // 64
    return (h[:, None] == h[None, :]).astype(np.float32)


def _ret_consts(rows, n_true):
    lg = np.log(1.0 - np.exp2(-5.0 - np.arange(RET_HEADS, dtype=np.float64)))
    i = np.arange(rows, dtype=np.float64)
    rel = i[:, None] - i[None, :]
    dmat = np.where(rel[None] >= 0, np.exp(np.minimum(rel[None], rows) * lg[:, None, None]), 0.0)
    lane_lg = np.repeat(lg, 64)[None, :]
    qdec = np.exp((i[:, None] + 1.0) * lane_lg)
    kdec = np.where(i[:, None] < n_true, np.exp((n_true - 1.0 - i[:, None]) * lane_lg), 0.0)
    bdm = _head_block_mask()
    decm = bdm * np.exp(n_true * np.repeat(lg, 64))[:, None]
    f = lambda a: jnp.asarray(a, F32)
    return f(dmat), f(qdec), f(kdec), f(decm), f(bdm), jnp.asarray(bdm / 64.0, BF16)


def ret_branch(proj, tab_r, s0, bsz, seq, rows, n):
    nch = seq // n
    dmat, qdec, kdec, decm, bdm, amat = _ret_consts(rows, n)
    rep, rept = _rep_consts()
    col = lambda j: pl.BlockSpec((n, BW), lambda b, c, j=j: (b * nch + c, j))
    const = lambda shape: pl.BlockSpec(shape, lambda b, c: (0,) * len(shape))
    return pl.pallas_call(
        functools.partial(_ret_kernel, rows=rows),
        grid=(bsz, nch),
        in_specs=[col(0), col(1), col(2), col(3),
                  pl.BlockSpec((3, n, LANE), lambda b, c: (0, c, 0)),
                  pl.BlockSpec((1, BW, RET_DK), lambda b, c: (b, 0, 0)),
                  const((RET_HEADS, rows, rows)), const((rows, BW)), const((rows, BW)),
                  const((BW, BW)), const((BW, BW)), const((BW, BW)), const((RET_DK, BW)), const((BW, RET_DK))],
        out_specs=[pl.BlockSpec((n, BW), lambda b, c: (b * nch + c, 0)),
                   pl.BlockSpec((1, BW, RET_DK), lambda b, c: (b, 0, 0))],
        out_shape=[jax.ShapeDtypeStruct((bsz * seq, BW), F32),
                   jax.ShapeDtypeStruct((bsz, BW, RET_DK), F32)],
        scratch_shapes=[pltpu.VMEM((BW, BW), F32)],
        compiler_params=_cp(("parallel", "arbitrary")),
        name="ret_branch",
    )(proj, proj, proj, proj, tab_r, s0, dmat, qdec, kdec, decm, bdm, amat, rep, rept)


def _softplus(x):
    return jnp.maximum(x, 0.0) + jnp.log(1.0 + jnp.exp(-jnp.abs(x)))


GDN_SUPER = ((0, 4 * GDN_CHUNK), (4 * GDN_CHUNK, 6 * GDN_CHUNK))
GM_BLOCK, GM_INCL, GM_STRICT, GM_EYE, GM_LEVEL0 = 0, 1, 2, 3, 4


def _stack_heads(a, bdm_rows):
    return jnp.concatenate([a] * (bdm_rows.shape[0] // GDN_CHUNK), axis=0) * bdm_rows


def _unstack_heads(parts):
    blocks = [p[i:i + GDN_CHUNK] for p in parts for i in range(0, p.shape[0], GDN_CHUNK)]
    out = blocks[0]
    for b in blocks[1:]:
        out = out + b
    return out


def _col_of_heads(a, s, e):
    return jnp.concatenate([jnp.broadcast_to(a[:, h * GDN_DK:h * GDN_DK + 1], (GDN_CHUNK, e - s))
                            for h in range(s // GDN_CHUNK, e // GDN_CHUNK)], axis=0)


def _row_of_heads(a_t, s, e):
    return jnp.concatenate([a_t[h * GDN_DK:h * GDN_DK + 1, :] for h in range(s // GDN_CHUNK, e // GDN_CHUNK)],
                           axis=1)


def _gdn_prepare(chunks, gm_ref, n_real):
    items = [(ci, s, e) for ci in range(len(chunks)) for s, e in GDN_SUPER]
    g_ts = [gc.T for _, _, _, _, gc in chunks]
    nmats, decs, qks = [], [], []
    for ci, s, e in items:
        qc, kc, _, bc, gc = chunks[ci]
        bdm_rows = gm_ref[GM_BLOCK, s:e, :]
        ks = _bf(_stack_heads(kc, bdm_rows))
        kk = _dot_nt(ks, ks)
        qks.append(_dot_nt(_stack_heads(qc, bdm_rows), ks))
        diff = _col_of_heads(gc, s, e) - _row_of_heads(g_ts[ci], s, e)
        dec = jnp.exp(jnp.where(gm_ref[GM_INCL, s:e, s:e] > 0.5, diff, NEG))
        decs.append(dec)
        nmats.append(_col_of_heads(bc, s, e) * (dec * gm_ref[GM_STRICT, s:e, s:e]) * kk)
    invs = [gm_ref[GM_EYE, s:e, s:e] - nm * gm_ref[GM_LEVEL0, s:e, s:e] for nm, (_, s, e) in zip(nmats, items)]
    for lvl in range(1, (min(n_real, GDN_CHUNK) - 1).bit_length()):
        right = [_dot(nm * gm_ref[GM_LEVEL0 + lvl, s:e, s:e], inv) for nm, inv, (_, s, e) in zip(nmats, invs, items)]
        invs = [inv - _dot(inv, r) for inv, r in zip(invs, right)]
    w_st, uv_st = [[] for _ in chunks], [[] for _ in chunks]
    for inv, (ci, s, e) in zip(invs, items):
        _, kc, vc, bc, gc = chunks[ci]
        bdm_rows = gm_ref[GM_BLOCK, s:e, :]
        wu = _dot(inv, jnp.concatenate([_stack_heads(bc * jnp.exp(gc) * kc, bdm_rows),
                                        _stack_heads(bc * vc, bdm_rows)], axis=1))
        w_st[ci].append(wu[:, 0:BW])
        uv_st[ci].append(wu[:, BW:2 * BW])
    n_sb = len(GDN_SUPER)
    return [(_unstack_heads(w_st[ci]), _unstack_heads(uv_st[ci]),
             [qks[ci * n_sb + j] * decs[ci * n_sb + j] for j in range(n_sb)]) for ci in range(len(chunks))]


def _gdn_apply(chunk, prepared, hbd, gm_ref):
    qc, kc, _, _, gc = chunk
    w, uv, a_mats = prepared
    g_last = gc[GDN_CHUNK - 1:GDN_CHUNK, :]
    u = uv - _dot(w, hbd)
    o_st = [_dot(a_mats[i], _stack_heads(u, gm_ref[GM_BLOCK, s:e, :])) for i, (s, e) in enumerate(GDN_SUPER)]
    o = jnp.exp(gc) * _dot(qc, hbd) + _unstack_heads(o_st)
    h_new = jnp.exp(g_last) * hbd + _dot_tn(kc * jnp.exp(g_last - gc), u) * gm_ref[GM_BLOCK]
    return o, h_new


def _gdn_kernel(q_ref, k_ref, v_ref, g_ref, misc_ref, cs0_ref, cw_ref, alog_ref, dtb_ref, ng_ref, h0_ref,
                ea_ref, eb_ref, gm_ref, tri_ref, amat_ref, rep_ref, rept_ref, y_ref, ho_ref, cs_ref,
                xp_sc, h_sc, *, rows):
    n = q_ref.shape[0]
    cw = 3 * BW

    @pl.when(pl.program_id(1) == 0)
    def _():
        xp_sc[...] = jnp.zeros(xp_sc.shape, F32)
        xp_sc[5:8, :] = cs0_ref[0]
        h_sc[...] = _expand_state(h0_ref[0], rep_ref[...], gm_ref[GM_BLOCK])

    for j, r in enumerate((q_ref, k_ref, v_ref)):
        xp_sc[8:8 + n, j * BW:(j + 1) * BW] = r[...]
    conv = xp_sc[5:5 + rows, :] * cw_ref[0:1, :]
    for i in range(1, GDN_CONV):
        conv = conv + xp_sc[5 + i:5 + i + rows, :] * cw_ref[i:i + 1, :]
    tail = xp_sc[8 + n - 3:8 + n, :]
    xp_sc[5:8, :] = tail
    cs_ref[0] = tail
    xc = _silu(conv)
    valid = _row_iota((rows, BW)) < n
    bdm_bf = _bf(gm_ref[GM_BLOCK])
    q, k, v = xc[:, 0:BW], xc[:, BW:2 * BW], xc[:, 2 * BW:cw]
    q = q * lax.rsqrt(_dot2(q * q, bdm_bf) + EPS) * GDN_DK ** -0.5
    k = k * lax.rsqrt(_dot2(k * k, bdm_bf) + EPS)
    misc = _pad_rows(misc_ref[...], rows)
    beta = _sigmoid(_dot2(misc, eb_ref[...]))
    la = -jnp.exp(alog_ref[...]) * _softplus(_dot2(misc, ea_ref[...]) + dtb_ref[...])
    k = jnp.where(valid, k, 0.0)
    v = jnp.where(valid, v, 0.0)
    la = jnp.where(valid, la, 0.0)
    la_hi, la_lo = _split(la)
    tri = tri_ref[...]
    gall = (jnp.dot(tri, la_hi, preferred_element_type=F32)
            + jnp.dot(tri, la_lo, preferred_element_type=F32))
    c = GDN_CHUNK
    chunks = [(q[r], k[r], v[r], beta[r], gall[r]) for r in (slice(i, i + c) for i in range(0, rows, c))]
    prepared = _gdn_prepare(chunks, gm_ref, n)
    outs = []
    hbd = h_sc[...]
    for chunk, prep in zip(chunks, prepared):
        o, hbd = _gdn_apply(chunk, prep, hbd, gm_ref)
        outs.append(o)
    h_sc[...] = hbd

    @pl.when(pl.program_id(1) == pl.num_programs(1) - 1)
    def _():
        ho_ref[0] = _collapse_state(hbd, rept_ref[...])

    o = outs[0] if len(outs) == 1 else jnp.concatenate(outs, axis=0)
    on = o * lax.rsqrt(_dot2(o * o, amat_ref[...]) + EPS) * ng_ref[...]
    y_ref[...] = on[0:n] * _silu(g_ref[...])


def _gdn_consts(rows):
    lanes = np.arange(BW) // 64
    ea = np.zeros((LANE, BW), np.float32)
    eb = np.zeros((LANE, BW), np.float32)
    ea[MISC_DA + lanes, np.arange(BW)] = 1.0
    eb[MISC_DB + lanes, np.arange(BW)] = 1.0
    i = np.arange(rows)
    tri = ((i[:, None] // GDN_CHUNK == i[None, :] // GDN_CHUNK) & (i[:, None] >= i[None, :])).astype(np.float32)
    bdm = _head_block_mask()
    r = np.arange(BW)
    ri, ci = r[:, None] % GDN_CHUNK, r[None, :] % GDN_CHUNK
    gm = [bdm, bdm * (ri >= ci), bdm * (ri > ci), np.eye(BW, dtype=np.float32)]
    s = 1
    while s < GDN_CHUNK:
        gm.append(bdm * ((ri // (2 * s)) == (ci // (2 * s))) * ((ri // s) % 2 == 1) * ((ci // s) % 2 == 0))
        s *= 2
    return (jnp.asarray(ea, BF16), jnp.asarray(eb, BF16), jnp.asarray(np.stack(gm), F32), jnp.asarray(tri, BF16),
            jnp.asarray(bdm / 64.0, BF16))


def gdn_branch(proj, misc, cs0, conv_w, a_log, dt_bias, norm_g, h0, bsz, seq, rows, n):
    nblk = seq // n
    ea, eb, gm, tri, amat = _gdn_consts(rows)
    rep, rept = _rep_consts()
    per_lane = lambda a, reps: jnp.repeat(a, reps).reshape(1, BW) if reps > 1 else jnp.tile(a, BW // a.shape[0]).reshape(1, BW)
    col = lambda j: pl.BlockSpec((n, BW), lambda b, c, j=j: (b * nblk + c, j))
    const = lambda shape: pl.BlockSpec(shape, lambda b, c: (0,) * len(shape))
    per_b = lambda shape: pl.BlockSpec(shape, lambda b, c: (b,) + (0,) * (len(shape) - 1))
    cw = 3 * BW
    return pl.pallas_call(
        functools.partial(_gdn_kernel, rows=rows),
        grid=(bsz, nblk),
        in_specs=[col(4), col(5), col(6), col(7),
                  pl.BlockSpec((n, LANE), lambda b, c: (b * nblk + c, 0)),
                  per_b((1, GDN_CONV - 1, cw)), const((GDN_CONV, cw)),
                  const((1, BW)), const((1, BW)), const((1, BW)), per_b((1, BW, GDN_DK)),
                  const((LANE, BW)), const((LANE, BW)), const(tuple(gm.shape)), const((rows, rows)), const((BW, BW)),
                  const((GDN_DK, BW)), const((BW, GDN_DK))],
        out_specs=[pl.BlockSpec((n, BW), lambda b, c: (b * nblk + c, 0)),
                   per_b((1, BW, GDN_DK)), per_b((1, GDN_CONV - 1, cw))],
        out_shape=[jax.ShapeDtypeStruct((bsz * seq, BW), F32),
                   jax.ShapeDtypeStruct((bsz, BW, GDN_DK), F32),
                   jax.ShapeDtypeStruct((bsz, GDN_CONV - 1, cw), F32)],
        scratch_shapes=[pltpu.VMEM((rows + 8, cw), F32), pltpu.VMEM((BW, BW), F32)],
        compiler_params=_cp(("parallel", "arbitrary")),
        name="gdn_branch",
    )(proj, proj, proj, proj, misc, cs0, conv_w, per_lane(a_log, 64), per_lane(dt_bias, 64),
      per_lane(norm_g, 1), h0, ea, eb, gm, tri, amat, rep, rept)


_BRANCH_OFFS = (0, ATT_WIDTH, ATT_WIDTH + BW, ATT_WIDTH + 2 * BW, MIX_WIDTH)


def _merge_kernel(x_ref, g_ref, sc_ref, sh_ref, gm_ref, ya_ref, yb_ref, yc_ref, yd_ref,
                  wg_ref, wb_ref, wo_ref, o_ref):
    x = x_ref[...]
    h = _bf(_norm_mod(x, g_ref[...], sc_ref[0], sh_ref[0]))
    merged = None
    for b, y_ref in enumerate((ya_ref, yb_ref, yc_ref, yd_ref)):
        gate = _sigmoid(jnp.dot(h, wg_ref[:, b * D_MODEL:(b + 1) * D_MODEL], preferred_element_type=F32))
        term = gate * jnp.dot(_bf(y_ref[...]), wb_ref[_BRANCH_OFFS[b]:_BRANCH_OFFS[b + 1], :],
                              preferred_element_type=F32)
        merged = term if merged is None else merged + term
    y = jnp.dot(_bf(merged), wo_ref[...], preferred_element_type=F32)
    o_ref[...] = x + gm_ref[0] * y


def merge_out(x, g, sc, sh, gm, ya, yb, yc, yd, wg_bf, wb_bf, wo_bf, tm, tiles_per_seq):
    m = x.shape[0]
    ms = _mod_spec(tm, sc.shape[1], tiles_per_seq)
    row = lambda w: pl.BlockSpec((tm, w), lambda i: (i, 0))
    const = lambda shape: pl.BlockSpec(shape, lambda i: (0,) * len(shape))
    return pl.pallas_call(
        _merge_kernel,
        grid=(m // tm,),
        in_specs=[row(D_MODEL), const((1, D_MODEL)), ms, ms, ms, row(ATT_WIDTH),
                  _seq_major_spec(tm, tiles_per_seq), row(BW), row(BW),
                  const((D_MODEL, N_BRANCH * D_MODEL)), const((MIX_WIDTH, D_MODEL)), const((D_MODEL, D_MODEL))],
        out_specs=row(D_MODEL),
        out_shape=jax.ShapeDtypeStruct((m, D_MODEL), F32),
        compiler_params=_cp(("parallel",)),
        name="merge_out",
    )(x, g.reshape(1, D_MODEL), sc, sh, gm, ya, yb, yc, yd, wg_bf, wb_bf, wo_bf)


def _top2(masked, lane):
    m1 = jnp.max(masked, axis=-1, keepdims=True)
    i1 = jnp.min(jnp.where(masked == m1, lane, LANE), axis=-1, keepdims=True)
    rest = jnp.where(lane == i1, -jnp.inf, masked)
    m2 = jnp.max(rest, axis=-1, keepdims=True)
    i2 = jnp.min(jnp.where(rest == m2, lane, LANE), axis=-1, keepdims=True)
    return m1, i1, m2, i2


def _route(scores, biased):
    lane = _lane_iota(scores.shape)
    grp = lane // EXPERTS_PER_GROUP
    best_val, best_grp = None, None
    for g in range(N_GROUPS):
        m1, _, m2, _ = _top2(jnp.where(grp == g, biased, -jnp.inf), lane)
        gs = m1 + m2
        if g == 0:
            best_val, best_grp = gs, jnp.zeros(gs.shape, jnp.int32)
        else:
            better = gs > best_val
            best_val = jnp.where(better, gs, best_val)
            best_grp = jnp.where(better, g, best_grp)
    _, e1, _, e2 = _top2(jnp.where(grp == best_grp, biased, -jnp.inf), lane)
    s1 = jnp.sum(jnp.where(lane == e1, scores, 0.0), axis=-1, keepdims=True)
    s2 = jnp.sum(jnp.where(lane == e2, scores, 0.0), axis=-1, keepdims=True)
    tot = s1 + s2
    return jnp.where(lane == e1, s1 / tot, 0.0) + jnp.where(lane == e2, s2 / tot, 0.0), best_grp


MOE_BLOCK = 128
MOE_ALIGN = 16


def _moe_kernel(x_ref, g_ref, sc_ref, sh_ref, gm_ref, wr_ref, rb_ref, tri_ref, upper_ref, w1_ref, w3_ref, w2_ref,
                fg_ref, o_ref, hs_sc, comb_sc, acc_sc, pt_sc, seg_sc, *, final):
    grp_id = pl.program_id(1)
    tm = x_ref.shape[0]

    @pl.when(grp_id == 0)
    def _():
        h = _norm_mod(x_ref[...], g_ref[...], sc_ref[0], sh_ref[0])
        scores = _sigmoid(_dot3(h, wr_ref[...]))
        comb, best = _route(scores, scores + rb_ref[...])
        lane = _lane_iota((tm, LANE))
        onehot = jnp.where(lane == best, 1.0, 0.0)
        incl = jnp.dot(tri_ref[...], _bf(onehot), preferred_element_type=F32)
        counts = incl[tm - 8:tm, :]
        offs = _dot2(counts, upper_ref[...])
        rank = jnp.sum(onehot * (offs[7:8, :] + incl), axis=-1, keepdims=True) - 1.0
        perm_t = jnp.where(_lane_iota((tm, tm)).astype(F32) == rank, 1.0, 0.0).astype(BF16)
        pt_sc[...] = perm_t
        hs_sc[0:tm, :] = _dot_tn(perm_t, _bf(h)).astype(BF16)
        hs_sc[tm:tm + MOE_BLOCK, :] = jnp.zeros((MOE_BLOCK, D_MODEL), BF16)
        ca, cb, cc = _split3(comb)
        tn = functools.partial(lax.dot_general, dimension_numbers=(((0,), (0,)), ((), ())),
                               preferred_element_type=F32)
        comb_sc[0:tm, :] = tn(perm_t, ca) + (tn(perm_t, cb) + tn(perm_t, cc))
        comb_sc[tm:tm + MOE_BLOCK, :] = jnp.zeros((MOE_BLOCK, LANE), F32)
        acc_sc[...] = jnp.zeros(acc_sc.shape, F32)
        for gi in range(N_GROUPS):
            seg_sc[gi] = offs[7, gi].astype(jnp.int32)
            seg_sc[N_GROUPS + gi] = counts[7, gi].astype(jnp.int32)

    off = seg_sc[grp_id]
    cnt = seg_sc[N_GROUPS + grp_id]
    start = (off // MOE_ALIGN) * MOE_ALIGN
    nblk = jnp.where(cnt > 0, (off + cnt - start + MOE_BLOCK - 1) // MOE_BLOCK, 0)

    def block(i, carry):
        r0 = pl.multiple_of(start + i * MOE_BLOCK, MOE_ALIGN)
        hb = hs_sc[pl.ds(r0, MOE_BLOCK), :]
        cblk = comb_sc[pl.ds(r0, MOE_BLOCK), :]
        lane = _lane_iota(cblk.shape)
        out = None
        for e in range(EXPERTS_PER_GROUP):
            ce = jnp.sum(jnp.where(lane == grp_id * EXPERTS_PER_GROUP + e, cblk, 0.0), axis=-1, keepdims=True)
            hid = (_silu(jnp.dot(hb, w1_ref[0, e], preferred_element_type=F32))
                   * jnp.dot(hb, w3_ref[0, e], preferred_element_type=F32))
            part = jnp.dot(_bf(hid * ce), w2_ref[0, e], preferred_element_type=F32)
            out = part if out is None else out + part
        acc_sc[pl.ds(r0, MOE_BLOCK), :] += out
        return carry

    lax.fori_loop(0, nblk, block, 0)

    @pl.when(grp_id == pl.num_programs(1) - 1)
    def _():
        d = functools.partial(jnp.dot, preferred_element_type=F32)
        a, b = _split(acc_sc[0:tm, :])
        perm_t = pt_sc[...]
        out = x_ref[...] + gm_ref[0] * (d(perm_t, a) + d(perm_t, b))
        if final:
            out = out * lax.rsqrt(jnp.mean(out * out, axis=-1, keepdims=True) + EPS) * fg_ref[...]
        o_ref[...] = out


def moe_out(x, g, sc, sh, gm, wr_pad, rb_pad, w1_bf, w3_bf, w2_bf, final_g, final, tm, tiles_per_seq):
    m = x.shape[0]
    ms = _mod_spec(tm, sc.shape[1], tiles_per_seq)
    const = lambda shape: pl.BlockSpec(shape, lambda i, e: (0,) * len(shape))
    r = np.arange(tm)
    tri = jnp.asarray(r[:, None] >= r[None, :], BF16)
    u = np.arange(LANE)
    upper = jnp.asarray(u[:, None] < u[None, :], BF16)
    grouped = lambda w: w.reshape((N_GROUPS, EXPERTS_PER_GROUP) + w.shape[1:])
    wspec = lambda a, b: pl.BlockSpec((1, EXPERTS_PER_GROUP, a, b), lambda i, e: (e, 0, 0, 0))
    return pl.pallas_call(
        functools.partial(_moe_kernel, final=final),
        grid=(m // tm, N_GROUPS),
        in_specs=[pl.BlockSpec((tm, D_MODEL), lambda i, e: (i, 0)), const((1, D_MODEL)), ms, ms, ms,
                  const((D_MODEL, LANE)), const((1, LANE)), const((tm, tm)), const((LANE, LANE)),
                  wspec(D_MODEL, EXPERT_FF), wspec(D_MODEL, EXPERT_FF), wspec(EXPERT_FF, D_MODEL),
                  const((1, D_MODEL))],
        out_specs=pl.BlockSpec((tm, D_MODEL), lambda i, e: (i, 0)),
        out_shape=jax.ShapeDtypeStruct((m, D_MODEL), F32),
        scratch_shapes=[pltpu.VMEM((tm + MOE_BLOCK, D_MODEL), BF16), pltpu.VMEM((tm + MOE_BLOCK, LANE), F32),
                        pltpu.VMEM((tm + MOE_BLOCK, D_MODEL), F32), pltpu.VMEM((tm, tm), BF16),
                        pltpu.SMEM((2 * N_GROUPS,), jnp.int32)],
        compiler_params=_cp(("parallel", "arbitrary")),
        name="moe_out",
    )(x, g.reshape(1, D_MODEL), sc, sh, gm, wr_pad, rb_pad, tri, upper, grouped(w1_bf), grouped(w3_bf),
      grouped(w2_bf), final_g.reshape(1, D_MODEL))


_REF_SPLITS = (ATT_WIDTH, KVW, KVW, IDX_HEADS * IDX_DIM, IDX_DIM, IDX_HEADS, BW,
               BW, BW, BW, BW, BW, BW, BW, GDN_HEADS, GDN_HEADS, BW)


def pack_w_in(w_in):
    offs = np.concatenate([[0], np.cumsum(_REF_SPLITS)])
    seg = [w_in[:, int(offs[i]):int(offs[i + 1])] for i in range(len(_REF_SPLITS))]
    (aq, ak, av, aiq, aik, aiw, bu, cq, ck, cv, cg, dq, dk, dv, da, db, dg) = seg
    zeros = lambda n: jnp.zeros((D_MODEL, n), w_in.dtype)
    misc = jnp.concatenate([aik, aiw, da, db, zeros(LANE - IDX_DIM - IDX_HEADS - 2 * GDN_HEADS)], axis=1)
    packed = jnp.concatenate([cq, ck, cv, cg, dq, dk, dv, dg, bu, aq, ak, av, aiq, zeros(2 * LANE)], axis=1)
    wt = jnp.concatenate([aiw.T, jnp.zeros((8 - IDX_HEADS, D_MODEL), w_in.dtype)], axis=0)
    return _bf(packed), misc, wt


def _time_major(a, bsz, seq):
    return a.reshape(bsz, seq, a.shape[-1]).transpose(1, 0, 2).reshape(seq * bsz, a.shape[-1])


def _batch_major(a, bsz, seq):
    return a.reshape(seq, bsz, a.shape[-1]).transpose(1, 0, 2).reshape(bsz * seq, a.shape[-1])


def _trunk_layer(x, mods, geom, attend, st, lw, final_g, final):
    bsz, seq, tm, tps, s5_tc, ret_rows, ret_n, gdn_rows, gdn_n = geom
    sh1, sc1, g1, sh2, sc2, g2 = mods
    proj, misc, iwt, u_seq = in_proj(x, lw['norm1'], sc1, sh1, lw['w_in'], lw['w_misc'], lw['w_iwt'], tm, tps)
    qr, kr, v, iqr, ikr, ik4, vt = attn_prep(proj, misc, lw['tab_a'], lw['tab_i'], lw['tab_k'], tm, tps)
    ya = attend(qr, kr, v, vt, iqr, ikr, ik4, iwt, misc)
    per_seq_tiles = tm * tps == seq
    u_tm = u_seq.reshape(seq * bsz, BW) if per_seq_tiles else _time_major(u_seq, bsz, seq)
    y_tm, s5_h = s5_branch(u_tm, st['s5'], lw['s5_abar'], lw['s5_win'], lw['s5_wout'], lw['s5_d'],
                           lw['s5_w_glu'], lw['s5_b_glu'], bsz, seq, s5_tc)
    yb = y_tm.reshape(seq, bsz * BW) if per_seq_tiles else _batch_major(y_tm, bsz, seq)
    yc, ret_s = ret_branch(proj, lw['tab_r'], st['ret'], bsz, seq, ret_rows, ret_n)
    yd, gdn_s, conv_s = gdn_branch(proj, misc, st['conv'], lw['gdn_conv_w'], lw['gdn_a_log'], lw['gdn_dt_bias'],
                                   lw['gdn_norm_g'], st['gdn'], bsz, seq, gdn_rows, gdn_n)
    x = merge_out(x, lw['norm1'], sc1, sh1, g1, ya, yb, yc, yd, lw['w_gate'], lw['w_br'], lw['w_out'], tm, tps)
    x = moe_out(x, lw['norm2'], sc2, sh2, g2, lw['w_router'], lw['router_bias'], lw['w_e1'], lw['w_e3'],
                lw['w_e2'], final_g, final, tm, tps)
    new_st = {'k': kr, 'v': v, 'ik': ikr, 's5': s5_h, 'ret': ret_s, 'gdn': gdn_s, 'conv': conv_s}
    return x, new_st


def kernel(x_prompt, x_sample, c_prompt, c_sample, cache_k, cache_v, cache_idx_k, page_table,
           state_s5_re, state_s5_im, state_ret, state_gdn, state_gdn_conv,
           norm1_g, norm2_g, final_g, w_ada, b_ada, w_in,
           s5_a_re, s5_a_im, s5_b_re, s5_b_im, s5_c_re, s5_c_im, s5_d, s5_log_dt, s5_w_glu, s5_b_glu,
           gdn_conv_w, gdn_a_log, gdn_dt_bias, gdn_norm_g,
           w_br, w_gate, w_out, w_router, router_bias, w_e1, w_e3, w_e2):
    bsz, seq, _ = x_prompt.shape
    dbs, dseq, _ = x_sample.shape
    depth = w_in.shape[0]
    n_pool = cache_k.shape[1]
    past = page_table.shape[1] * PAGE_SIZE
    mp, ms = bsz * seq, dbs * dseq
    tm_p = 512
    pos_p = jnp.arange(seq, dtype=jnp.int32)
    pos_s = past + jnp.arange(dseq, dtype=jnp.int32)
    pos_s_tok = jnp.tile(pos_s, dbs)

    def tables(pos):
        return {'tab_a': _rope_tables(pos, ROT_DIMS, ROPE_THETA, HEAD_DIM),
                'tab_i': _rope_tables(pos, IDX_ROT, ROPE_THETA, IDX_DIM),
                'tab_k': _rope_tables(pos, IDX_ROT, ROPE_THETA, IDX_DIM, active=IDX_DIM)}

    tabs_p = dict(tables(pos_p), tab_r=_rope_tables(pos_p, RET_DK, RET_THETA, RET_DK))
    tabs_s = dict(tables(pos_s_tok), tab_r=_rope_tables(pos_s, RET_DK, RET_THETA, RET_DK))
    ck = cache_k.transpose(0, 1, 3, 4, 2).reshape(depth * n_pool, KVW, PAGE_SIZE)
    cv = cache_v.transpose(0, 1, 3, 4, 2).reshape(depth * n_pool, KVW, PAGE_SIZE)
    cik = cache_idx_k.transpose(0, 1, 3, 2).reshape(depth * n_pool, IDX_DIM, PAGE_SIZE)
    wr_pad = jnp.pad(w_router, ((0, 0), (0, LANE - N_EXPERTS)))
    rb_pad = jnp.pad(router_bias, (0, LANE - N_EXPERTS)).reshape(1, LANE)
    c_all = jnp.concatenate([c_prompt, c_sample], axis=0)

    geom_p = (bsz, seq, tm_p, seq // tm_p, 64, 256, 256, 256, 256)
    geom_s = (dbs, dseq, ms, 1, dseq, LANE, dseq, GDN_CHUNK, dseq)
    zero_st = {'s5': jnp.zeros((bsz, 2 * S5_NS), F32), 'ret': jnp.zeros((bsz, BW, RET_DK), F32),
               'gdn': jnp.zeros((bsz, BW, GDN_DK), F32), 'conv': jnp.zeros((bsz, GDN_CONV - 1, 3 * BW), F32)}

    xp = x_prompt.reshape(mp, D_MODEL)
    xs = x_sample.reshape(ms, D_MODEL)
    outs_p, outs_s = [], []
    for l in range(depth):
        w_in_p, w_misc, w_iwt = pack_w_in(w_in[l])
        abar8, win = s5_params(s5_log_dt[l], s5_a_re[l], s5_a_im[l], s5_b_re[l], s5_b_im[l])
        lw = {'norm1': norm1_g[l], 'norm2': norm2_g[l], 'w_in': w_in_p, 'w_misc': w_misc, 'w_iwt': w_iwt,
              's5_abar': abar8, 's5_win': win,
              's5_wout': _bf(jnp.concatenate([_block_diag_out(s5_c_re[l]), _block_diag_out(s5_c_im[l])], axis=0)),
              's5_d': s5_d[l], 's5_w_glu': _bf(s5_w_glu[l]), 's5_b_glu': s5_b_glu[l],
              'gdn_conv_w': gdn_conv_w[l], 'gdn_a_log': gdn_a_log[l], 'gdn_dt_bias': gdn_dt_bias[l],
              'gdn_norm_g': gdn_norm_g[l],
              'w_br': _bf(w_br[l]), 'w_gate': _bf(w_gate[l]), 'w_out': _bf(w_out[l]),
              'w_router': wr_pad, 'router_bias': rb_pad,
              'w_e1': _bf(w_e1[l]), 'w_e3': _bf(w_e3[l]), 'w_e2': _bf(w_e2[l])}
        mod = ada_mod(c_all, _bf(w_ada[l]), b_ada[l])
        mods = [mod[:, i * D_MODEL:(i + 1) * D_MODEL] for i in range(6)]
        mods_p = [m[:bsz].reshape(bsz, 1, D_MODEL) for m in mods]
        mods_s = [jnp.repeat(m[bsz:], dseq, axis=0).reshape(1, ms, D_MODEL) for m in mods]
        final = l == depth - 1

        def attend_p(qr, kr, v, vt, iqr, ikr, ik4, iwt, misc):
            return attn_prompt(qr, kr, vt, iqr, ik4, iwt, bsz, seq)

        def attend_s(qr, kr, v, vt, iqr, ikr, ik4, iwt, misc, l=l):
            return attn_sample(qr, kr, v, iqr, ik4, misc, ck, cv, cik, page_table, l, dseq)

        st_s = {'s5': jnp.concatenate([state_s5_re[l].reshape(dbs, S5_NS), state_s5_im[l].reshape(dbs, S5_NS)], axis=1),
                'ret': state_ret[l].reshape(dbs, BW, RET_DK), 'gdn': state_gdn[l].reshape(dbs, BW, GDN_DK),
                'conv': state_gdn_conv[l]}
        xp, ns_p = _trunk_layer(xp, mods_p, geom_p, attend_p, zero_st, dict(lw, **tabs_p), final_g, final)
        xs, ns_s = _trunk_layer(xs, mods_s, geom_s, attend_s, st_s, dict(lw, **tabs_s), final_g, final)
        outs_p.append(ns_p)
        outs_s.append(ns_s)

    def stack(outs, name, shape):
        return jnp.stack([o[name] for o in outs], axis=0).reshape((depth,) + shape)

    def states(outs, b, t):
        re = jnp.stack([o['s5'][:, :S5_NS] for o in outs], axis=0).reshape(depth, b, S5_GROUPS, S5_STATE)
        im = jnp.stack([o['s5'][:, S5_NS:] for o in outs], axis=0).reshape(depth, b, S5_GROUPS, S5_STATE)
        ret = stack(outs, 'ret', (b, RET_HEADS, RET_DK, RET_DK))
        gdn = stack(outs, 'gdn', (b, GDN_HEADS, GDN_DK, GDN_DK))
        return (stack(outs, 'k', (b, t, KV_HEADS, HEAD_DIM)), stack(outs, 'v', (b, t, KV_HEADS, HEAD_DIM)),
                stack(outs, 'ik', (b, t, IDX_DIM)), re, im, ret, gdn,
                stack(outs, 'conv', (b, GDN_CONV - 1, 3 * BW)))

    kp, vp, ikp, rep, imp, retp, gdnp, convp = states(outs_p, bsz, seq)
    ks_, vs_, iks, res, ims, rets, gdns, convs = states(outs_s, dbs, dseq)
    return (xp.reshape(bsz, seq, D_MODEL), xs.reshape(dbs, dseq, D_MODEL), kp, vp, ikp, ks_, vs_, iks,
            rep, imp, res, ims, retp, rets, gdnp, gdns, convp, convs)
```

```python
import functools
import math

import numpy as np
import jax
import jax.numpy as jnp
from jax import lax
from jax.experimental import pallas as pl
from jax.experimental.pallas import tpu as pltpu

F32 = jnp.float32
BF16 = jnp.bfloat16

D_MODEL = 1024
DEPTH = 2
PAST_LEN = 8192
PAGE_SIZE = 128
ATT_HEADS = 8
KV_HEADS = 2
HEAD_DIM = 64
ROT_DIMS = HEAD_DIM // 4
ROPE_THETA = 500000.0
IDX_HEADS = 4
IDX_DIM = 32
IDX_ROT = IDX_DIM // 4
TOPK_MAX = 256
S5_GROUPS = 24
S5_GROUP_CH = 16
S5_STATE = 64
S5_WIDTH = S5_GROUPS * S5_GROUP_CH
S5_NS = S5_GROUPS * S5_STATE
RET_HEADS = 6
RET_DK = 64
RET_THETA = 10000.0
GDN_HEADS = 6
GDN_DK = 64
GDN_CONV = 4
GDN_CHUNK = 64
ATT_WIDTH = ATT_HEADS * HEAD_DIM
BW = 384
KVW = KV_HEADS * HEAD_DIM
N_BRANCH = 4
MIX_WIDTH = ATT_WIDTH + 3 * BW
N_EXPERTS = 16
N_GROUPS = 4
EXPERTS_PER_GROUP = 4
EXPERT_FF = 256
EPS = 1e-6

PROJ_COLS = 12 * BW
ATT_COL0 = 9 * BW
MISC_DA = 36
MISC_DB = 42
LANE = 128
VMEM_LIMIT = 56 * 1024 * 1024
NEG = -1e30


def _cp(sem):
    return pltpu.CompilerParams(dimension_semantics=sem, vmem_limit_bytes=VMEM_LIMIT)


def _bf(x):
    return x.astype(BF16)


def _dot(a, b):
    return jnp.dot(_bf(a), _bf(b), preferred_element_type=F32)


def _dot_nt(a, b):
    return lax.dot_general(_bf(a), _bf(b), (((1,), (1,)), ((), ())), preferred_element_type=F32)


def _dot_tn(a, b):
    return lax.dot_general(_bf(a), _bf(b), (((0,), (0,)), ((), ())), preferred_element_type=F32)


def _split(x):
    hi = x.astype(BF16)
    lo = (x - hi.astype(F32)).astype(BF16)
    return hi, lo


def _dot3(a, b):
    ah, al = _split(a)
    bh, bl = _split(b)
    d = functools.partial(jnp.dot, preferred_element_type=F32)
    return d(ah, bh) + (d(ah, bl) + d(al, bh))


def _dot2(a, b01):
    ah, al = _split(a)
    d = functools.partial(jnp.dot, preferred_element_type=F32)
    return d(ah, b01) + d(al, b01)


def _sigmoid(x):
    return 1.0 / (1.0 + jnp.exp(-x))


def _silu(x):
    return x * _sigmoid(x)


def _lane_iota(shape):
    return lax.broadcasted_iota(jnp.int32, shape, len(shape) - 1)


def _row_iota(shape):
    return lax.broadcasted_iota(jnp.int32, shape, len(shape) - 2)


def _ada_kernel(c_ref, w_ref, b_ref, o_ref):
    o_ref[...] = _dot(_silu(c_ref[...]), w_ref[...]) + b_ref[...]


def ada_mod(c, w_bf, b):
    n = c.shape[0]
    cols = w_bf.shape[1]
    tn = 1024
    return pl.pallas_call(
        _ada_kernel,
        grid=(cols // tn,),
        in_specs=[pl.BlockSpec((n, D_MODEL), lambda j: (0, 0)),
                  pl.BlockSpec((D_MODEL, tn), lambda j: (0, j)),
                  pl.BlockSpec((1, tn), lambda j: (0, j))],
        out_specs=pl.BlockSpec((n, tn), lambda j: (0, j)),
        out_shape=jax.ShapeDtypeStruct((n, cols), F32),
        compiler_params=_cp(("parallel",)),
        name="ada_mod",
    )(c, w_bf, b.reshape(1, cols))


def _norm_mod(x, g, sc, sh):
    y = x * lax.rsqrt(jnp.mean(x * x, axis=-1, keepdims=True) + EPS) * g
    return y * (1.0 + sc) + sh


def _in_kernel(x_ref, g_ref, sc_ref, sh_ref, w_ref, wm_ref, wt_ref, o_ref, om_ref, ot_ref, h_sc):
    @pl.when(pl.program_id(1) == 0)
    def _():
        h = _norm_mod(x_ref[...], g_ref[...], sc_ref[0], sh_ref[0])
        h_sc[...] = h.astype(BF16)
        hh, hl = _split(h)
        wmh, wml = _split(wm_ref[...])
        d = functools.partial(jnp.dot, preferred_element_type=F32)
        om_ref[...] = d(hh, wmh) + (d(hh, wml) + d(hl, wmh))
        wth, wtl = _split(wt_ref[...])
        nt = functools.partial(lax.dot_general, dimension_numbers=(((1,), (1,)), ((), ())),
                               preferred_element_type=F32)
        ot_ref[...] = nt(wth, hh) + (nt(wth, hl) + nt(wtl, hh))

    o_ref[...] = jnp.dot(h_sc[...], w_ref[...], preferred_element_type=F32)


def _mod_spec(tm, mod_rows, tiles_per_seq):
    if mod_rows == 1:
        return pl.BlockSpec((1, 1, D_MODEL), lambda i, *_: (i // tiles_per_seq, 0, 0))
    return pl.BlockSpec((1, tm, D_MODEL), lambda i, *_: (i, 0, 0))


def in_proj(x, g, sc, sh, w_bf, w_misc, w_iwt, tm, tiles_per_seq):
    m = x.shape[0]
    tn = PROJ_COLS
    mod_rows = sc.shape[1]
    ms = _mod_spec(tm, mod_rows, tiles_per_seq)
    return pl.pallas_call(
        _in_kernel,
        grid=(m // tm, PROJ_COLS // tn),
        in_specs=[pl.BlockSpec((tm, D_MODEL), lambda i, j: (i, 0)),
                  pl.BlockSpec((1, D_MODEL), lambda i, j: (0, 0)),
                  ms, ms,
                  pl.BlockSpec((D_MODEL, tn), lambda i, j: (0, j)),
                  pl.BlockSpec((D_MODEL, LANE), lambda i, j: (0, 0)),
                  pl.BlockSpec((8, D_MODEL), lambda i, j: (0, 0))],
        out_specs=[pl.BlockSpec((tm, tn), lambda i, j: (i, j)),
                   pl.BlockSpec((tm, LANE), lambda i, j: (i, 0)),
                   pl.BlockSpec((8, tm), lambda i, j: (0, i))],
        out_shape=[jax.ShapeDtypeStruct((m, PROJ_COLS), F32),
                   jax.ShapeDtypeStruct((m, LANE), F32),
                   jax.ShapeDtypeStruct((8, m), F32)],
        scratch_shapes=[pltpu.VMEM((tm, D_MODEL), BF16)],
        compiler_params=_cp(("parallel", "arbitrary")),
        name="in_proj",
    )(x, g.reshape(1, D_MODEL), sc, sh, w_bf, w_misc, w_iwt)


def _rope_tables(pos, rot_dims, theta, period, width=LANE, active=None):
    half = rot_dims // 2
    inv_freq = jnp.power(jnp.float32(theta), -jnp.arange(half, dtype=F32) / half)
    ang = pos.astype(F32)[:, None] * inv_freq
    cos, sin = jnp.cos(ang), jnp.sin(ang)
    t = pos.shape[0]
    c = jnp.concatenate([cos, cos, jnp.ones((t, period - rot_dims), F32)], axis=1)
    s_up = jnp.concatenate([-sin, jnp.zeros((t, period - half), F32)], axis=1)
    s_dn = jnp.concatenate([jnp.zeros((t, half), F32), sin, jnp.zeros((t, period - rot_dims), F32)], axis=1)
    reps = width // period
    tab = jnp.stack([jnp.tile(a, (1, reps)) for a in (c, s_up, s_dn)], axis=0)
    if active is not None:
        ident = jnp.stack([jnp.ones((t, width), F32), jnp.zeros((t, width), F32),
                           jnp.zeros((t, width), F32)], axis=0)
        tab = jnp.where(jnp.arange(width) < active, tab, ident)
    return tab


def _rope(x, tab_ref, half):
    w = x.shape[1]
    reps = w // LANE

    def wide(k):
        t = tab_ref[k]
        return t if reps == 1 else jnp.concatenate([t] * reps, axis=1)

    return (x * wide(0) + pltpu.roll(x, w - half, axis=1) * wide(1)
            + pltpu.roll(x, half, axis=1) * wide(2))


def _prep_kernel(p_ref, m_ref, ta_ref, ti_ref, tk_ref, q_ref, k_ref, v_ref, iq_ref, ik_ref, ik4_ref, vt_ref):
    q_ref[...] = _rope(p_ref[:, 0:ATT_WIDTH], ta_ref, ROT_DIMS // 2)
    k_ref[...] = _rope(p_ref[:, 512:640], ta_ref, ROT_DIMS // 2)
    v_ref[...] = p_ref[:, 640:768]
    vt_ref[0] = p_ref[:, 640:768].T
    iq_ref[...] = _rope(p_ref[:, 768:896], ti_ref, IDX_ROT // 2)
    ikr = _rope(m_ref[...], tk_ref, IDX_ROT // 2)
    ik_ref[...] = ikr[:, 0:IDX_DIM]
    m = jnp.where(_lane_iota(ikr.shape) < IDX_DIM, ikr, 0.0)
    ik4_ref[...] = (m + pltpu.roll(m, 32, axis=1)) + (pltpu.roll(m, 64, axis=1) + pltpu.roll(m, 96, axis=1))


def attn_prep(proj, misc, tab_a, tab_i, tab_k, tm, tiles_per_seq):
    m = proj.shape[0]
    tspec = pl.BlockSpec((3, tm, LANE), lambda i: (0, i % tiles_per_seq, 0))
    widths = (ATT_WIDTH, KVW, KVW, LANE, IDX_DIM, LANE)
    seq = tm * tiles_per_seq
    return pl.pallas_call(
        _prep_kernel,
        grid=(m // tm,),
        in_specs=[pl.BlockSpec((tm, 3 * BW), lambda i: (i, ATT_COL0 // (3 * BW))),
                  pl.BlockSpec((tm, LANE), lambda i: (i, 0)), tspec, tspec, tspec],
        out_specs=[pl.BlockSpec((tm, w), lambda i: (i, 0)) for w in widths]
        + [pl.BlockSpec((1, KVW, tm), lambda i: (i // tiles_per_seq, 0, i % tiles_per_seq))],
        out_shape=[jax.ShapeDtypeStruct((m, w), F32) for w in widths]
        + [jax.ShapeDtypeStruct((m // seq, KVW, seq), F32)],
        compiler_params=_cp(("parallel",)),
        name="attn_prep",
    )(proj, misc, tab_a, tab_i, tab_k)


BISECT_MAX_ITERS = 48
BISECT_UNROLL = 4
FAR = 2.0 ** 126


def _count(ones, axis):
    return jnp.sum(ones, axis=axis, keepdims=True)


COUNT_ROWS = 32
COUNT_ACCS = 4


def _count_where(s_ref, pred, axis):
    n = s_ref.shape[0]
    if axis != 0 or n % (COUNT_ROWS * COUNT_ACCS) != 0:
        return _count(pred(s_ref[...]), axis)
    accs = [None] * COUNT_ACCS
    for j, i in enumerate(range(0, n, COUNT_ROWS)):
        part = pred(s_ref[i:i + COUNT_ROWS, :])
        a = j % COUNT_ACCS
        accs[a] = part if accs[a] is None else accs[a] + part
    return jnp.sum((accs[0] + accs[1]) + (accs[2] + accs[3]), axis=0, keepdims=True)


def _bisect_topk(s_ref, axis, topk):
    kshape = tuple(1 if a == axis else n for a, n in enumerate(s_ref.shape))
    s = s_ref[...]
    lo0 = jnp.min(jnp.where(s > -FAR, s, FAR), axis=axis, keepdims=True)
    mx = jnp.max(s, axis=axis, keepdims=True)
    hi0 = mx + (jnp.abs(mx) * 2.0 ** -20 + 1e-30)
    cnt_lo0 = _count_where(s_ref, lambda t: jnp.where(t >= lo0, 1, 0), axis)
    n_zero = _count_where(s_ref, lambda t: jnp.where(t == 0.0, 1, 0), axis)

    def pending(lo, hi, cnt_lo, cnt_hi):
        only_zeros = jnp.where(lo <= 0.0, jnp.where(hi > 0.0, jnp.where(cnt_lo - cnt_hi == n_zero, 1, 0), 0), 0)
        return jnp.max(jnp.where(cnt_lo <= topk, 0, 1 - only_zeros))

    def cond(c):
        return jnp.logical_and(c[0] < BISECT_MAX_ITERS, c[1] > 0)

    def body(c):
        it, _, lo, hi, cnt_lo, cnt_hi = c
        for _ in range(BISECT_UNROLL):
            mid = 0.5 * lo + 0.5 * hi
            cm = _count_where(s_ref, lambda t: jnp.where(t >= mid, 1, 0), axis)
            ge = cm >= topk
            lo, hi = jnp.where(ge, mid, lo), jnp.where(ge, hi, mid)
            cnt_lo, cnt_hi = jnp.where(ge, cm, cnt_lo), jnp.where(ge, cnt_hi, cm)
        return it + BISECT_UNROLL, pending(lo, hi, cnt_lo, cnt_hi), lo, hi, cnt_lo, cnt_hi

    cnt_hi0 = jnp.zeros(kshape, jnp.int32)
    init = (jnp.int32(0), pending(lo0, hi0, cnt_lo0, cnt_hi0), lo0, hi0, cnt_lo0, cnt_hi0)
    _, _, lo, hi, cnt_lo, cnt_hi = lax.while_loop(cond, body, init)
    return lo, hi, cnt_lo, cnt_hi


def _topk_select(s_ref, idx, axis, n_idx_bits, p_sc, topk):
    kshape = tuple(1 if a == axis else n for a, n in enumerate(s_ref.shape))
    lo, hi, cnt_lo, cnt_hi = _bisect_topk(s_ref, axis, topk)
    need = topk - cnt_hi
    p_sc[...] = jnp.full(kshape, (1 << n_idx_bits) - 1, jnp.int32)

    @pl.when(jnp.max(cnt_lo - cnt_hi - need) > 0)
    def _():
        tied = jnp.where(s_ref[...] >= lo, jnp.where(s_ref[...] >= hi, 0, 1), 0)

        def ibody(i, p):
            cand = p + jnp.left_shift(jnp.int32(1), n_idx_bits - 1 - i)
            taken = _count(jnp.where(idx < cand, tied, 0), axis)
            return jnp.where(taken < need, cand, p)

        p_sc[...] = lax.fori_loop(0, n_idx_bits, ibody, jnp.zeros(kshape, jnp.int32))

    s = s_ref[...]
    return jnp.where(s >= hi, 1, jnp.where(s >= lo, jnp.where(idx <= p_sc[...], 1, 0), 0))


def _topk_bias_keys_major(s_ref, tri_ref, topk):
    tk, tq = s_ref.shape
    lo, hi, _, cnt_hi = _bisect_topk(s_ref, 0, topk)
    need = (topk - cnt_hi).astype(F32)
    tri = tri_ref[...]
    offset = jnp.zeros((1, tq), F32)
    parts = []
    for c in range(tk // tri.shape[0]):
        s = s_ref[c * tri.shape[0]:(c + 1) * tri.shape[0], :]
        cand = jnp.where(s >= lo, jnp.where(s >= hi, 0.0, 1.0), 0.0)
        rank = jnp.dot(tri, _bf(cand), preferred_element_type=F32) + offset
        offset = rank[tri.shape[0] - 1:tri.shape[0], :]
        parts.append(jnp.where(s >= hi, 0.0, jnp.where(cand * rank > 0.5, jnp.where(rank <= need, 0.0, NEG), NEG)))
    return jnp.concatenate(parts, axis=0)


def _group_queries(q, g):
    tiles = []
    keep = (_lane_iota((q.shape[0], LANE)) // HEAD_DIM) == g
    for hl in range(ATT_HEADS // KV_HEADS):
        h = g * (ATT_HEADS // KV_HEADS) + hl
        t = q[:, (h // 2) * LANE:(h // 2 + 1) * LANE]
        if h % 2 != g:
            t = pltpu.roll(t, HEAD_DIM, axis=1)
        tiles.append(jnp.where(keep, t, 0.0))
    return jnp.concatenate(tiles, axis=0)


def _ungroup_outputs(o_groups, tq):
    low = _lane_iota((tq, LANE)) < HEAD_DIM
    tiles = []
    for j in range(ATT_HEADS // 2):
        halves = []
        for h in (2 * j, 2 * j + 1):
            g, hl = divmod(h, ATT_HEADS // KV_HEADS)
            t = o_groups[g][hl * tq:(hl + 1) * tq]
            if h % 2 != g:
                t = pltpu.roll(t, HEAD_DIM, axis=1)
            halves.append(t)
        tiles.append(jnp.where(low, halves[0], halves[1]))
    return jnp.concatenate(tiles, axis=1)


def _masked_attention_keys_major(q, kb, vt, bias_t):
    tq = q.shape[0]
    heads = ATT_HEADS // KV_HEADS
    tk = kb.shape[0]
    bias4 = jnp.concatenate([bias_t] * heads, axis=1)
    vrow_group = _row_iota((LANE, tk)) // HEAD_DIM
    q = q * HEAD_DIM ** -0.5
    normed = []
    for g in range(KV_HEADS):
        st = _dot_nt(kb, _group_queries(q, g)) + bias4
        p = jnp.exp(_bf(st - jnp.max(st, axis=0, keepdims=True)))
        ot = jnp.dot(_bf(jnp.where(vrow_group == g, vt, 1.0)), p, preferred_element_type=F32)
        other = (1 - g) * HEAD_DIM
        normed.append(ot[g * HEAD_DIM:(g + 1) * HEAD_DIM, :] / ot[other:other + 1, :])
    tiles = []
    for j in range(ATT_HEADS // 2):
        g, hl = divmod(2 * j, heads)
        pair = jnp.concatenate([normed[g][:, hl * tq:(hl + 1) * tq], normed[g][:, (hl + 1) * tq:(hl + 2) * tq]], axis=0)
        tiles.append(pair.T)
    return jnp.concatenate(tiles, axis=1)


def _attn_prompt_kernel(q_ref, iq_ref, iwt_ref, k_ref, vt_ref, ik4_ref, tri_ref, o_ref, s_sc, *, qblk0, tq, topk):
    tk = k_ref.shape[1]
    q0 = (qblk0 + pl.program_id(1)) * tq
    iq = iq_ref[0]
    head_of_lane = _lane_iota((tq, LANE)) // IDX_DIM
    iq4 = jnp.concatenate([jnp.where(head_of_lane == h, iq, 0.0) for h in range(IDX_HEADS)], axis=0)
    lg = _dot_nt(ik4_ref[0], iq4)
    iwt = iwt_ref[...] * (IDX_HEADS ** -0.5 * IDX_DIM ** -0.5)
    score = None
    for h in range(IDX_HEADS):
        part = jnp.maximum(lg[:, h * tq:(h + 1) * tq], 0.0) * iwt[h:h + 1, :]
        score = part if score is None else score + part
    adm = _row_iota((tk, tq)) <= q0 + _lane_iota((tk, tq))
    s_sc[...] = jnp.where(adm, score, -FAR)
    bias_t = _topk_bias_keys_major(s_sc, tri_ref, topk)
    o_ref[0] = _masked_attention_keys_major(q_ref[0], _bf(k_ref[0]), vt_ref[0], bias_t)


PREFIX_ROWS = 256


def attn_prompt(qr, kr, vt, iqr, ik4, iwt, bsz, seq, n_classes=8, tq=128):
    topk = min(TOPK_MAX, seq // 4)
    nq = seq // tq
    per = max(1, nq // n_classes)
    q3 = qr.reshape(bsz, seq, ATT_WIDTH)
    iq3 = iqr.reshape(bsz, seq, LANE)
    k3, ik3 = (a.reshape(bsz, seq, LANE) for a in (kr, ik4))
    outs = []
    for c in range(nq // per):
        tk = (c + 1) * per * tq
        qb0 = c * per
        r = np.arange(math.gcd(PREFIX_ROWS, tk))
        tri = jnp.asarray(r[:, None] >= r[None, :], BF16)
        out = pl.pallas_call(
            functools.partial(_attn_prompt_kernel, qblk0=qb0, tq=tq, topk=topk),
            grid=(bsz, per),
            in_specs=[pl.BlockSpec((1, tq, ATT_WIDTH), lambda b, j, qb0=qb0: (b, qb0 + j, 0)),
                      pl.BlockSpec((1, tq, LANE), lambda b, j, qb0=qb0: (b, qb0 + j, 0)),
                      pl.BlockSpec((8, tq), lambda b, j, qb0=qb0: (0, b * nq + qb0 + j)),
                      pl.BlockSpec((1, tk, LANE), lambda b, j: (b, 0, 0)),
                      pl.BlockSpec((1, LANE, tk), lambda b, j: (b, 0, 0)),
                      pl.BlockSpec((1, tk, LANE), lambda b, j: (b, 0, 0)),
                      pl.BlockSpec(tri.shape, lambda b, j: (0, 0))],
            out_specs=pl.BlockSpec((1, tq, ATT_WIDTH), lambda b, j: (b, j, 0)),
            out_shape=jax.ShapeDtypeStruct((bsz, per * tq, ATT_WIDTH), F32),
            scratch_shapes=[pltpu.VMEM((tk, tq), F32)],
            compiler_params=_cp(("parallel", "arbitrary")),
            name=f"attn_prompt_{tk}",
        )(q3, iq3, iwt, k3, vt, ik3, tri)
        outs.append(out)
    return jnp.concatenate(outs, axis=1).reshape(bsz * seq, ATT_WIDTH)


def _attn_sample_kernel(pt_ref, q_ref, iq_ref, misc_ref, kn_ref, vn_ref, ik4n_ref, *rest, npg, topk):
    del pt_ref
    kp, vp, ikp = rest[0:npg], rest[npg:2 * npg], rest[2 * npg:3 * npg]
    o_ref, key_sc, p_sc = rest[3 * npg:]
    tq = q_ref.shape[1]
    past = npg * PAGE_SIZE
    lk = past + LANE
    k_tile = lambda j: kp[j][0] if j < npg else _pad_rows(kn_ref[0], LANE).T
    v_tile = lambda j: vp[j][0] if j < npg else _pad_rows(vn_ref[0], LANE).T
    ik_tile = lambda j: ikp[j][0] if j < npg else _pad_rows(ik4n_ref[0], LANE).T[0:IDX_DIM, :]
    tile = lambda j: slice(j * LANE, (j + 1) * LANE)
    iq = iq_ref[0]
    iqs = _bf(jnp.concatenate([iq[:, h * IDX_DIM:(h + 1) * IDX_DIM] for h in range(IDX_HEADS)], axis=0))
    misc = misc_ref[0]
    iw = jnp.concatenate([misc[:, IDX_DIM + h:IDX_DIM + h + 1] for h in range(IDX_HEADS)], axis=0)
    iw = iw * (IDX_HEADS ** -0.5 * IDX_DIM ** -0.5)
    for j in range(npg + 1):
        wl = jnp.maximum(_dot(iqs, ik_tile(j)), 0.0) * iw
        score = (wl[0:tq] + wl[tq:2 * tq]) + (wl[2 * tq:3 * tq] + wl[3 * tq:4 * tq])
        if j == npg:
            score = jnp.where(_lane_iota((tq, LANE)) <= _row_iota((tq, LANE)), score, -FAR)
        key_sc[:, tile(j)] = score
    sel = _topk_select(key_sc, _lane_iota((tq, lk)), 1, (lk - 1).bit_length(), p_sc, topk)
    bias = jnp.where(sel > 0, 0.0, NEG)
    heads = ATT_HEADS // KV_HEADS
    q = q_ref[0] * HEAD_DIM ** -0.5
    qg = _bf(jnp.concatenate([_group_queries(q, g) for g in range(KV_HEADS)], axis=0))
    s = jnp.concatenate([_dot(qg, k_tile(j)) for j in range(npg + 1)], axis=1)
    s = s + jnp.concatenate([bias] * ATT_HEADS, axis=0)
    pr = jnp.exp(s - jnp.max(s, axis=-1, keepdims=True))
    o = None
    for j in range(npg + 1):
        part = _dot_nt(pr[:, tile(j)], v_tile(j))
        o = part if o is None else o + part
    o = o / jnp.sum(pr, axis=-1, keepdims=True)
    o_ref[0] = _ungroup_outputs([o[g * heads * tq:(g + 1) * heads * tq] for g in range(KV_HEADS)], tq)


def attn_sample(qr, kr, v, iqr, ik4, misc, cache_kt, cache_vt, cache_ikt, page_table, layer, dseq):
    db, npg = page_table.shape
    n_pool = cache_kt.shape[0] // DEPTH
    lk = npg * PAGE_SIZE + LANE
    topk = min(TOPK_MAX, (npg * PAGE_SIZE + dseq) // 4)
    base = layer * n_pool
    r3 = lambda a: a.reshape(db, dseq, a.shape[-1])
    row_spec = lambda w: pl.BlockSpec((1, dseq, w), lambda b, pt: (b, 0, 0))

    def page_spec(w, j):
        return pl.BlockSpec((1, w, PAGE_SIZE), lambda b, pt, j=j: (pt[b, j] + base, 0, 0))

    in_specs = [row_spec(ATT_WIDTH), row_spec(LANE), row_spec(LANE),
                row_spec(LANE), row_spec(LANE), row_spec(LANE)]
    in_specs += [page_spec(KVW, j) for j in range(npg)]
    in_specs += [page_spec(KVW, j) for j in range(npg)]
    in_specs += [page_spec(IDX_DIM, j) for j in range(npg)]
    out = pl.pallas_call(
        functools.partial(_attn_sample_kernel, npg=npg, topk=topk),
        grid_spec=pltpu.PrefetchScalarGridSpec(
            num_scalar_prefetch=1,
            grid=(db,),
            in_specs=in_specs,
            out_specs=pl.BlockSpec((1, dseq, ATT_WIDTH), lambda b, pt: (b, 0, 0)),
            scratch_shapes=[pltpu.VMEM((dseq, lk), F32), pltpu.VMEM((dseq, 1), jnp.int32)]),
        out_shape=jax.ShapeDtypeStruct((db, dseq, ATT_WIDTH), F32),
        compiler_params=_cp(("parallel",)),
        name="attn_sample",
    )(page_table, r3(qr), r3(iqr), r3(misc), r3(kr), r3(v), r3(ik4),
      *([cache_kt] * npg), *([cache_vt] * npg), *([cache_ikt] * npg))
    return out.reshape(db * dseq, ATT_WIDTH)


def _s5_param_kernel(ldt_ref, are_ref, aim_ref, bre_ref, bim_ref, abar_ref, win_ref):
    dt = jnp.exp(ldt_ref[...])
    a_re, a_im = are_ref[...], aim_ref[...]
    mag = jnp.exp(dt * a_re)
    abar_re = mag * jnp.cos(dt * a_im)
    abar_im = mag * jnp.sin(dt * a_im)
    den = a_re * a_re + a_im * a_im
    num_re = abar_re - 1.0
    coef_re = (num_re * a_re + abar_im * a_im) / den
    coef_im = (abar_im * a_re - num_re * a_im) / den
    abar_ref[:, 0:S5_NS] = jnp.broadcast_to(abar_re, (8, S5_NS))
    abar_ref[:, S5_NS:2 * S5_NS] = jnp.broadcast_to(abar_im, (8, S5_NS))
    b_re, b_im = bre_ref[...], bim_ref[...]
    win_ref[:, 0:S5_NS] = _bf(coef_re * b_re - coef_im * b_im)
    win_ref[:, S5_NS:2 * S5_NS] = _bf(coef_re * b_im + coef_im * b_re)


def _block_diag_in(b):
    eye = jnp.eye(S5_GROUPS, dtype=b.dtype)
    return jnp.einsum('gnc,gh->gchn', b, eye).reshape(S5_WIDTH, S5_NS)


def _block_diag_out(c):
    eye = jnp.eye(S5_GROUPS, dtype=c.dtype)
    return jnp.einsum('gcn,gh->gnhc', c, eye).reshape(S5_NS, S5_WIDTH)


def s5_params(log_dt, a_re, a_im, b_re, b_im):
    per_state = lambda a: a.reshape(1, S5_NS)
    ldt = per_state(jnp.broadcast_to(log_dt[:, None], (S5_GROUPS, S5_STATE)))
    return pl.pallas_call(
        _s5_param_kernel,
        out_shape=[jax.ShapeDtypeStruct((8, 2 * S5_NS), F32),
                   jax.ShapeDtypeStruct((S5_WIDTH, 2 * S5_NS), BF16)],
        compiler_params=pltpu.CompilerParams(vmem_limit_bytes=VMEM_LIMIT),
        name="s5_params",
    )(ldt, per_state(a_re), per_state(a_im), _block_diag_in(b_re), _block_diag_in(b_im))


def _gelu_tanh(x):
    return 0.5 * x * (1.0 + jnp.tanh(math.sqrt(2.0 / math.pi) * (x + 0.044715 * (x * x * x))))


S5_LANES = 512
S5_SLAB = (LANE // S5_GROUP_CH) * S5_STATE


def _s5_kernel(u_ref, abar_ref, win_ref, h0_ref, wout_ref, d_ref, wglu_ref, bglu_ref,
               y_ref, hn_ref, s_sc, *, bsz, tc):
    c = pl.program_id(0)

    @pl.when(c == 0)
    def _():
        hn_ref[...] = h0_ref[...]

    u = u_ref[...]
    ub = _bf(u)
    for kt in range(S5_WIDTH // LANE):
        ch = slice(kt * LANE, (kt + 1) * LANE)
        for part in range(2):
            st = slice(part * S5_NS + kt * S5_SLAB, part * S5_NS + (kt + 1) * S5_SLAB)
            s_sc[:, st] = jnp.dot(ub[:, ch], win_ref[ch, st], preferred_element_type=F32)
    nchunk = S5_NS // S5_LANES
    for rg in range(bsz // 8):
        rows = slice(rg * 8, rg * 8 + 8)

        def body(t, carry):
            row0 = pl.multiple_of(t * bsz + rg * 8, 8)
            new = []
            for cc in range(nchunk):
                lre = slice(cc * S5_LANES, (cc + 1) * S5_LANES)
                lim = slice(S5_NS + cc * S5_LANES, S5_NS + (cc + 1) * S5_LANES)
                xr, xi = carry[2 * cc], carry[2 * cc + 1]
                ar, ai = abar_ref[:, lre], abar_ref[:, lim]
                nr = (ar * xr - ai * xi) + s_sc[pl.ds(row0, 8), lre]
                ni = (ar * xi + ai * xr) + s_sc[pl.ds(row0, 8), lim]
                s_sc[pl.ds(row0, 8), lre] = nr
                s_sc[pl.ds(row0, 8), lim] = ni
                new += [nr, ni]
            return tuple(new)

        init = []
        for cc in range(nchunk):
            init += [hn_ref[rows, cc * S5_LANES:(cc + 1) * S5_LANES],
                     hn_ref[rows, S5_NS + cc * S5_LANES:S5_NS + (cc + 1) * S5_LANES]]
        fin = lax.fori_loop(0, tc, body, tuple(init))
        for cc in range(nchunk):
            hn_ref[rows, cc * S5_LANES:(cc + 1) * S5_LANES] = fin[2 * cc]
            hn_ref[rows, S5_NS + cc * S5_LANES:S5_NS + (cc + 1) * S5_LANES] = fin[2 * cc + 1]

    y_tiles = []
    for kt in range(S5_WIDTH // LANE):
        ch = slice(kt * LANE, (kt + 1) * LANE)
        re = slice(kt * S5_SLAB, (kt + 1) * S5_SLAB)
        im = slice(S5_NS + kt * S5_SLAB, S5_NS + (kt + 1) * S5_SLAB)
        y_tiles.append(jnp.dot(_bf(s_sc[:, re]), wout_ref[re, ch], preferred_element_type=F32)
                       - jnp.dot(_bf(s_sc[:, im]), wout_ref[im, ch], preferred_element_type=F32))
    y = jnp.concatenate(y_tiles, axis=1) + d_ref[...] * u
    z = _gelu_tanh(y)
    y_ref[...] = z * _sigmoid(jnp.dot(_bf(z), wglu_ref[...], preferred_element_type=F32) + bglu_ref[...])


def s5_branch(u_tm, h0, abar8, win, wout_bf, d, wglu_bf, bglu, bsz, seq, tc):
    rows = tc * bsz
    const = lambda shape: pl.BlockSpec(shape, lambda c: (0,) * len(shape))
    return pl.pallas_call(
        functools.partial(_s5_kernel, bsz=bsz, tc=tc),
        grid=(seq // tc,),
        in_specs=[pl.BlockSpec((rows, S5_WIDTH), lambda c: (c, 0)),
                  const((8, 2 * S5_NS)), const((S5_WIDTH, 2 * S5_NS)), const((bsz, 2 * S5_NS)),
                  const((2 * S5_NS, S5_WIDTH)), const((1, S5_WIDTH)), const((S5_WIDTH, S5_WIDTH)),
                  const((1, S5_WIDTH))],
        out_specs=[pl.BlockSpec((rows, S5_WIDTH), lambda c: (c, 0)), const((bsz, 2 * S5_NS))],
        out_shape=[jax.ShapeDtypeStruct((seq * bsz, S5_WIDTH), F32),
                   jax.ShapeDtypeStruct((bsz, 2 * S5_NS), F32)],
        scratch_shapes=[pltpu.VMEM((rows, 2 * S5_NS), F32)],
        compiler_params=_cp(("arbitrary",)),
        name="s5_branch",
    )(u_tm, abar8, win, h0, wout_bf, d.reshape(1, S5_WIDTH), wglu_bf, bglu.reshape(1, S5_WIDTH))


def _pad_rows(x, rows):
    n = x.shape[0]
    return x if n == rows else jnp.concatenate([x, jnp.zeros((rows - n, x.shape[1]), x.dtype)], axis=0)


def _head_mean(x, amat_bf):
    return _dot2(x, amat_bf)


def _split3(x):
    a = x.astype(BF16)
    r = x - a.astype(F32)
    b = r.astype(BF16)
    return a, b, (r - b.astype(F32)).astype(BF16)


def _expand_state(tall, rep_bf, bdm):
    d = functools.partial(jnp.dot, preferred_element_type=F32)
    a, b, c = _split3(tall)
    return (d(a, rep_bf) + (d(b, rep_bf) + d(c, rep_bf))) * bdm


def _collapse_state(bd, rept_bf):
    d = functools.partial(jnp.dot, preferred_element_type=F32)
    a, b, c = _split3(bd)
    return d(a, rept_bf) + (d(b, rept_bf) + d(c, rept_bf))


def _rep_consts():
    rep = np.tile(np.eye(64, dtype=np.float32), (1, BW // 64))
    return jnp.asarray(rep, BF16), jnp.asarray(rep.T, BF16)


def _ret_kernel(q_ref, k_ref, v_ref, g_ref, tab_ref, s0_ref, dmat_ref, qdec_ref, kdec_ref, decm_ref,
                bdm_ref, amat_ref, rep_ref, rept_ref, y_ref, so_ref, s_ref, *, rows):
    n = q_ref.shape[0]

    @pl.when(pl.program_id(1) == 0)
    def _():
        s_ref[...] = _expand_state(s0_ref[0], rep_ref[...], bdm_ref[...])

    q = _pad_rows(_rope(q_ref[...], tab_ref, RET_DK // 2), rows)
    k = _pad_rows(_rope(k_ref[...], tab_ref, RET_DK // 2) * RET_DK ** -0.5, rows)
    v = _pad_rows(v_ref[...], rows)
    state = s_ref[...]
    inter = _dot(q * qdec_ref[...], state)
    lane = _lane_iota((rows, LANE))
    tiles = []
    for p in range(RET_HEADS // 2):
        lanes = slice(p * LANE, (p + 1) * LANE)
        qp, kp, vp = q[:, lanes], _bf(k[:, lanes]), v[:, lanes]
        acc = None
        for hh in range(2):
            mine = (lane < RET_DK) if hh == 0 else (lane >= RET_DK)
            s = _dot_nt(jnp.where(mine, qp, 0.0), kp) * dmat_ref[2 * p + hh]
            part = _dot(s, jnp.where(mine, vp, 0.0))
            acc = part if acc is None else acc + part
        tiles.append(acc)
    o = jnp.concatenate(tiles, axis=1) + inter
    new_state = state * decm_ref[...] + _dot_tn(k * kdec_ref[...], v) * bdm_ref[...]
    s_ref[...] = new_state

    @pl.when(pl.program_id(1) == pl.num_programs(1) - 1)
    def _():
        so_ref[0] = _collapse_state(new_state, rept_ref[...])

    amat = amat_ref[...]
    mu = _head_mean(o, amat)
    d = o - mu
    var = _head_mean(d * d, amat)
    on = d * lax.rsqrt(var + 1e-5)
    y_ref[...] = (_silu(g_ref[...]) * on[0:n]).astype(y_ref.dtype)


def _head_block_mask():
    h = np.arange(BW) // 64
    return (h[:, None] == h[None, :]).astype(np.float32)


def _ret_consts(rows, n_true):
    lg = np.log(1.0 - np.exp2(-5.0 - np.arange(RET_HEADS, dtype=np.float64)))
    i = np.arange(rows, dtype=np.float64)
    rel = i[:, None] - i[None, :]
    dmat = np.where(rel[None] >= 0, np.exp(np.minimum(rel[None], rows) * lg[:, None, None]), 0.0)
    lane_lg = np.repeat(lg, 64)[None, :]
    qdec = np.exp((i[:, None] + 1.0) * lane_lg)
    kdec = np.where(i[:, None] < n_true, np.exp((n_true - 1.0 - i[:, None]) * lane_lg), 0.0)
    bdm = _head_block_mask()
    decm = bdm * np.exp(n_true * np.repeat(lg, 64))[:, None]
    f = lambda a: jnp.asarray(a, F32)
    return f(dmat), f(qdec), f(kdec), f(decm), f(bdm), jnp.asarray(bdm / 64.0, BF16)


def ret_branch(proj, tab_r, s0, bsz, seq, rows, n):
    nch = seq // n
    dmat, qdec, kdec, decm, bdm, amat = _ret_consts(rows, n)
    rep, rept = _rep_consts()
    col = lambda j: pl.BlockSpec((n, BW), lambda b, c, j=j: (b * nch + c, j))
    const = lambda shape: pl.BlockSpec(shape, lambda b, c: (0,) * len(shape))
    return pl.pallas_call(
        functools.partial(_ret_kernel, rows=rows),
        grid=(bsz, nch),
        in_specs=[col(0), col(1), col(2), col(3),
                  pl.BlockSpec((3, n, LANE), lambda b, c: (0, c, 0)),
                  pl.BlockSpec((1, BW, RET_DK), lambda b, c: (b, 0, 0)),
                  const((RET_HEADS, rows, rows)), const((rows, BW)), const((rows, BW)),
                  const((BW, BW)), const((BW, BW)), const((BW, BW)), const((RET_DK, BW)), const((BW, RET_DK))],
        out_specs=[pl.BlockSpec((n, BW), lambda b, c: (b * nch + c, 0)),
                   pl.BlockSpec((1, BW, RET_DK), lambda b, c: (b, 0, 0))],
        out_shape=[jax.ShapeDtypeStruct((bsz * seq, BW), F32),
                   jax.ShapeDtypeStruct((bsz, BW, RET_DK), F32)],
        scratch_shapes=[pltpu.VMEM((BW, BW), F32)],
        compiler_params=_cp(("parallel", "arbitrary")),
        name="ret_branch",
    )(proj, proj, proj, proj, tab_r, s0, dmat, qdec, kdec, decm, bdm, amat, rep, rept)


def _softplus(x):
    return jnp.maximum(x, 0.0) + jnp.log(1.0 + jnp.exp(-jnp.abs(x)))


GDN_SUPER = ((0, 4 * GDN_CHUNK), (4 * GDN_CHUNK, 6 * GDN_CHUNK))
GM_BLOCK, GM_INCL, GM_STRICT, GM_EYE, GM_LEVEL0 = 0, 1, 2, 3, 4


def _stack_heads(a, bdm_rows):
    return jnp.concatenate([a] * (bdm_rows.shape[0] // GDN_CHUNK), axis=0) * bdm_rows


def _unstack_heads(parts):
    blocks = [p[i:i + GDN_CHUNK] for p in parts for i in range(0, p.shape[0], GDN_CHUNK)]
    out = blocks[0]
    for b in blocks[1:]:
        out = out + b
    return out


def _col_of_heads(a, s, e):
    return jnp.concatenate([jnp.broadcast_to(a[:, h * GDN_DK:h * GDN_DK + 1], (GDN_CHUNK, e - s))
                            for h in range(s // GDN_CHUNK, e // GDN_CHUNK)], axis=0)


def _row_of_heads(a_t, s, e):
    return jnp.concatenate([a_t[h * GDN_DK:h * GDN_DK + 1, :] for h in range(s // GDN_CHUNK, e // GDN_CHUNK)],
                           axis=1)


def _gdn_prepare(chunks, gm_ref, n_real):
    items = [(ci, s, e) for ci in range(len(chunks)) for s, e in GDN_SUPER]
    g_ts = [gc.T for _, _, _, _, gc in chunks]
    nmats, decs, qks = [], [], []
    for ci, s, e in items:
        qc, kc, _, bc, gc = chunks[ci]
        bdm_rows = gm_ref[GM_BLOCK, s:e, :]
        ks = _bf(_stack_heads(kc, bdm_rows))
        kk = _dot_nt(ks, ks)
        qks.append(_dot_nt(_stack_heads(qc, bdm_rows), ks))
        diff = _col_of_heads(gc, s, e) - _row_of_heads(g_ts[ci], s, e)
        dec = jnp.exp(jnp.where(gm_ref[GM_INCL, s:e, s:e] > 0.5, diff, NEG))
        decs.append(dec)
        nmats.append(_col_of_heads(bc, s, e) * (dec * gm_ref[GM_STRICT, s:e, s:e]) * kk)
    invs = [gm_ref[GM_EYE, s:e, s:e] - nm * gm_ref[GM_LEVEL0, s:e, s:e] for nm, (_, s, e) in zip(nmats, items)]
    for lvl in range(1, (min(n_real, GDN_CHUNK) - 1).bit_length()):
        right = [_dot(nm * gm_ref[GM_LEVEL0 + lvl, s:e, s:e], inv) for nm, inv, (_, s, e) in zip(nmats, invs, items)]
        invs = [inv - _dot(inv, r) for inv, r in zip(invs, right)]
    w_st, uv_st = [[] for _ in chunks], [[] for _ in chunks]
    for inv, (ci, s, e) in zip(invs, items):
        _, kc, vc, bc, gc = chunks[ci]
        bdm_rows = gm_ref[GM_BLOCK, s:e, :]
        wu = _dot(inv, jnp.concatenate([_stack_heads(bc * jnp.exp(gc) * kc, bdm_rows),
                                        _stack_heads(bc * vc, bdm_rows)], axis=1))
        w_st[ci].append(wu[:, 0:BW])
        uv_st[ci].append(wu[:, BW:2 * BW])
    n_sb = len(GDN_SUPER)
    return [(_unstack_heads(w_st[ci]), _unstack_heads(uv_st[ci]),
             [qks[ci * n_sb + j] * decs[ci * n_sb + j] for j in range(n_sb)]) for ci in range(len(chunks))]


def _gdn_apply(chunk, prepared, hbd, gm_ref):
    qc, kc, _, _, gc = chunk
    w, uv, a_mats = prepared
    g_last = gc[GDN_CHUNK - 1:GDN_CHUNK, :]
    u = uv - _dot(w, hbd)
    o_st = [_dot(a_mats[i], _stack_heads(u, gm_ref[GM_BLOCK, s:e, :])) for i, (s, e) in enumerate(GDN_SUPER)]
    o = jnp.exp(gc) * _dot(qc, hbd) + _unstack_heads(o_st)
    h_new = jnp.exp(g_last) * hbd + _dot_tn(kc * jnp.exp(g_last - gc), u) * gm_ref[GM_BLOCK]
    return o, h_new


def _gdn_kernel(q_ref, k_ref, v_ref, g_ref, misc_ref, cs0_ref, cw_ref, alog_ref, dtb_ref, ng_ref, h0_ref,
                ea_ref, eb_ref, gm_ref, tri_ref, amat_ref, rep_ref, rept_ref, y_ref, ho_ref, cs_ref,
                xp_sc, h_sc, *, rows):
    n = q_ref.shape[0]
    cw = 3 * BW

    @pl.when(pl.program_id(1) == 0)
    def _():
        xp_sc[...] = jnp.zeros(xp_sc.shape, F32)
        xp_sc[5:8, :] = cs0_ref[0]
        h_sc[...] = _expand_state(h0_ref[0], rep_ref[...], gm_ref[GM_BLOCK])

    for j, r in enumerate((q_ref, k_ref, v_ref)):
        xp_sc[8:8 + n, j * BW:(j + 1) * BW] = r[...]
    conv = xp_sc[5:5 + rows, :] * cw_ref[0:1, :]
    for i in range(1, GDN_CONV):
        conv = conv + xp_sc[5 + i:5 + i + rows, :] * cw_ref[i:i + 1, :]
    tail = xp_sc[8 + n - 3:8 + n, :]
    xp_sc[5:8, :] = tail
    cs_ref[0] = tail
    xc = _silu(conv)
    valid = _row_iota((rows, BW)) < n
    bdm_bf = _bf(gm_ref[GM_BLOCK])
    q, k, v = xc[:, 0:BW], xc[:, BW:2 * BW], xc[:, 2 * BW:cw]
    q = q * lax.rsqrt(_dot2(q * q, bdm_bf) + EPS) * GDN_DK ** -0.5
    k = k * lax.rsqrt(_dot2(k * k, bdm_bf) + EPS)
    misc = _pad_rows(misc_ref[...], rows)
    beta = _sigmoid(_dot2(misc, eb_ref[...]))
    la = -jnp.exp(alog_ref[...]) * _softplus(_dot2(misc, ea_ref[...]) + dtb_ref[...])
    k = jnp.where(valid, k, 0.0)
    v = jnp.where(valid, v, 0.0)
    la = jnp.where(valid, la, 0.0)
    la_hi, la_lo = _split(la)
    tri = tri_ref[...]
    gall = (jnp.dot(tri, la_hi, preferred_element_type=F32)
            + jnp.dot(tri, la_lo, preferred_element_type=F32))
    c = GDN_CHUNK
    chunks = [(q[r], k[r], v[r], beta[r], gall[r]) for r in (slice(i, i + c) for i in range(0, rows, c))]
    prepared = _gdn_prepare(chunks, gm_ref, n)
    outs = []
    hbd = h_sc[...]
    for chunk, prep in zip(chunks, prepared):
        o, hbd = _gdn_apply(chunk, prep, hbd, gm_ref)
        outs.append(o)
    h_sc[...] = hbd

    @pl.when(pl.program_id(1) == pl.num_programs(1) - 1)
    def _():
        ho_ref[0] = _collapse_state(hbd, rept_ref[...])

    o = outs[0] if len(outs) == 1 else jnp.concatenate(outs, axis=0)
    on = o * lax.rsqrt(_dot2(o * o, amat_ref[...]) + EPS) * ng_ref[...]
    y_ref[...] = on[0:n] * _silu(g_ref[...])


def _gdn_consts(rows):
    lanes = np.arange(BW) // 64
    ea = np.zeros((LANE, BW), np.float32)
    eb = np.zeros((LANE, BW), np.float32)
    ea[MISC_DA + lanes, np.arange(BW)] = 1.0
    eb[MISC_DB + lanes, np.arange(BW)] = 1.0
    i = np.arange(rows)
    tri = ((i[:, None] // GDN_CHUNK == i[None, :] // GDN_CHUNK) & (i[:, None] >= i[None, :])).astype(np.float32)
    bdm = _head_block_mask()
    r = np.arange(BW)
    ri, ci = r[:, None] % GDN_CHUNK, r[None, :] % GDN_CHUNK
    gm = [bdm, bdm * (ri >= ci), bdm * (ri > ci), np.eye(BW, dtype=np.float32)]
    s = 1
    while s < GDN_CHUNK:
        gm.append(bdm * ((ri // (2 * s)) == (ci // (2 * s))) * ((ri // s) % 2 == 1) * ((ci // s) % 2 == 0))
        s *= 2
    return (jnp.asarray(ea, BF16), jnp.asarray(eb, BF16), jnp.asarray(np.stack(gm), F32), jnp.asarray(tri, BF16),
            jnp.asarray(bdm / 64.0, BF16))


def gdn_branch(proj, misc, cs0, conv_w, a_log, dt_bias, norm_g, h0, bsz, seq, rows, n):
    nblk = seq // n
    ea, eb, gm, tri, amat = _gdn_consts(rows)
    rep, rept = _rep_consts()
    per_lane = lambda a, reps: jnp.repeat(a, reps).reshape(1, BW) if reps > 1 else jnp.tile(a, BW // a.shape[0]).reshape(1, BW)
    col = lambda j: pl.BlockSpec((n, BW), lambda b, c, j=j: (b * nblk + c, j))
    const = lambda shape: pl.BlockSpec(shape, lambda b, c: (0,) * len(shape))
    per_b = lambda shape: pl.BlockSpec(shape, lambda b, c: (b,) + (0,) * (len(shape) - 1))
    cw = 3 * BW
    return pl.pallas_call(
        functools.partial(_gdn_kernel, rows=rows),
        grid=(bsz, nblk),
        in_specs=[col(4), col(5), col(6), col(7),
                  pl.BlockSpec((n, LANE), lambda b, c: (b * nblk + c, 0)),
                  per_b((1, GDN_CONV - 1, cw)), const((GDN_CONV, cw)),
                  const((1, BW)), const((1, BW)), const((1, BW)), per_b((1, BW, GDN_DK)),
                  const((LANE, BW)), const((LANE, BW)), const(tuple(gm.shape)), const((rows, rows)), const((BW, BW)),
                  const((GDN_DK, BW)), const((BW, GDN_DK))],
        out_specs=[pl.BlockSpec((n, BW), lambda b, c: (b * nblk + c, 0)),
                   per_b((1, BW, GDN_DK)), per_b((1, GDN_CONV - 1, cw))],
        out_shape=[jax.ShapeDtypeStruct((bsz * seq, BW), F32),
                   jax.ShapeDtypeStruct((bsz, BW, GDN_DK), F32),
                   jax.ShapeDtypeStruct((bsz, GDN_CONV - 1, cw), F32)],
        scratch_shapes=[pltpu.VMEM((rows + 8, cw), F32), pltpu.VMEM((BW, BW), F32)],
        compiler_params=_cp(("parallel", "arbitrary")),
        name="gdn_branch",
    )(proj, proj, proj, proj, misc, cs0, conv_w, per_lane(a_log, 64), per_lane(dt_bias, 64),
      per_lane(norm_g, 1), h0, ea, eb, gm, tri, amat, rep, rept)


_BRANCH_OFFS = (0, ATT_WIDTH, ATT_WIDTH + BW, ATT_WIDTH + 2 * BW, MIX_WIDTH)


def _merge_kernel(x_ref, g_ref, sc_ref, sh_ref, gm_ref, ya_ref, yb_ref, yc_ref, yd_ref,
                  wg_ref, wb_ref, wo_ref, o_ref):
    x = x_ref[...]
    h = _bf(_norm_mod(x, g_ref[...], sc_ref[0], sh_ref[0]))
    merged = None
    for b, y_ref in enumerate((ya_ref, yb_ref, yc_ref, yd_ref)):
        gate = _sigmoid(jnp.dot(h, wg_ref[:, b * D_MODEL:(b + 1) * D_MODEL], preferred_element_type=F32))
        term = gate * jnp.dot(_bf(y_ref[...]), wb_ref[_BRANCH_OFFS[b]:_BRANCH_OFFS[b + 1], :],
                              preferred_element_type=F32)
        merged = term if merged is None else merged + term
    y = jnp.dot(_bf(merged), wo_ref[...], preferred_element_type=F32)
    o_ref[...] = x + gm_ref[0] * y


def merge_out(x, g, sc, sh, gm, ya, yb, yc, yd, wg_bf, wb_bf, wo_bf, tm, tiles_per_seq):
    m = x.shape[0]
    ms = _mod_spec(tm, sc.shape[1], tiles_per_seq)
    row = lambda w: pl.BlockSpec((tm, w), lambda i: (i, 0))
    const = lambda shape: pl.BlockSpec(shape, lambda i: (0,) * len(shape))
    return pl.pallas_call(
        _merge_kernel,
        grid=(m // tm,),
        in_specs=[row(D_MODEL), const((1, D_MODEL)), ms, ms, ms, row(ATT_WIDTH), row(BW), row(BW), row(BW),
                  const((D_MODEL, N_BRANCH * D_MODEL)), const((MIX_WIDTH, D_MODEL)), const((D_MODEL, D_MODEL))],
        out_specs=row(D_MODEL),
        out_shape=jax.ShapeDtypeStruct((m, D_MODEL), F32),
        compiler_params=_cp(("parallel",)),
        name="merge_out",
    )(x, g.reshape(1, D_MODEL), sc, sh, gm, ya, yb, yc, yd, wg_bf, wb_bf, wo_bf)


def _top2(masked, lane):
    m1 = jnp.max(masked, axis=-1, keepdims=True)
    i1 = jnp.min(jnp.where(masked == m1, lane, LANE), axis=-1, keepdims=True)
    rest = jnp.where(lane == i1, -jnp.inf, masked)
    m2 = jnp.max(rest, axis=-1, keepdims=True)
    i2 = jnp.min(jnp.where(rest == m2, lane, LANE), axis=-1, keepdims=True)
    return m1, i1, m2, i2


def _route(scores, biased):
    lane = _lane_iota(scores.shape)
    grp = lane // EXPERTS_PER_GROUP
    best_val, best_grp = None, None
    for g in range(N_GROUPS):
        m1, _, m2, _ = _top2(jnp.where(grp == g, biased, -jnp.inf), lane)
        gs = m1 + m2
        if g == 0:
            best_val, best_grp = gs, jnp.zeros(gs.shape, jnp.int32)
        else:
            better = gs > best_val
            best_val = jnp.where(better, gs, best_val)
            best_grp = jnp.where(better, g, best_grp)
    _, e1, _, e2 = _top2(jnp.where(grp == best_grp, biased, -jnp.inf), lane)
    s1 = jnp.sum(jnp.where(lane == e1, scores, 0.0), axis=-1, keepdims=True)
    s2 = jnp.sum(jnp.where(lane == e2, scores, 0.0), axis=-1, keepdims=True)
    tot = s1 + s2
    return jnp.where(lane == e1, s1 / tot, 0.0) + jnp.where(lane == e2, s2 / tot, 0.0), best_grp


MOE_BLOCK = 256
MOE_ALIGN = 16


def _moe_kernel(x_ref, g_ref, sc_ref, sh_ref, gm_ref, wr_ref, rb_ref, tri_ref, upper_ref, w1_ref, w3_ref, w2_ref,
                fg_ref, o_ref, hs_sc, comb_sc, acc_sc, pt_sc, seg_sc, *, final):
    grp_id = pl.program_id(1)
    tm = x_ref.shape[0]

    @pl.when(grp_id == 0)
    def _():
        h = _norm_mod(x_ref[...], g_ref[...], sc_ref[0], sh_ref[0])
        scores = _sigmoid(_dot3(h, wr_ref[...]))
        comb, best = _route(scores, scores + rb_ref[...])
        lane = _lane_iota((tm, LANE))
        onehot = jnp.where(lane == best, 1.0, 0.0)
        incl = jnp.dot(tri_ref[...], _bf(onehot), preferred_element_type=F32)
        counts = incl[tm - 8:tm, :]
        offs = _dot2(counts, upper_ref[...])
        rank = jnp.sum(onehot * (offs[7:8, :] + incl), axis=-1, keepdims=True) - 1.0
        perm_t = jnp.where(_lane_iota((tm, tm)).astype(F32) == rank, 1.0, 0.0).astype(BF16)
        pt_sc[...] = perm_t
        hs_sc[0:tm, :] = _dot_tn(perm_t, _bf(h)).astype(BF16)
        hs_sc[tm:tm + MOE_BLOCK, :] = jnp.zeros((MOE_BLOCK, D_MODEL), BF16)
        ca, cb, cc = _split3(comb)
        tn = functools.partial(lax.dot_general, dimension_numbers=(((0,), (0,)), ((), ())),
                               preferred_element_type=F32)
        comb_sc[0:tm, :] = tn(perm_t, ca) + (tn(perm_t, cb) + tn(perm_t, cc))
        comb_sc[tm:tm + MOE_BLOCK, :] = jnp.zeros((MOE_BLOCK, LANE), F32)
        acc_sc[...] = jnp.zeros(acc_sc.shape, F32)
        for gi in range(N_GROUPS):
            seg_sc[gi] = offs[7, gi].astype(jnp.int32)
            seg_sc[N_GROUPS + gi] = counts[7, gi].astype(jnp.int32)

    off = seg_sc[grp_id]
    cnt = seg_sc[N_GROUPS + grp_id]
    start = (off // MOE_ALIGN) * MOE_ALIGN
    nblk = jnp.where(cnt > 0, (off + cnt - start + MOE_BLOCK - 1) // MOE_BLOCK, 0)

    def block(i, carry):
        r0 = pl.multiple_of(start + i * MOE_BLOCK, MOE_ALIGN)
        hb = hs_sc[pl.ds(r0, MOE_BLOCK), :]
        cblk = comb_sc[pl.ds(r0, MOE_BLOCK), :]
        lane = _lane_iota(cblk.shape)
        out = None
        for e in range(EXPERTS_PER_GROUP):
            ce = jnp.sum(jnp.where(lane == grp_id * EXPERTS_PER_GROUP + e, cblk, 0.0), axis=-1, keepdims=True)
            hid = (_silu(jnp.dot(hb, w1_ref[0, e], preferred_element_type=F32))
                   * jnp.dot(hb, w3_ref[0, e], preferred_element_type=F32))
            part = jnp.dot(_bf(hid * ce), w2_ref[0, e], preferred_element_type=F32)
            out = part if out is None else out + part
        acc_sc[pl.ds(r0, MOE_BLOCK), :] += out
        return carry

    lax.fori_loop(0, nblk, block, 0)

    @pl.when(grp_id == pl.num_programs(1) - 1)
    def _():
        d = functools.partial(jnp.dot, preferred_element_type=F32)
        a, b = _split(acc_sc[0:tm, :])
        perm_t = pt_sc[...]
        out = x_ref[...] + gm_ref[0] * (d(perm_t, a) + d(perm_t, b))
        if final:
            out = out * lax.rsqrt(jnp.mean(out * out, axis=-1, keepdims=True) + EPS) * fg_ref[...]
        o_ref[...] = out


def moe_out(x, g, sc, sh, gm, wr_pad, rb_pad, w1_bf, w3_bf, w2_bf, final_g, final, tm, tiles_per_seq):
    m = x.shape[0]
    ms = _mod_spec(tm, sc.shape[1], tiles_per_seq)
    const = lambda shape: pl.BlockSpec(shape, lambda i, e: (0,) * len(shape))
    r = np.arange(tm)
    tri = jnp.asarray(r[:, None] >= r[None, :], BF16)
    u = np.arange(LANE)
    upper = jnp.asarray(u[:, None] < u[None, :], BF16)
    grouped = lambda w: w.reshape((N_GROUPS, EXPERTS_PER_GROUP) + w.shape[1:])
    wspec = lambda a, b: pl.BlockSpec((1, EXPERTS_PER_GROUP, a, b), lambda i, e: (e, 0, 0, 0))
    return pl.pallas_call(
        functools.partial(_moe_kernel, final=final),
        grid=(m // tm, N_GROUPS),
        in_specs=[pl.BlockSpec((tm, D_MODEL), lambda i, e: (i, 0)), const((1, D_MODEL)), ms, ms, ms,
                  const((D_MODEL, LANE)), const((1, LANE)), const((tm, tm)), const((LANE, LANE)),
                  wspec(D_MODEL, EXPERT_FF), wspec(D_MODEL, EXPERT_FF), wspec(EXPERT_FF, D_MODEL),
                  const((1, D_MODEL))],
        out_specs=pl.BlockSpec((tm, D_MODEL), lambda i, e: (i, 0)),
        out_shape=jax.ShapeDtypeStruct((m, D_MODEL), F32),
        scratch_shapes=[pltpu.VMEM((tm + MOE_BLOCK, D_MODEL), BF16), pltpu.VMEM((tm + MOE_BLOCK, LANE), F32),
                        pltpu.VMEM((tm + MOE_BLOCK, D_MODEL), F32), pltpu.VMEM((tm, tm), BF16),
                        pltpu.SMEM((2 * N_GROUPS,), jnp.int32)],
        compiler_params=_cp(("parallel", "arbitrary")),
        name="moe_out",
    )(x, g.reshape(1, D_MODEL), sc, sh, gm, wr_pad, rb_pad, tri, upper, grouped(w1_bf), grouped(w3_bf),
      grouped(w2_bf), final_g.reshape(1, D_MODEL))


_REF_SPLITS = (ATT_WIDTH, KVW, KVW, IDX_HEADS * IDX_DIM, IDX_DIM, IDX_HEADS, BW,
               BW, BW, BW, BW, BW, BW, BW, GDN_HEADS, GDN_HEADS, BW)


def pack_w_in(w_in):
    offs = np.concatenate([[0], np.cumsum(_REF_SPLITS)])
    seg = [w_in[:, int(offs[i]):int(offs[i + 1])] for i in range(len(_REF_SPLITS))]
    (aq, ak, av, aiq, aik, aiw, bu, cq, ck, cv, cg, dq, dk, dv, da, db, dg) = seg
    zeros = lambda n: jnp.zeros((D_MODEL, n), w_in.dtype)
    misc = jnp.concatenate([aik, aiw, da, db, zeros(LANE - IDX_DIM - IDX_HEADS - 2 * GDN_HEADS)], axis=1)
    packed = jnp.concatenate([cq, ck, cv, cg, dq, dk, dv, dg, bu, aq, ak, av, aiq, zeros(2 * LANE)], axis=1)
    wt = jnp.concatenate([aiw.T, jnp.zeros((8 - IDX_HEADS, D_MODEL), w_in.dtype)], axis=0)
    return _bf(packed), misc, wt


def _time_major(a, bsz, seq):
    return a.reshape(bsz, seq, a.shape[-1]).transpose(1, 0, 2).reshape(seq * bsz, a.shape[-1])


def _batch_major(a, bsz, seq):
    return a.reshape(seq, bsz, a.shape[-1]).transpose(1, 0, 2).reshape(bsz * seq, a.shape[-1])


def _trunk_layer(x, mods, geom, attend, st, lw, final_g, final):
    bsz, seq, tm, tps, s5_tc, ret_rows, ret_n, gdn_rows, gdn_n = geom
    sh1, sc1, g1, sh2, sc2, g2 = mods
    proj, misc, iwt = in_proj(x, lw['norm1'], sc1, sh1, lw['w_in'], lw['w_misc'], lw['w_iwt'], tm, tps)
    qr, kr, v, iqr, ikr, ik4, vt = attn_prep(proj, misc, lw['tab_a'], lw['tab_i'], lw['tab_k'], tm, tps)
    ya = attend(qr, kr, v, vt, iqr, ikr, ik4, iwt, misc)
    u_tm = _time_major(proj[:, 8 * BW:9 * BW], bsz, seq)
    y_tm, s5_h = s5_branch(u_tm, st['s5'], lw['s5_abar'], lw['s5_win'], lw['s5_wout'], lw['s5_d'],
                           lw['s5_w_glu'], lw['s5_b_glu'], bsz, seq, s5_tc)
    yb = _batch_major(y_tm, bsz, seq)
    yc, ret_s = ret_branch(proj, lw['tab_r'], st['ret'], bsz, seq, ret_rows, ret_n)
    yd, gdn_s, conv_s = gdn_branch(proj, misc, st['conv'], lw['gdn_conv_w'], lw['gdn_a_log'], lw['gdn_dt_bias'],
                                   lw['gdn_norm_g'], st['gdn'], bsz, seq, gdn_rows, gdn_n)
    x = merge_out(x, lw['norm1'], sc1, sh1, g1, ya, yb, yc, yd, lw['w_gate'], lw['w_br'], lw['w_out'], tm, tps)
    x = moe_out(x, lw['norm2'], sc2, sh2, g2, lw['w_router'], lw['router_bias'], lw['w_e1'], lw['w_e3'],
                lw['w_e2'], final_g, final, tm, tps)
    new_st = {'k': kr, 'v': v, 'ik': ikr, 's5': s5_h, 'ret': ret_s, 'gdn': gdn_s, 'conv': conv_s}
    return x, new_st


def kernel(x_prompt, x_sample, c_prompt, c_sample, cache_k, cache_v, cache_idx_k, page_table,
           state_s5_re, state_s5_im, state_ret, state_gdn, state_gdn_conv,
           norm1_g, norm2_g, final_g, w_ada, b_ada, w_in,
           s5_a_re, s5_a_im, s5_b_re, s5_b_im, s5_c_re, s5_c_im, s5_d, s5_log_dt, s5_w_glu, s5_b_glu,
           gdn_conv_w, gdn_a_log, gdn_dt_bias, gdn_norm_g,
           w_br, w_gate, w_out, w_router, router_bias, w_e1, w_e3, w_e2):
    bsz, seq, _ = x_prompt.shape
    dbs, dseq, _ = x_sample.shape
    depth = w_in.shape[0]
    n_pool = cache_k.shape[1]
    past = page_table.shape[1] * PAGE_SIZE
    mp, ms = bsz * seq, dbs * dseq
    tm_p = 512
    pos_p = jnp.arange(seq, dtype=jnp.int32)
    pos_s = past + jnp.arange(dseq, dtype=jnp.int32)
    pos_s_tok = jnp.tile(pos_s, dbs)

    def tables(pos):
        return {'tab_a': _rope_tables(pos, ROT_DIMS, ROPE_THETA, HEAD_DIM),
                'tab_i': _rope_tables(pos, IDX_ROT, ROPE_THETA, IDX_DIM),
                'tab_k': _rope_tables(pos, IDX_ROT, ROPE_THETA, IDX_DIM, active=IDX_DIM)}

    tabs_p = dict(tables(pos_p), tab_r=_rope_tables(pos_p, RET_DK, RET_THETA, RET_DK))
    tabs_s = dict(tables(pos_s_tok), tab_r=_rope_tables(pos_s, RET_DK, RET_THETA, RET_DK))
    ck = cache_k.transpose(0, 1, 3, 4, 2).reshape(depth * n_pool, KVW, PAGE_SIZE)
    cv = cache_v.transpose(0, 1, 3, 4, 2).reshape(depth * n_pool, KVW, PAGE_SIZE)
    cik = cache_idx_k.transpose(0, 1, 3, 2).reshape(depth * n_pool, IDX_DIM, PAGE_SIZE)
    wr_pad = jnp.pad(w_router, ((0, 0), (0, LANE - N_EXPERTS)))
    rb_pad = jnp.pad(router_bias, (0, LANE - N_EXPERTS)).reshape(1, LANE)
    c_all = jnp.concatenate([c_prompt, c_sample], axis=0)

    geom_p = (bsz, seq, tm_p, seq // tm_p, 128, 256, 256, 256, 256)
    geom_s = (dbs, dseq, ms, 1, dseq, LANE, dseq, GDN_CHUNK, dseq)
    zero_st = {'s5': jnp.zeros((bsz, 2 * S5_NS), F32), 'ret': jnp.zeros((bsz, BW, RET_DK), F32),
               'gdn': jnp.zeros((bsz, BW, GDN_DK), F32), 'conv': jnp.zeros((bsz, GDN_CONV - 1, 3 * BW), F32)}

    xp = x_prompt.reshape(mp, D_MODEL)
    xs = x_sample.reshape(ms, D_MODEL)
    outs_p, outs_s = [], []
    for l in range(depth):
        w_in_p, w_misc, w_iwt = pack_w_in(w_in[l])
        abar8, win = s5_params(s5_log_dt[l], s5_a_re[l], s5_a_im[l], s5_b_re[l], s5_b_im[l])
        lw = {'norm1': norm1_g[l], 'norm2': norm2_g[l], 'w_in': w_in_p, 'w_misc': w_misc, 'w_iwt': w_iwt,
              's5_abar': abar8, 's5_win': win,
              's5_wout': _bf(jnp.concatenate([_block_diag_out(s5_c_re[l]), _block_diag_out(s5_c_im[l])], axis=0)),
              's5_d': s5_d[l], 's5_w_glu': _bf(s5_w_glu[l]), 's5_b_glu': s5_b_glu[l],
              'gdn_conv_w': gdn_conv_w[l], 'gdn_a_log': gdn_a_log[l], 'gdn_dt_bias': gdn_dt_bias[l],
              'gdn_norm_g': gdn_norm_g[l],
              'w_br': _bf(w_br[l]), 'w_gate': _bf(w_gate[l]), 'w_out': _bf(w_out[l]),
              'w_router': wr_pad, 'router_bias': rb_pad,
              'w_e1': _bf(w_e1[l]), 'w_e3': _bf(w_e3[l]), 'w_e2': _bf(w_e2[l])}
        mod = ada_mod(c_all, _bf(w_ada[l]), b_ada[l])
        mods = [mod[:, i * D_MODEL:(i + 1) * D_MODEL] for i in range(6)]
        mods_p = [m[:bsz].reshape(bsz, 1, D_MODEL) for m in mods]
        mods_s = [jnp.repeat(m[bsz:], dseq, axis=0).reshape(1, ms, D_MODEL) for m in mods]
        final = l == depth - 1

        def attend_p(qr, kr, v, vt, iqr, ikr, ik4, iwt, misc):
            return attn_prompt(qr, kr, vt, iqr, ik4, iwt, bsz, seq)

        def attend_s(qr, kr, v, vt, iqr, ikr, ik4, iwt, misc, l=l):
            return attn_sample(qr, kr, v, iqr, ik4, misc, ck, cv, cik, page_table, l, dseq)

        st_s = {'s5': jnp.concatenate([state_s5_re[l].reshape(dbs, S5_NS), state_s5_im[l].reshape(dbs, S5_NS)], axis=1),
                'ret': state_ret[l].reshape(dbs, BW, RET_DK), 'gdn': state_gdn[l].reshape(dbs, BW, GDN_DK),
                'conv': state_gdn_conv[l]}
        xp, ns_p = _trunk_layer(xp, mods_p, geom_p, attend_p, zero_st, dict(lw, **tabs_p), final_g, final)
        xs, ns_s = _trunk_layer(xs, mods_s, geom_s, attend_s, st_s, dict(lw, **tabs_s), final_g, final)
        outs_p.append(ns_p)
        outs_s.append(ns_s)

    def stack(outs, name, shape):
        return jnp.stack([o[name] for o in outs], axis=0).reshape((depth,) + shape)

    def states(outs, b, t):
        re = jnp.stack([o['s5'][:, :S5_NS] for o in outs], axis=0).reshape(depth, b, S5_GROUPS, S5_STATE)
        im = jnp.stack([o['s5'][:, S5_NS:] for o in outs], axis=0).reshape(depth, b, S5_GROUPS, S5_STATE)
        ret = stack(outs, 'ret', (b, RET_HEADS, RET_DK, RET_DK))
        gdn = stack(outs, 'gdn', (b, GDN_HEADS, GDN_DK, GDN_DK))
        return (stack(outs, 'k', (b, t, KV_HEADS, HEAD_DIM)), stack(outs, 'v', (b, t, KV_HEADS, HEAD_DIM)),
                stack(outs, 'ik', (b, t, IDX_DIM)), re, im, ret, gdn,
                stack(outs, 'conv', (b, GDN_CONV - 1, 3 * BW)))

    kp, vp, ikp, rep, imp, retp, gdnp, convp = states(outs_p, bsz, seq)
    ks_, vs_, iks, res, ims, rets, gdns, convs = states(outs_s, dbs, dseq)
    return (xp.reshape(bsz, seq, D_MODEL), xs.reshape(dbs, dseq, D_MODEL), kp, vp, ikp, ks_, vs_, iks,
            rep, imp, res, ims, retp, rets, gdnp, gdns, convp, convs)
```

```python
import functools
import math

import numpy as np
import jax
import jax.numpy as jnp
from jax import lax
from jax.experimental import pallas as pl
from jax.experimental.pallas import tpu as pltpu

F32 = jnp.float32
BF16 = jnp.bfloat16

D_MODEL = 1024
DEPTH = 2
PAST_LEN = 8192
PAGE_SIZE = 128
ATT_HEADS = 8
KV_HEADS = 2
HEAD_DIM = 64
ROT_DIMS = HEAD_DIM // 4
ROPE_THETA = 500000.0
IDX_HEADS = 4
IDX_DIM = 32
IDX_ROT = IDX_DIM // 4
TOPK_MAX = 256
S5_GROUPS = 24
S5_GROUP_CH = 16
S5_STATE = 64
S5_WIDTH = S5_GROUPS * S5_GROUP_CH
S5_NS = S5_GROUPS * S5_STATE
RET_HEADS = 6
RET_DK = 64
RET_THETA = 10000.0
GDN_HEADS = 6
GDN_DK = 64
GDN_CONV = 4
GDN_CHUNK = 64
ATT_WIDTH = ATT_HEADS * HEAD_DIM
BW = 384
KVW = KV_HEADS * HEAD_DIM
N_BRANCH = 4
MIX_WIDTH = ATT_WIDTH + 3 * BW
N_EXPERTS = 16
N_GROUPS = 4
EXPERTS_PER_GROUP = 4
EXPERT_FF = 256
EPS = 1e-6

PROJ_COLS = 12 * BW
ATT_COL0 = 9 * BW
MISC_DA = 36
MISC_DB = 42
LANE = 128
VMEM_LIMIT = 56 * 1024 * 1024
NEG = -1e30


def _cp(sem):
    return pltpu.CompilerParams(dimension_semantics=sem, vmem_limit_bytes=VMEM_LIMIT)


def _bf(x):
    return x.astype(BF16)


def _dot(a, b):
    return jnp.dot(_bf(a), _bf(b), preferred_element_type=F32)


def _dot_nt(a, b):
    return lax.dot_general(_bf(a), _bf(b), (((1,), (1,)), ((), ())), preferred_element_type=F32)


def _dot_tn(a, b):
    return lax.dot_general(_bf(a), _bf(b), (((0,), (0,)), ((), ())), preferred_element_type=F32)


def _split(x):
    hi = x.astype(BF16)
    lo = (x - hi.astype(F32)).astype(BF16)
    return hi, lo


def _dot3(a, b):
    ah, al = _split(a)
    bh, bl = _split(b)
    d = functools.partial(jnp.dot, preferred_element_type=F32)
    return d(ah, bh) + (d(ah, bl) + d(al, bh))


def _dot2(a, b01):
    ah, al = _split(a)
    d = functools.partial(jnp.dot, preferred_element_type=F32)
    return d(ah, b01) + d(al, b01)


def _sigmoid(x):
    return 1.0 / (1.0 + jnp.exp(-x))


def _silu(x):
    return x * _sigmoid(x)


def _lane_iota(shape):
    return lax.broadcasted_iota(jnp.int32, shape, len(shape) - 1)


def _row_iota(shape):
    return lax.broadcasted_iota(jnp.int32, shape, len(shape) - 2)


def _ada_kernel(c_ref, w_ref, b_ref, o_ref):
    o_ref[...] = _dot(_silu(c_ref[...]), w_ref[...]) + b_ref[...]


def ada_mod(c, w_bf, b):
    n = c.shape[0]
    cols = w_bf.shape[1]
    tn = 1024
    return pl.pallas_call(
        _ada_kernel,
        grid=(cols // tn,),
        in_specs=[pl.BlockSpec((n, D_MODEL), lambda j: (0, 0)),
                  pl.BlockSpec((D_MODEL, tn), lambda j: (0, j)),
                  pl.BlockSpec((1, tn), lambda j: (0, j))],
        out_specs=pl.BlockSpec((n, tn), lambda j: (0, j)),
        out_shape=jax.ShapeDtypeStruct((n, cols), F32),
        compiler_params=_cp(("parallel",)),
        name="ada_mod",
    )(c, w_bf, b.reshape(1, cols))


def _norm_mod(x, g, sc, sh):
    y = x * lax.rsqrt(jnp.mean(x * x, axis=-1, keepdims=True) + EPS) * g
    return y * (1.0 + sc) + sh


def _in_kernel(x_ref, g_ref, sc_ref, sh_ref, w_ref, wm_ref, wt_ref, o_ref, om_ref, ot_ref, h_sc):
    @pl.when(pl.program_id(1) == 0)
    def _():
        h = _norm_mod(x_ref[...], g_ref[...], sc_ref[0], sh_ref[0])
        h_sc[...] = h.astype(BF16)
        hh, hl = _split(h)
        wmh, wml = _split(wm_ref[...])
        d = functools.partial(jnp.dot, preferred_element_type=F32)
        om_ref[...] = d(hh, wmh) + (d(hh, wml) + d(hl, wmh))
        wth, wtl = _split(wt_ref[...])
        nt = functools.partial(lax.dot_general, dimension_numbers=(((1,), (1,)), ((), ())),
                               preferred_element_type=F32)
        ot_ref[...] = nt(wth, hh) + (nt(wth, hl) + nt(wtl, hh))

    o_ref[...] = jnp.dot(h_sc[...], w_ref[...], preferred_element_type=F32)


def _mod_spec(tm, mod_rows, tiles_per_seq):
    if mod_rows == 1:
        return pl.BlockSpec((1, 1, D_MODEL), lambda i, *_: (i // tiles_per_seq, 0, 0))
    return pl.BlockSpec((1, tm, D_MODEL), lambda i, *_: (i, 0, 0))


def in_proj(x, g, sc, sh, w_bf, w_misc, w_iwt, tm, tiles_per_seq):
    m = x.shape[0]
    tn = PROJ_COLS
    mod_rows = sc.shape[1]
    ms = _mod_spec(tm, mod_rows, tiles_per_seq)
    return pl.pallas_call(
        _in_kernel,
        grid=(m // tm, PROJ_COLS // tn),
        in_specs=[pl.BlockSpec((tm, D_MODEL), lambda i, j: (i, 0)),
                  pl.BlockSpec((1, D_MODEL), lambda i, j: (0, 0)),
                  ms, ms,
                  pl.BlockSpec((D_MODEL, tn), lambda i, j: (0, j)),
                  pl.BlockSpec((D_MODEL, LANE), lambda i, j: (0, 0)),
                  pl.BlockSpec((8, D_MODEL), lambda i, j: (0, 0))],
        out_specs=[pl.BlockSpec((tm, tn), lambda i, j: (i, j)),
                   pl.BlockSpec((tm, LANE), lambda i, j: (i, 0)),
                   pl.BlockSpec((8, tm), lambda i, j: (0, i))],
        out_shape=[jax.ShapeDtypeStruct((m, PROJ_COLS), F32),
                   jax.ShapeDtypeStruct((m, LANE), F32),
                   jax.ShapeDtypeStruct((8, m), F32)],
        scratch_shapes=[pltpu.VMEM((tm, D_MODEL), BF16)],
        compiler_params=_cp(("parallel", "arbitrary")),
        name="in_proj",
    )(x, g.reshape(1, D_MODEL), sc, sh, w_bf, w_misc, w_iwt)


def _rope_tables(pos, rot_dims, theta, period, width=LANE, active=None):
    half = rot_dims // 2
    inv_freq = jnp.power(jnp.float32(theta), -jnp.arange(half, dtype=F32) / half)
    ang = pos.astype(F32)[:, None] * inv_freq
    cos, sin = jnp.cos(ang), jnp.sin(ang)
    t = pos.shape[0]
    c = jnp.concatenate([cos, cos, jnp.ones((t, period - rot_dims), F32)], axis=1)
    s_up = jnp.concatenate([-sin, jnp.zeros((t, period - half), F32)], axis=1)
    s_dn = jnp.concatenate([jnp.zeros((t, half), F32), sin, jnp.zeros((t, period - rot_dims), F32)], axis=1)
    reps = width // period
    tab = jnp.stack([jnp.tile(a, (1, reps)) for a in (c, s_up, s_dn)], axis=0)
    if active is not None:
        ident = jnp.stack([jnp.ones((t, width), F32), jnp.zeros((t, width), F32),
                           jnp.zeros((t, width), F32)], axis=0)
        tab = jnp.where(jnp.arange(width) < active, tab, ident)
    return tab


def _rope(x, tab_ref, half):
    w = x.shape[1]
    reps = w // LANE

    def wide(k):
        t = tab_ref[k]
        return t if reps == 1 else jnp.concatenate([t] * reps, axis=1)

    return (x * wide(0) + pltpu.roll(x, w - half, axis=1) * wide(1)
            + pltpu.roll(x, half, axis=1) * wide(2))


def _prep_kernel(p_ref, m_ref, ta_ref, ti_ref, tk_ref, q_ref, k_ref, v_ref, iq_ref, ik_ref, ik4_ref, vt_ref):
    q_ref[...] = _rope(p_ref[:, 0:ATT_WIDTH], ta_ref, ROT_DIMS // 2)
    k_ref[...] = _rope(p_ref[:, 512:640], ta_ref, ROT_DIMS // 2)
    v_ref[...] = p_ref[:, 640:768]
    vt_ref[0] = p_ref[:, 640:768].T
    iq_ref[...] = _rope(p_ref[:, 768:896], ti_ref, IDX_ROT // 2)
    ikr = _rope(m_ref[...], tk_ref, IDX_ROT // 2)
    ik_ref[...] = ikr[:, 0:IDX_DIM]
    m = jnp.where(_lane_iota(ikr.shape) < IDX_DIM, ikr, 0.0)
    ik4_ref[...] = (m + pltpu.roll(m, 32, axis=1)) + (pltpu.roll(m, 64, axis=1) + pltpu.roll(m, 96, axis=1))


def attn_prep(proj, misc, tab_a, tab_i, tab_k, tm, tiles_per_seq):
    m = proj.shape[0]
    tspec = pl.BlockSpec((3, tm, LANE), lambda i: (0, i % tiles_per_seq, 0))
    widths = (ATT_WIDTH, KVW, KVW, LANE, IDX_DIM, LANE)
    seq = tm * tiles_per_seq
    return pl.pallas_call(
        _prep_kernel,
        grid=(m // tm,),
        in_specs=[pl.BlockSpec((tm, 3 * BW), lambda i: (i, ATT_COL0 // (3 * BW))),
                  pl.BlockSpec((tm, LANE), lambda i: (i, 0)), tspec, tspec, tspec],
        out_specs=[pl.BlockSpec((tm, w), lambda i: (i, 0)) for w in widths]
        + [pl.BlockSpec((1, KVW, tm), lambda i: (i // tiles_per_seq, 0, i % tiles_per_seq))],
        out_shape=[jax.ShapeDtypeStruct((m, w), F32) for w in widths]
        + [jax.ShapeDtypeStruct((m // seq, KVW, seq), F32)],
        compiler_params=_cp(("parallel",)),
        name="attn_prep",
    )(proj, misc, tab_a, tab_i, tab_k)


BISECT_MAX_ITERS = 48
BISECT_UNROLL = 4
FAR = 2.0 ** 126


def _count(ones, axis):
    return jnp.sum(ones, axis=axis, keepdims=True)


COUNT_ROWS = 32
COUNT_ACCS = 4


def _count_where(s_ref, pred, axis):
    n = s_ref.shape[0]
    if axis != 0 or n % (COUNT_ROWS * COUNT_ACCS) != 0:
        return _count(pred(s_ref[...]), axis)
    accs = [None] * COUNT_ACCS
    for j, i in enumerate(range(0, n, COUNT_ROWS)):
        part = pred(s_ref[i:i + COUNT_ROWS, :])
        a = j % COUNT_ACCS
        accs[a] = part if accs[a] is None else accs[a] + part
    return jnp.sum((accs[0] + accs[1]) + (accs[2] + accs[3]), axis=0, keepdims=True)


def _bisect_topk(s_ref, axis, topk):
    kshape = tuple(1 if a == axis else n for a, n in enumerate(s_ref.shape))
    s = s_ref[...]
    lo0 = jnp.min(jnp.where(s > -FAR, s, FAR), axis=axis, keepdims=True)
    mx = jnp.max(s, axis=axis, keepdims=True)
    hi0 = mx + (jnp.abs(mx) * 2.0 ** -20 + 1e-30)
    cnt_lo0 = _count_where(s_ref, lambda t: jnp.where(t >= lo0, 1, 0), axis)
    n_zero = _count_where(s_ref, lambda t: jnp.where(t == 0.0, 1, 0), axis)

    def pending(lo, hi, cnt_lo, cnt_hi):
        only_zeros = jnp.where(lo <= 0.0, jnp.where(hi > 0.0, jnp.where(cnt_lo - cnt_hi == n_zero, 1, 0), 0), 0)
        return jnp.max(jnp.where(cnt_lo <= topk, 0, 1 - only_zeros))

    def cond(c):
        return jnp.logical_and(c[0] < BISECT_MAX_ITERS, c[1] > 0)

    def body(c):
        it, _, lo, hi, cnt_lo, cnt_hi = c
        for _ in range(BISECT_UNROLL):
            mid = 0.5 * lo + 0.5 * hi
            cm = _count_where(s_ref, lambda t: jnp.where(t >= mid, 1, 0), axis)
            ge = cm >= topk
            lo, hi = jnp.where(ge, mid, lo), jnp.where(ge, hi, mid)
            cnt_lo, cnt_hi = jnp.where(ge, cm, cnt_lo), jnp.where(ge, cnt_hi, cm)
        return it + BISECT_UNROLL, pending(lo, hi, cnt_lo, cnt_hi), lo, hi, cnt_lo, cnt_hi

    cnt_hi0 = jnp.zeros(kshape, jnp.int32)
    init = (jnp.int32(0), pending(lo0, hi0, cnt_lo0, cnt_hi0), lo0, hi0, cnt_lo0, cnt_hi0)
    _, _, lo, hi, cnt_lo, cnt_hi = lax.while_loop(cond, body, init)
    return lo, hi, cnt_lo, cnt_hi


def _topk_select(s_ref, idx, axis, n_idx_bits, p_sc, topk):
    kshape = tuple(1 if a == axis else n for a, n in enumerate(s_ref.shape))
    lo, hi, cnt_lo, cnt_hi = _bisect_topk(s_ref, axis, topk)
    need = topk - cnt_hi
    p_sc[...] = jnp.full(kshape, (1 << n_idx_bits) - 1, jnp.int32)

    @pl.when(jnp.max(cnt_lo - cnt_hi - need) > 0)
    def _():
        tied = jnp.where(s_ref[...] >= lo, jnp.where(s_ref[...] >= hi, 0, 1), 0)

        def ibody(i, p):
            cand = p + jnp.left_shift(jnp.int32(1), n_idx_bits - 1 - i)
            taken = _count(jnp.where(idx < cand, tied, 0), axis)
            return jnp.where(taken < need, cand, p)

        p_sc[...] = lax.fori_loop(0, n_idx_bits, ibody, jnp.zeros(kshape, jnp.int32))

    s = s_ref[...]
    return jnp.where(s >= hi, 1, jnp.where(s >= lo, jnp.where(idx <= p_sc[...], 1, 0), 0))


def _topk_bias_keys_major(s_ref, tri_ref, topk):
    tk, tq = s_ref.shape
    lo, hi, _, cnt_hi = _bisect_topk(s_ref, 0, topk)
    need = (topk - cnt_hi).astype(F32)
    tri = tri_ref[...]
    offset = jnp.zeros((1, tq), F32)
    parts = []
    for c in range(tk // tri.shape[0]):
        s = s_ref[c * tri.shape[0]:(c + 1) * tri.shape[0], :]
        cand = jnp.where(s >= lo, jnp.where(s >= hi, 0.0, 1.0), 0.0)
        rank = jnp.dot(tri, _bf(cand), preferred_element_type=F32) + offset
        offset = rank[tri.shape[0] - 1:tri.shape[0], :]
        parts.append(jnp.where(s >= hi, 0.0, jnp.where(cand * rank > 0.5, jnp.where(rank <= need, 0.0, NEG), NEG)))
    return jnp.concatenate(parts, axis=0)


def _group_queries(q, g):
    tiles = []
    keep = (_lane_iota((q.shape[0], LANE)) // HEAD_DIM) == g
    for hl in range(ATT_HEADS // KV_HEADS):
        h = g * (ATT_HEADS // KV_HEADS) + hl
        t = q[:, (h // 2) * LANE:(h // 2 + 1) * LANE]
        if h % 2 != g:
            t = pltpu.roll(t, HEAD_DIM, axis=1)
        tiles.append(jnp.where(keep, t, 0.0))
    return jnp.concatenate(tiles, axis=0)


def _ungroup_outputs(o_groups, tq):
    low = _lane_iota((tq, LANE)) < HEAD_DIM
    tiles = []
    for j in range(ATT_HEADS // 2):
        halves = []
        for h in (2 * j, 2 * j + 1):
            g, hl = divmod(h, ATT_HEADS // KV_HEADS)
            t = o_groups[g][hl * tq:(hl + 1) * tq]
            if h % 2 != g:
                t = pltpu.roll(t, HEAD_DIM, axis=1)
            halves.append(t)
        tiles.append(jnp.where(low, halves[0], halves[1]))
    return jnp.concatenate(tiles, axis=1)


def _masked_attention_keys_major(q, kb, vt, bias_t):
    tq = q.shape[0]
    heads = ATT_HEADS // KV_HEADS
    tk = kb.shape[0]
    bias4 = jnp.concatenate([bias_t] * heads, axis=1)
    vrow_group = _row_iota((LANE, tk)) // HEAD_DIM
    q = q * HEAD_DIM ** -0.5
    normed = []
    for g in range(KV_HEADS):
        st = _dot_nt(kb, _group_queries(q, g)) + bias4
        p = jnp.exp(_bf(st - jnp.max(st, axis=0, keepdims=True)))
        ot = jnp.dot(_bf(jnp.where(vrow_group == g, vt, 1.0)), p, preferred_element_type=F32)
        other = (1 - g) * HEAD_DIM
        normed.append(ot[g * HEAD_DIM:(g + 1) * HEAD_DIM, :] / ot[other:other + 1, :])
    tiles = []
    for j in range(ATT_HEADS // 2):
        g, hl = divmod(2 * j, heads)
        pair = jnp.concatenate([normed[g][:, hl * tq:(hl + 1) * tq], normed[g][:, (hl + 1) * tq:(hl + 2) * tq]], axis=0)
        tiles.append(pair.T)
    return jnp.concatenate(tiles, axis=1)


def _attn_prompt_kernel(q_ref, iq_ref, iwt_ref, k_ref, vt_ref, ik4_ref, tri_ref, o_ref, s_sc, *, qblk0, tq, topk):
    tk = k_ref.shape[1]
    q0 = (qblk0 + pl.program_id(1)) * tq
    iq = iq_ref[0]
    head_of_lane = _lane_iota((tq, LANE)) // IDX_DIM
    iq4 = jnp.concatenate([jnp.where(head_of_lane == h, iq, 0.0) for h in range(IDX_HEADS)], axis=0)
    lg = _dot_nt(ik4_ref[0], iq4)
    iwt = iwt_ref[...] * (IDX_HEADS ** -0.5 * IDX_DIM ** -0.5)
    score = None
    for h in range(IDX_HEADS):
        part = jnp.maximum(lg[:, h * tq:(h + 1) * tq], 0.0) * iwt[h:h + 1, :]
        score = part if score is None else score + part
    adm = _row_iota((tk, tq)) <= q0 + _lane_iota((tk, tq))
    s_sc[...] = jnp.where(adm, score, -FAR)
    bias_t = _topk_bias_keys_major(s_sc, tri_ref, topk)
    o_ref[0] = _masked_attention_keys_major(q_ref[0], _bf(k_ref[0]), vt_ref[0], bias_t)


PREFIX_ROWS = 256


def attn_prompt(qr, kr, vt, iqr, ik4, iwt, bsz, seq, n_classes=8, tq=256):
    topk = min(TOPK_MAX, seq // 4)
    nq = seq // tq
    per = max(1, nq // n_classes)
    q3 = qr.reshape(bsz, seq, ATT_WIDTH)
    iq3 = iqr.reshape(bsz, seq, LANE)
    k3, ik3 = (a.reshape(bsz, seq, LANE) for a in (kr, ik4))
    outs = []
    for c in range(nq // per):
        tk = (c + 1) * per * tq
        qb0 = c * per
        r = np.arange(math.gcd(PREFIX_ROWS, tk))
        tri = jnp.asarray(r[:, None] >= r[None, :], BF16)
        out = pl.pallas_call(
            functools.partial(_attn_prompt_kernel, qblk0=qb0, tq=tq, topk=topk),
            grid=(bsz, per),
            in_specs=[pl.BlockSpec((1, tq, ATT_WIDTH), lambda b, j, qb0=qb0: (b, qb0 + j, 0)),
                      pl.BlockSpec((1, tq, LANE), lambda b, j, qb0=qb0: (b, qb0 + j, 0)),
                      pl.BlockSpec((8, tq), lambda b, j, qb0=qb0: (0, b * nq + qb0 + j)),
                      pl.BlockSpec((1, tk, LANE), lambda b, j: (b, 0, 0)),
                      pl.BlockSpec((1, LANE, tk), lambda b, j: (b, 0, 0)),
                      pl.BlockSpec((1, tk, LANE), lambda b, j: (b, 0, 0)),
                      pl.BlockSpec(tri.shape, lambda b, j: (0, 0))],
            out_specs=pl.BlockSpec((1, tq, ATT_WIDTH), lambda b, j: (b, j, 0)),
            out_shape=jax.ShapeDtypeStruct((bsz, per * tq, ATT_WIDTH), F32),
            scratch_shapes=[pltpu.VMEM((tk, tq), F32)],
            compiler_params=_cp(("parallel", "arbitrary")),
            name=f"attn_prompt_{tk}",
        )(q3, iq3, iwt, k3, vt, ik3, tri)
        outs.append(out)
    return jnp.concatenate(outs, axis=1).reshape(bsz * seq, ATT_WIDTH)


def _attn_sample_kernel(pt_ref, q_ref, iq_ref, misc_ref, kn_ref, vn_ref, ik4n_ref, *rest, npg, topk):
    del pt_ref
    kp, vp, ikp = rest[0:npg], rest[npg:2 * npg], rest[2 * npg:3 * npg]
    o_ref, key_sc, p_sc = rest[3 * npg:]
    tq = q_ref.shape[1]
    past = npg * PAGE_SIZE
    lk = past + LANE
    k_tile = lambda j: kp[j][0] if j < npg else _pad_rows(kn_ref[0], LANE).T
    v_tile = lambda j: vp[j][0] if j < npg else _pad_rows(vn_ref[0], LANE).T
    ik_tile = lambda j: ikp[j][0] if j < npg else _pad_rows(ik4n_ref[0], LANE).T[0:IDX_DIM, :]
    tile = lambda j: slice(j * LANE, (j + 1) * LANE)
    iq = iq_ref[0]
    iqs = _bf(jnp.concatenate([iq[:, h * IDX_DIM:(h + 1) * IDX_DIM] for h in range(IDX_HEADS)], axis=0))
    misc = misc_ref[0]
    iw = jnp.concatenate([misc[:, IDX_DIM + h:IDX_DIM + h + 1] for h in range(IDX_HEADS)], axis=0)
    iw = iw * (IDX_HEADS ** -0.5 * IDX_DIM ** -0.5)
    for j in range(npg + 1):
        wl = jnp.maximum(_dot(iqs, ik_tile(j)), 0.0) * iw
        score = (wl[0:tq] + wl[tq:2 * tq]) + (wl[2 * tq:3 * tq] + wl[3 * tq:4 * tq])
        if j == npg:
            score = jnp.where(_lane_iota((tq, LANE)) <= _row_iota((tq, LANE)), score, -FAR)
        key_sc[:, tile(j)] = score
    sel = _topk_select(key_sc, _lane_iota((tq, lk)), 1, (lk - 1).bit_length(), p_sc, topk)
    bias = jnp.where(sel > 0, 0.0, NEG)
    heads = ATT_HEADS // KV_HEADS
    q = q_ref[0] * HEAD_DIM ** -0.5
    qg = _bf(jnp.concatenate([_group_queries(q, g) for g in range(KV_HEADS)], axis=0))
    s = jnp.concatenate([_dot(qg, k_tile(j)) for j in range(npg + 1)], axis=1)
    s = s + jnp.concatenate([bias] * ATT_HEADS, axis=0)
    pr = jnp.exp(s - jnp.max(s, axis=-1, keepdims=True))
    o = None
    for j in range(npg + 1):
        part = _dot_nt(pr[:, tile(j)], v_tile(j))
        o = part if o is None else o + part
    o = o / jnp.sum(pr, axis=-1, keepdims=True)
    o_ref[0] = _ungroup_outputs([o[g * heads * tq:(g + 1) * heads * tq] for g in range(KV_HEADS)], tq)


def attn_sample(qr, kr, v, iqr, ik4, misc, cache_kt, cache_vt, cache_ikt, page_table, layer, dseq):
    db, npg = page_table.shape
    n_pool = cache_kt.shape[0] // DEPTH
    lk = npg * PAGE_SIZE + LANE
    topk = min(TOPK_MAX, (npg * PAGE_SIZE + dseq) // 4)
    base = layer * n_pool
    r3 = lambda a: a.reshape(db, dseq, a.shape[-1])
    row_spec = lambda w: pl.BlockSpec((1, dseq, w), lambda b, pt: (b, 0, 0))

    def page_spec(w, j):
        return pl.BlockSpec((1, w, PAGE_SIZE), lambda b, pt, j=j: (pt[b, j] + base, 0, 0))

    in_specs = [row_spec(ATT_WIDTH), row_spec(LANE), row_spec(LANE),
                row_spec(LANE), row_spec(LANE), row_spec(LANE)]
    in_specs += [page_spec(KVW, j) for j in range(npg)]
    in_specs += [page_spec(KVW, j) for j in range(npg)]
    in_specs += [page_spec(IDX_DIM, j) for j in range(npg)]
    out = pl.pallas_call(
        functools.partial(_attn_sample_kernel, npg=npg, topk=topk),
        grid_spec=pltpu.PrefetchScalarGridSpec(
            num_scalar_prefetch=1,
            grid=(db,),
            in_specs=in_specs,
            out_specs=pl.BlockSpec((1, dseq, ATT_WIDTH), lambda b, pt: (b, 0, 0)),
            scratch_shapes=[pltpu.VMEM((dseq, lk), F32), pltpu.VMEM((dseq, 1), jnp.int32)]),
        out_shape=jax.ShapeDtypeStruct((db, dseq, ATT_WIDTH), F32),
        compiler_params=_cp(("parallel",)),
        name="attn_sample",
    )(page_table, r3(qr), r3(iqr), r3(misc), r3(kr), r3(v), r3(ik4),
      *([cache_kt] * npg), *([cache_vt] * npg), *([cache_ikt] * npg))
    return out.reshape(db * dseq, ATT_WIDTH)


def _s5_param_kernel(ldt_ref, are_ref, aim_ref, bre_ref, bim_ref, abar_ref, win_ref):
    dt = jnp.exp(ldt_ref[...])
    a_re, a_im = are_ref[...], aim_ref[...]
    mag = jnp.exp(dt * a_re)
    abar_re = mag * jnp.cos(dt * a_im)
    abar_im = mag * jnp.sin(dt * a_im)
    den = a_re * a_re + a_im * a_im
    num_re = abar_re - 1.0
    coef_re = (num_re * a_re + abar_im * a_im) / den
    coef_im = (abar_im * a_re - num_re * a_im) / den
    abar_ref[:, 0:S5_NS] = jnp.broadcast_to(abar_re, (8, S5_NS))
    abar_ref[:, S5_NS:2 * S5_NS] = jnp.broadcast_to(abar_im, (8, S5_NS))
    b_re, b_im = bre_ref[...], bim_ref[...]
    win_ref[:, 0:S5_NS] = _bf(coef_re * b_re - coef_im * b_im)
    win_ref[:, S5_NS:2 * S5_NS] = _bf(coef_re * b_im + coef_im * b_re)


def _block_diag_in(b):
    eye = jnp.eye(S5_GROUPS, dtype=b.dtype)
    return jnp.einsum('gnc,gh->gchn', b, eye).reshape(S5_WIDTH, S5_NS)


def _block_diag_out(c):
    eye = jnp.eye(S5_GROUPS, dtype=c.dtype)
    return jnp.einsum('gcn,gh->gnhc', c, eye).reshape(S5_NS, S5_WIDTH)


def s5_params(log_dt, a_re, a_im, b_re, b_im):
    per_state = lambda a: a.reshape(1, S5_NS)
    ldt = per_state(jnp.broadcast_to(log_dt[:, None], (S5_GROUPS, S5_STATE)))
    return pl.pallas_call(
        _s5_param_kernel,
        out_shape=[jax.ShapeDtypeStruct((8, 2 * S5_NS), F32),
                   jax.ShapeDtypeStruct((S5_WIDTH, 2 * S5_NS), BF16)],
        compiler_params=pltpu.CompilerParams(vmem_limit_bytes=VMEM_LIMIT),
        name="s5_params",
    )(ldt, per_state(a_re), per_state(a_im), _block_diag_in(b_re), _block_diag_in(b_im))


def _gelu_tanh(x):
    return 0.5 * x * (1.0 + jnp.tanh(math.sqrt(2.0 / math.pi) * (x + 0.044715 * (x * x * x))))


S5_LANES = 512
S5_SLAB = (LANE // S5_GROUP_CH) * S5_STATE


def _s5_kernel(u_ref, abar_ref, win_ref, h0_ref, wout_ref, d_ref, wglu_ref, bglu_ref,
               y_ref, hn_ref, s_sc, *, bsz, tc):
    c = pl.program_id(0)

    @pl.when(c == 0)
    def _():
        hn_ref[...] = h0_ref[...]

    u = u_ref[...]
    ub = _bf(u)
    for kt in range(S5_WIDTH // LANE):
        ch = slice(kt * LANE, (kt + 1) * LANE)
        for part in range(2):
            st = slice(part * S5_NS + kt * S5_SLAB, part * S5_NS + (kt + 1) * S5_SLAB)
            s_sc[:, st] = jnp.dot(ub[:, ch], win_ref[ch, st], preferred_element_type=F32)
    nchunk = S5_NS // S5_LANES
    for rg in range(bsz // 8):
        rows = slice(rg * 8, rg * 8 + 8)

        def body(t, carry):
            row0 = pl.multiple_of(t * bsz + rg * 8, 8)
            new = []
            for cc in range(nchunk):
                lre = slice(cc * S5_LANES, (cc + 1) * S5_LANES)
                lim = slice(S5_NS + cc * S5_LANES, S5_NS + (cc + 1) * S5_LANES)
                xr, xi = carry[2 * cc], carry[2 * cc + 1]
                ar, ai = abar_ref[:, lre], abar_ref[:, lim]
                nr = (ar * xr - ai * xi) + s_sc[pl.ds(row0, 8), lre]
                ni = (ar * xi + ai * xr) + s_sc[pl.ds(row0, 8), lim]
                s_sc[pl.ds(row0, 8), lre] = nr
                s_sc[pl.ds(row0, 8), lim] = ni
                new += [nr, ni]
            return tuple(new)

        init = []
        for cc in range(nchunk):
            init += [hn_ref[rows, cc * S5_LANES:(cc + 1) * S5_LANES],
                     hn_ref[rows, S5_NS + cc * S5_LANES:S5_NS + (cc + 1) * S5_LANES]]
        fin = lax.fori_loop(0, tc, body, tuple(init))
        for cc in range(nchunk):
            hn_ref[rows, cc * S5_LANES:(cc + 1) * S5_LANES] = fin[2 * cc]
            hn_ref[rows, S5_NS + cc * S5_LANES:S5_NS + (cc + 1) * S5_LANES] = fin[2 * cc + 1]

    y_tiles = []
    for kt in range(S5_WIDTH // LANE):
        ch = slice(kt * LANE, (kt + 1) * LANE)
        re = slice(kt * S5_SLAB, (kt + 1) * S5_SLAB)
        im = slice(S5_NS + kt * S5_SLAB, S5_NS + (kt + 1) * S5_SLAB)
        y_tiles.append(jnp.dot(_bf(s_sc[:, re]), wout_ref[re, ch], preferred_element_type=F32)
                       - jnp.dot(_bf(s_sc[:, im]), wout_ref[im, ch], preferred_element_type=F32))
    y = jnp.concatenate(y_tiles, axis=1) + d_ref[...] * u
    z = _gelu_tanh(y)
    y_ref[...] = z * _sigmoid(jnp.dot(_bf(z), wglu_ref[...], preferred_element_type=F32) + bglu_ref[...])


def s5_branch(u_tm, h0, abar8, win, wout_bf, d, wglu_bf, bglu, bsz, seq, tc):
    rows = tc * bsz
    const = lambda shape: pl.BlockSpec(shape, lambda c: (0,) * len(shape))
    return pl.pallas_call(
        functools.partial(_s5_kernel, bsz=bsz, tc=tc),
        grid=(seq // tc,),
        in_specs=[pl.BlockSpec((rows, S5_WIDTH), lambda c: (c, 0)),
                  const((8, 2 * S5_NS)), const((S5_WIDTH, 2 * S5_NS)), const((bsz, 2 * S5_NS)),
                  const((2 * S5_NS, S5_WIDTH)), const((1, S5_WIDTH)), const((S5_WIDTH, S5_WIDTH)),
                  const((1, S5_WIDTH))],
        out_specs=[pl.BlockSpec((rows, S5_WIDTH), lambda c: (c, 0)), const((bsz, 2 * S5_NS))],
        out_shape=[jax.ShapeDtypeStruct((seq * bsz, S5_WIDTH), F32),
                   jax.ShapeDtypeStruct((bsz, 2 * S5_NS), F32)],
        scratch_shapes=[pltpu.VMEM((rows, 2 * S5_NS), F32)],
        compiler_params=_cp(("arbitrary",)),
        name="s5_branch",
    )(u_tm, abar8, win, h0, wout_bf, d.reshape(1, S5_WIDTH), wglu_bf, bglu.reshape(1, S5_WIDTH))


def _pad_rows(x, rows):
    n = x.shape[0]
    return x if n == rows else jnp.concatenate([x, jnp.zeros((rows - n, x.shape[1]), x.dtype)], axis=0)


def _head_mean(x, amat_bf):
    return _dot2(x, amat_bf)


def _split3(x):
    a = x.astype(BF16)
    r = x - a.astype(F32)
    b = r.astype(BF16)
    return a, b, (r - b.astype(F32)).astype(BF16)


def _expand_state(tall, rep_bf, bdm):
    d = functools.partial(jnp.dot, preferred_element_type=F32)
    a, b, c = _split3(tall)
    return (d(a, rep_bf) + (d(b, rep_bf) + d(c, rep_bf))) * bdm


def _collapse_state(bd, rept_bf):
    d = functools.partial(jnp.dot, preferred_element_type=F32)
    a, b, c = _split3(bd)
    return d(a, rept_bf) + (d(b, rept_bf) + d(c, rept_bf))


def _rep_consts():
    rep = np.tile(np.eye(64, dtype=np.float32), (1, BW // 64))
    return jnp.asarray(rep, BF16), jnp.asarray(rep.T, BF16)


def _ret_kernel(q_ref, k_ref, v_ref, g_ref, tab_ref, s0_ref, dmat_ref, qdec_ref, kdec_ref, decm_ref,
                bdm_ref, amat_ref, rep_ref, rept_ref, y_ref, so_ref, s_ref, *, rows):
    n = q_ref.shape[0]

    @pl.when(pl.program_id(1) == 0)
    def _():
        s_ref[...] = _expand_state(s0_ref[0], rep_ref[...], bdm_ref[...])

    q = _pad_rows(_rope(q_ref[...], tab_ref, RET_DK // 2), rows)
    k = _pad_rows(_rope(k_ref[...], tab_ref, RET_DK // 2) * RET_DK ** -0.5, rows)
    v = _pad_rows(v_ref[...], rows)
    state = s_ref[...]
    inter = _dot(q * qdec_ref[...], state)
    lane = _lane_iota((rows, LANE))
    tiles = []
    for p in range(RET_HEADS // 2):
        lanes = slice(p * LANE, (p + 1) * LANE)
        qp, kp, vp = q[:, lanes], _bf(k[:, lanes]), v[:, lanes]
        acc = None
        for hh in range(2):
            mine = (lane < RET_DK) if hh == 0 else (lane >= RET_DK)
            s = _dot_nt(jnp.where(mine, qp, 0.0), kp) * dmat_ref[2 * p + hh]
            part = _dot(s, jnp.where(mine, vp, 0.0))
            acc = part if acc is None else acc + part
        tiles.append(acc)
    o = jnp.concatenate(tiles, axis=1) + inter
    new_state = state * decm_ref[...] + _dot_tn(k * kdec_ref[...], v) * bdm_ref[...]
    s_ref[...] = new_state

    @pl.when(pl.program_id(1) == pl.num_programs(1) - 1)
    def _():
        so_ref[0] = _collapse_state(new_state, rept_ref[...])

    amat = amat_ref[...]
    mu = _head_mean(o, amat)
    d = o - mu
    var = _head_mean(d * d, amat)
    on = d * lax.rsqrt(var + 1e-5)
    y_ref[...] = (_silu(g_ref[...]) * on[0:n]).astype(y_ref.dtype)


def _head_block_mask():
    h = np.arange(BW) // 64
    return (h[:, None] == h[None, :]).astype(np.float32)


def _ret_consts(rows, n_true):
    lg = np.log(1.0 - np.exp2(-5.0 - np.arange(RET_HEADS, dtype=np.float64)))
    i = np.arange(rows, dtype=np.float64)
    rel = i[:, None] - i[None, :]
    dmat = np.where(rel[None] >= 0, np.exp(np.minimum(rel[None], rows) * lg[:, None, None]), 0.0)
    lane_lg = np.repeat(lg, 64)[None, :]
    qdec = np.exp((i[:, None] + 1.0) * lane_lg)
    kdec = np.where(i[:, None] < n_true, np.exp((n_true - 1.0 - i[:, None]) * lane_lg), 0.0)
    bdm = _head_block_mask()
    decm = bdm * np.exp(n_true * np.repeat(lg, 64))[:, None]
    f = lambda a: jnp.asarray(a, F32)
    return f(dmat), f(qdec), f(kdec), f(decm), f(bdm), jnp.asarray(bdm / 64.0, BF16)


def ret_branch(proj, tab_r, s0, bsz, seq, rows, n):
    nch = seq // n
    dmat, qdec, kdec, decm, bdm, amat = _ret_consts(rows, n)
    rep, rept = _rep_consts()
    col = lambda j: pl.BlockSpec((n, BW), lambda b, c, j=j: (b * nch + c, j))
    const = lambda shape: pl.BlockSpec(shape, lambda b, c: (0,) * len(shape))
    return pl.pallas_call(
        functools.partial(_ret_kernel, rows=rows),
        grid=(bsz, nch),
        in_specs=[col(0), col(1), col(2), col(3),
                  pl.BlockSpec((3, n, LANE), lambda b, c: (0, c, 0)),
                  pl.BlockSpec((1, BW, RET_DK), lambda b, c: (b, 0, 0)),
                  const((RET_HEADS, rows, rows)), const((rows, BW)), const((rows, BW)),
                  const((BW, BW)), const((BW, BW)), const((BW, BW)), const((RET_DK, BW)), const((BW, RET_DK))],
        out_specs=[pl.BlockSpec((n, BW), lambda b, c: (b * nch + c, 0)),
                   pl.BlockSpec((1, BW, RET_DK), lambda b, c: (b, 0, 0))],
        out_shape=[jax.ShapeDtypeStruct((bsz * seq, BW), F32),
                   jax.ShapeDtypeStruct((bsz, BW, RET_DK), F32)],
        scratch_shapes=[pltpu.VMEM((BW, BW), F32)],
        compiler_params=_cp(("parallel", "arbitrary")),
        name="ret_branch",
    )(proj, proj, proj, proj, tab_r, s0, dmat, qdec, kdec, decm, bdm, amat, rep, rept)


def _softplus(x):
    return jnp.maximum(x, 0.0) + jnp.log(1.0 + jnp.exp(-jnp.abs(x)))


GDN_SUPER = ((0, 4 * GDN_CHUNK), (4 * GDN_CHUNK, 6 * GDN_CHUNK))
GM_BLOCK, GM_INCL, GM_STRICT, GM_EYE, GM_LEVEL0 = 0, 1, 2, 3, 4


def _stack_heads(a, bdm_rows):
    return jnp.concatenate([a] * (bdm_rows.shape[0] // GDN_CHUNK), axis=0) * bdm_rows


def _unstack_heads(parts):
    blocks = [p[i:i + GDN_CHUNK] for p in parts for i in range(0, p.shape[0], GDN_CHUNK)]
    out = blocks[0]
    for b in blocks[1:]:
        out = out + b
    return out


def _col_of_heads(a, s, e):
    return jnp.concatenate([jnp.broadcast_to(a[:, h * GDN_DK:h * GDN_DK + 1], (GDN_CHUNK, e - s))
                            for h in range(s // GDN_CHUNK, e // GDN_CHUNK)], axis=0)


def _row_of_heads(a_t, s, e):
    return jnp.concatenate([a_t[h * GDN_DK:h * GDN_DK + 1, :] for h in range(s // GDN_CHUNK, e // GDN_CHUNK)],
                           axis=1)


def _gdn_prepare(chunks, gm_ref, n_real):
    items = [(ci, s, e) for ci in range(len(chunks)) for s, e in GDN_SUPER]
    g_ts = [gc.T for _, _, _, _, gc in chunks]
    nmats, decs, qks = [], [], []
    for ci, s, e in items:
        qc, kc, _, bc, gc = chunks[ci]
        bdm_rows = gm_ref[GM_BLOCK, s:e, :]
        ks = _bf(_stack_heads(kc, bdm_rows))
        kk = _dot_nt(ks, ks)
        qks.append(_dot_nt(_stack_heads(qc, bdm_rows), ks))
        diff = _col_of_heads(gc, s, e) - _row_of_heads(g_ts[ci], s, e)
        dec = jnp.exp(jnp.where(gm_ref[GM_INCL, s:e, s:e] > 0.5, diff, NEG))
        decs.append(dec)
        nmats.append(_col_of_heads(bc, s, e) * (dec * gm_ref[GM_STRICT, s:e, s:e]) * kk)
    invs = [gm_ref[GM_EYE, s:e, s:e] - nm * gm_ref[GM_LEVEL0, s:e, s:e] for nm, (_, s, e) in zip(nmats, items)]
    for lvl in range(1, (min(n_real, GDN_CHUNK) - 1).bit_length()):
        right = [_dot(nm * gm_ref[GM_LEVEL0 + lvl, s:e, s:e], inv) for nm, inv, (_, s, e) in zip(nmats, invs, items)]
        invs = [inv - _dot(inv, r) for inv, r in zip(invs, right)]
    w_st, uv_st = [[] for _ in chunks], [[] for _ in chunks]
    for inv, (ci, s, e) in zip(invs, items):
        _, kc, vc, bc, gc = chunks[ci]
        bdm_rows = gm_ref[GM_BLOCK, s:e, :]
        wu = _dot(inv, jnp.concatenate([_stack_heads(bc * jnp.exp(gc) * kc, bdm_rows),
                                        _stack_heads(bc * vc, bdm_rows)], axis=1))
        w_st[ci].append(wu[:, 0:BW])
        uv_st[ci].append(wu[:, BW:2 * BW])
    n_sb = len(GDN_SUPER)
    return [(_unstack_heads(w_st[ci]), _unstack_heads(uv_st[ci]),
             [qks[ci * n_sb + j] * decs[ci * n_sb + j] for j in range(n_sb)]) for ci in range(len(chunks))]


def _gdn_apply(chunk, prepared, hbd, gm_ref):
    qc, kc, _, _, gc = chunk
    w, uv, a_mats = prepared
    g_last = gc[GDN_CHUNK - 1:GDN_CHUNK, :]
    u = uv - _dot(w, hbd)
    o_st = [_dot(a_mats[i], _stack_heads(u, gm_ref[GM_BLOCK, s:e, :])) for i, (s, e) in enumerate(GDN_SUPER)]
    o = jnp.exp(gc) * _dot(qc, hbd) + _unstack_heads(o_st)
    h_new = jnp.exp(g_last) * hbd + _dot_tn(kc * jnp.exp(g_last - gc), u) * gm_ref[GM_BLOCK]
    return o, h_new


def _gdn_kernel(q_ref, k_ref, v_ref, g_ref, misc_ref, cs0_ref, cw_ref, alog_ref, dtb_ref, ng_ref, h0_ref,
                ea_ref, eb_ref, gm_ref, tri_ref, amat_ref, rep_ref, rept_ref, y_ref, ho_ref, cs_ref,
                xp_sc, h_sc, *, rows):
    n = q_ref.shape[0]
    cw = 3 * BW

    @pl.when(pl.program_id(1) == 0)
    def _():
        xp_sc[...] = jnp.zeros(xp_sc.shape, F32)
        xp_sc[5:8, :] = cs0_ref[0]
        h_sc[...] = _expand_state(h0_ref[0], rep_ref[...], gm_ref[GM_BLOCK])

    for j, r in enumerate((q_ref, k_ref, v_ref)):
        xp_sc[8:8 + n, j * BW:(j + 1) * BW] = r[...]
    conv = xp_sc[5:5 + rows, :] * cw_ref[0:1, :]
    for i in range(1, GDN_CONV):
        conv = conv + xp_sc[5 + i:5 + i + rows, :] * cw_ref[i:i + 1, :]
    tail = xp_sc[8 + n - 3:8 + n, :]
    xp_sc[5:8, :] = tail
    cs_ref[0] = tail
    xc = _silu(conv)
    valid = _row_iota((rows, BW)) < n
    bdm_bf = _bf(gm_ref[GM_BLOCK])
    q, k, v = xc[:, 0:BW], xc[:, BW:2 * BW], xc[:, 2 * BW:cw]
    q = q * lax.rsqrt(_dot2(q * q, bdm_bf) + EPS) * GDN_DK ** -0.5
    k = k * lax.rsqrt(_dot2(k * k, bdm_bf) + EPS)
    misc = _pad_rows(misc_ref[...], rows)
    beta = _sigmoid(_dot2(misc, eb_ref[...]))
    la = -jnp.exp(alog_ref[...]) * _softplus(_dot2(misc, ea_ref[...]) + dtb_ref[...])
    k = jnp.where(valid, k, 0.0)
    v = jnp.where(valid, v, 0.0)
    la = jnp.where(valid, la, 0.0)
    la_hi, la_lo = _split(la)
    tri = tri_ref[...]
    gall = (jnp.dot(tri, la_hi, preferred_element_type=F32)
            + jnp.dot(tri, la_lo, preferred_element_type=F32))
    c = GDN_CHUNK
    chunks = [(q[r], k[r], v[r], beta[r], gall[r]) for r in (slice(i, i + c) for i in range(0, rows, c))]
    prepared = _gdn_prepare(chunks, gm_ref, n)
    outs = []
    hbd = h_sc[...]
    for chunk, prep in zip(chunks, prepared):
        o, hbd = _gdn_apply(chunk, prep, hbd, gm_ref)
        outs.append(o)
    h_sc[...] = hbd

    @pl.when(pl.program_id(1) == pl.num_programs(1) - 1)
    def _():
        ho_ref[0] = _collapse_state(hbd, rept_ref[...])

    o = outs[0] if len(outs) == 1 else jnp.concatenate(outs, axis=0)
    on = o * lax.rsqrt(_dot2(o * o, amat_ref[...]) + EPS) * ng_ref[...]
    y_ref[...] = on[0:n] * _silu(g_ref[...])


def _gdn_consts(rows):
    lanes = np.arange(BW) // 64
    ea = np.zeros((LANE, BW), np.float32)
    eb = np.zeros((LANE, BW), np.float32)
    ea[MISC_DA + lanes, np.arange(BW)] = 1.0
    eb[MISC_DB + lanes, np.arange(BW)] = 1.0
    i = np.arange(rows)
    tri = ((i[:, None] // GDN_CHUNK == i[None, :] // GDN_CHUNK) & (i[:, None] >= i[None, :])).astype(np.float32)
    bdm = _head_block_mask()
    r = np.arange(BW)
    ri, ci = r[:, None] % GDN_CHUNK, r[None, :] % GDN_CHUNK
    gm = [bdm, bdm * (ri >= ci), bdm * (ri > ci), np.eye(BW, dtype=np.float32)]
    s = 1
    while s < GDN_CHUNK:
        gm.append(bdm * ((ri // (2 * s)) == (ci // (2 * s))) * ((ri // s) % 2 == 1) * ((ci // s) % 2 == 0))
        s *= 2
    return (jnp.asarray(ea, BF16), jnp.asarray(eb, BF16), jnp.asarray(np.stack(gm), F32), jnp.asarray(tri, BF16),
            jnp.asarray(bdm / 64.0, BF16))


def gdn_branch(proj, misc, cs0, conv_w, a_log, dt_bias, norm_g, h0, bsz, seq, rows, n):
    nblk = seq // n
    ea, eb, gm, tri, amat = _gdn_consts(rows)
    rep, rept = _rep_consts()
    per_lane = lambda a, reps: jnp.repeat(a, reps).reshape(1, BW) if reps > 1 else jnp.tile(a, BW // a.shape[0]).reshape(1, BW)
    col = lambda j: pl.BlockSpec((n, BW), lambda b, c, j=j: (b * nblk + c, j))
    const = lambda shape: pl.BlockSpec(shape, lambda b, c: (0,) * len(shape))
    per_b = lambda shape: pl.BlockSpec(shape, lambda b, c: (b,) + (0,) * (len(shape) - 1))
    cw = 3 * BW
    return pl.pallas_call(
        functools.partial(_gdn_kernel, rows=rows),
        grid=(bsz, nblk),
        in_specs=[col(4), col(5), col(6), col(7),
                  pl.BlockSpec((n, LANE), lambda b, c: (b * nblk + c, 0)),
                  per_b((1, GDN_CONV - 1, cw)), const((GDN_CONV, cw)),
                  const((1, BW)), const((1, BW)), const((1, BW)), per_b((1, BW, GDN_DK)),
                  const((LANE, BW)), const((LANE, BW)), const(tuple(gm.shape)), const((rows, rows)), const((BW, BW)),
                  const((GDN_DK, BW)), const((BW, GDN_DK))],
        out_specs=[pl.BlockSpec((n, BW), lambda b, c: (b * nblk + c, 0)),
                   per_b((1, BW, GDN_DK)), per_b((1, GDN_CONV - 1, cw))],
        out_shape=[jax.ShapeDtypeStruct((bsz * seq, BW), F32),
                   jax.ShapeDtypeStruct((bsz, BW, GDN_DK), F32),
                   jax.ShapeDtypeStruct((bsz, GDN_CONV - 1, cw), F32)],
        scratch_shapes=[pltpu.VMEM((rows + 8, cw), F32), pltpu.VMEM((BW, BW), F32)],
        compiler_params=_cp(("parallel", "arbitrary")),
        name="gdn_branch",
    )(proj, proj, proj, proj, misc, cs0, conv_w, per_lane(a_log, 64), per_lane(dt_bias, 64),
      per_lane(norm_g, 1), h0, ea, eb, gm, tri, amat, rep, rept)


_BRANCH_OFFS = (0, ATT_WIDTH, ATT_WIDTH + BW, ATT_WIDTH + 2 * BW, MIX_WIDTH)


def _merge_kernel(x_ref, g_ref, sc_ref, sh_ref, gm_ref, ya_ref, yb_ref, yc_ref, yd_ref,
                  wg_ref, wb_ref, wo_ref, o_ref):
    x = x_ref[...]
    h = _bf(_norm_mod(x, g_ref[...], sc_ref[0], sh_ref[0]))
    merged = None
    for b, y_ref in enumerate((ya_ref, yb_ref, yc_ref, yd_ref)):
        gate = _sigmoid(jnp.dot(h, wg_ref[:, b * D_MODEL:(b + 1) * D_MODEL], preferred_element_type=F32))
        term = gate * jnp.dot(_bf(y_ref[...]), wb_ref[_BRANCH_OFFS[b]:_BRANCH_OFFS[b + 1], :],
                              preferred_element_type=F32)
        merged = term if merged is None else merged + term
    y = jnp.dot(_bf(merged), wo_ref[...], preferred_element_type=F32)
    o_ref[...] = x + gm_ref[0] * y


def merge_out(x, g, sc, sh, gm, ya, yb, yc, yd, wg_bf, wb_bf, wo_bf, tm, tiles_per_seq):
    m = x.shape[0]
    ms = _mod_spec(tm, sc.shape[1], tiles_per_seq)
    row = lambda w: pl.BlockSpec((tm, w), lambda i: (i, 0))
    const = lambda shape: pl.BlockSpec(shape, lambda i: (0,) * len(shape))
    return pl.pallas_call(
        _merge_kernel,
        grid=(m // tm,),
        in_specs=[row(D_MODEL), const((1, D_MODEL)), ms, ms, ms, row(ATT_WIDTH), row(BW), row(BW), row(BW),
                  const((D_MODEL, N_BRANCH * D_MODEL)), const((MIX_WIDTH, D_MODEL)), const((D_MODEL, D_MODEL))],
        out_specs=row(D_MODEL),
        out_shape=jax.ShapeDtypeStruct((m, D_MODEL), F32),
        compiler_params=_cp(("parallel",)),
        name="merge_out",
    )(x, g.reshape(1, D_MODEL), sc, sh, gm, ya, yb, yc, yd, wg_bf, wb_bf, wo_bf)


def _top2(masked, lane):
    m1 = jnp.max(masked, axis=-1, keepdims=True)
    i1 = jnp.min(jnp.where(masked == m1, lane, LANE), axis=-1, keepdims=True)
    rest = jnp.where(lane == i1, -jnp.inf, masked)
    m2 = jnp.max(rest, axis=-1, keepdims=True)
    i2 = jnp.min(jnp.where(rest == m2, lane, LANE), axis=-1, keepdims=True)
    return m1, i1, m2, i2


def _route(scores, biased):
    lane = _lane_iota(scores.shape)
    grp = lane // EXPERTS_PER_GROUP
    best_val, best_grp = None, None
    for g in range(N_GROUPS):
        m1, _, m2, _ = _top2(jnp.where(grp == g, biased, -jnp.inf), lane)
        gs = m1 + m2
        if g == 0:
            best_val, best_grp = gs, jnp.zeros(gs.shape, jnp.int32)
        else:
            better = gs > best_val
            best_val = jnp.where(better, gs, best_val)
            best_grp = jnp.where(better, g, best_grp)
    _, e1, _, e2 = _top2(jnp.where(grp == best_grp, biased, -jnp.inf), lane)
    s1 = jnp.sum(jnp.where(lane == e1, scores, 0.0), axis=-1, keepdims=True)
    s2 = jnp.sum(jnp.where(lane == e2, scores, 0.0), axis=-1, keepdims=True)
    tot = s1 + s2
    return jnp.where(lane == e1, s1 / tot, 0.0) + jnp.where(lane == e2, s2 / tot, 0.0), best_grp


MOE_BLOCK = 256
MOE_ALIGN = 16


def _moe_kernel(x_ref, g_ref, sc_ref, sh_ref, gm_ref, wr_ref, rb_ref, tri_ref, upper_ref, w1_ref, w3_ref, w2_ref,
                fg_ref, o_ref, hs_sc, comb_sc, acc_sc, pt_sc, seg_sc, *, final):
    grp_id = pl.program_id(1)
    tm = x_ref.shape[0]

    @pl.when(grp_id == 0)
    def _():
        h = _norm_mod(x_ref[...], g_ref[...], sc_ref[0], sh_ref[0])
        scores = _sigmoid(_dot3(h, wr_ref[...]))
        comb, best = _route(scores, scores + rb_ref[...])
        lane = _lane_iota((tm, LANE))
        onehot = jnp.where(lane == best, 1.0, 0.0)
        incl = jnp.dot(tri_ref[...], _bf(onehot), preferred_element_type=F32)
        counts = incl[tm - 8:tm, :]
        offs = _dot2(counts, upper_ref[...])
        rank = jnp.sum(onehot * (offs[7:8, :] + incl), axis=-1, keepdims=True) - 1.0
        perm_t = jnp.where(_lane_iota((tm, tm)).astype(F32) == rank, 1.0, 0.0).astype(BF16)
        pt_sc[...] = perm_t
        hs_sc[0:tm, :] = _dot_tn(perm_t, _bf(h)).astype(BF16)
        hs_sc[tm:tm + MOE_BLOCK, :] = jnp.zeros((MOE_BLOCK, D_MODEL), BF16)
        ca, cb, cc = _split3(comb)
        tn = functools.partial(lax.dot_general, dimension_numbers=(((0,), (0,)), ((), ())),
                               preferred_element_type=F32)
        comb_sc[0:tm, :] = tn(perm_t, ca) + (tn(perm_t, cb) + tn(perm_t, cc))
        comb_sc[tm:tm + MOE_BLOCK, :] = jnp.zeros((MOE_BLOCK, LANE), F32)
        acc_sc[...] = jnp.zeros(acc_sc.shape, F32)
        for gi in range(N_GROUPS):
            seg_sc[gi] = offs[7, gi].astype(jnp.int32)
            seg_sc[N_GROUPS + gi] = counts[7, gi].astype(jnp.int32)

    off = seg_sc[grp_id]
    cnt = seg_sc[N_GROUPS + grp_id]
    start = (off // MOE_ALIGN) * MOE_ALIGN
    nblk = jnp.where(cnt > 0, (off + cnt - start + MOE_BLOCK - 1) // MOE_BLOCK, 0)

    def block(i, carry):
        r0 = pl.multiple_of(start + i * MOE_BLOCK, MOE_ALIGN)
        hb = hs_sc[pl.ds(r0, MOE_BLOCK), :]
        cblk = comb_sc[pl.ds(r0, MOE_BLOCK), :]
        lane = _lane_iota(cblk.shape)
        out = None
        for e in range(EXPERTS_PER_GROUP):
            ce = jnp.sum(jnp.where(lane == grp_id * EXPERTS_PER_GROUP + e, cblk, 0.0), axis=-1, keepdims=True)
            hid = (_silu(jnp.dot(hb, w1_ref[0, e], preferred_element_type=F32))
                   * jnp.dot(hb, w3_ref[0, e], preferred_element_type=F32))
            part = jnp.dot(_bf(hid * ce), w2_ref[0, e], preferred_element_type=F32)
            out = part if out is None else out + part
        acc_sc[pl.ds(r0, MOE_BLOCK), :] += out
        return carry

    lax.fori_loop(0, nblk, block, 0)

    @pl.when(grp_id == pl.num_programs(1) - 1)
    def _():
        d = functools.partial(jnp.dot, preferred_element_type=F32)
        a, b = _split(acc_sc[0:tm, :])
        perm_t = pt_sc[...]
        out = x_ref[...] + gm_ref[0] * (d(perm_t, a) + d(perm_t, b))
        if final:
            out = out * lax.rsqrt(jnp.mean(out * out, axis=-1, keepdims=True) + EPS) * fg_ref[...]
        o_ref[...] = out


def moe_out(x, g, sc, sh, gm, wr_pad, rb_pad, w1_bf, w3_bf, w2_bf, final_g, final, tm, tiles_per_seq):
    m = x.shape[0]
    ms = _mod_spec(tm, sc.shape[1], tiles_per_seq)
    const = lambda shape: pl.BlockSpec(shape, lambda i, e: (0,) * len(shape))
    r = np.arange(tm)
    tri = jnp.asarray(r[:, None] >= r[None, :], BF16)
    u = np.arange(LANE)
    upper = jnp.asarray(u[:, None] < u[None, :], BF16)
    grouped = lambda w: w.reshape((N_GROUPS, EXPERTS_PER_GROUP) + w.shape[1:])
    wspec = lambda a, b: pl.BlockSpec((1, EXPERTS_PER_GROUP, a, b), lambda i, e: (e, 0, 0, 0))
    return pl.pallas_call(
        functools.partial(_moe_kernel, final=final),
        grid=(m // tm, N_GROUPS),
        in_specs=[pl.BlockSpec((tm, D_MODEL), lambda i, e: (i, 0)), const((1, D_MODEL)), ms, ms, ms,
                  const((D_MODEL, LANE)), const((1, LANE)), const((tm, tm)), const((LANE, LANE)),
                  wspec(D_MODEL, EXPERT_FF), wspec(D_MODEL, EXPERT_FF), wspec(EXPERT_FF, D_MODEL),
                  const((1, D_MODEL))],
        out_specs=pl.BlockSpec((tm, D_MODEL), lambda i, e: (i, 0)),
        out_shape=jax.ShapeDtypeStruct((m, D_MODEL), F32),
        scratch_shapes=[pltpu.VMEM((tm + MOE_BLOCK, D_MODEL), BF16), pltpu.VMEM((tm + MOE_BLOCK, LANE), F32),
                        pltpu.VMEM((tm + MOE_BLOCK, D_MODEL), F32), pltpu.VMEM((tm, tm), BF16),
                        pltpu.SMEM((2 * N_GROUPS,), jnp.int32)],
        compiler_params=_cp(("parallel", "arbitrary")),
        name="moe_out",
    )(x, g.reshape(1, D_MODEL), sc, sh, gm, wr_pad, rb_pad, tri, upper, grouped(w1_bf), grouped(w3_bf),
      grouped(w2_bf), final_g.reshape(1, D_MODEL))


_REF_SPLITS = (ATT_WIDTH, KVW, KVW, IDX_HEADS * IDX_DIM, IDX_DIM, IDX_HEADS, BW,
               BW, BW, BW, BW, BW, BW, BW, GDN_HEADS, GDN_HEADS, BW)


def pack_w_in(w_in):
    offs = np.concatenate([[0], np.cumsum(_REF_SPLITS)])
    seg = [w_in[:, int(offs[i]):int(offs[i + 1])] for i in range(len(_REF_SPLITS))]
    (aq, ak, av, aiq, aik, aiw, bu, cq, ck, cv, cg, dq, dk, dv, da, db, dg) = seg
    zeros = lambda n: jnp.zeros((D_MODEL, n), w_in.dtype)
    misc = jnp.concatenate([aik, aiw, da, db, zeros(LANE - IDX_DIM - IDX_HEADS - 2 * GDN_HEADS)], axis=1)
    packed = jnp.concatenate([cq, ck, cv, cg, dq, dk, dv, dg, bu, aq, ak, av, aiq, zeros(2 * LANE)], axis=1)
    wt = jnp.concatenate([aiw.T, jnp.zeros((8 - IDX_HEADS, D_MODEL), w_in.dtype)], axis=0)
    return _bf(packed), misc, wt


def _time_major(a, bsz, seq):
    return a.reshape(bsz, seq, a.shape[-1]).transpose(1, 0, 2).reshape(seq * bsz, a.shape[-1])


def _batch_major(a, bsz, seq):
    return a.reshape(seq, bsz, a.shape[-1]).transpose(1, 0, 2).reshape(bsz * seq, a.shape[-1])


def _trunk_layer(x, mods, geom, attend, st, lw, final_g, final):
    bsz, seq, tm, tps, s5_tc, ret_rows, ret_n, gdn_rows, gdn_n = geom
    sh1, sc1, g1, sh2, sc2, g2 = mods
    proj, misc, iwt = in_proj(x, lw['norm1'], sc1, sh1, lw['w_in'], lw['w_misc'], lw['w_iwt'], tm, tps)
    qr, kr, v, iqr, ikr, ik4, vt = attn_prep(proj, misc, lw['tab_a'], lw['tab_i'], lw['tab_k'], tm, tps)
    ya = attend(qr, kr, v, vt, iqr, ikr, ik4, iwt, misc)
    u_tm = _time_major(proj[:, 8 * BW:9 * BW], bsz, seq)
    y_tm, s5_h = s5_branch(u_tm, st['s5'], lw['s5_abar'], lw['s5_win'], lw['s5_wout'], lw['s5_d'],
                           lw['s5_w_glu'], lw['s5_b_glu'], bsz, seq, s5_tc)
    yb = _batch_major(y_tm, bsz, seq)
    yc, ret_s = ret_branch(proj, lw['tab_r'], st['ret'], bsz, seq, ret_rows, ret_n)
    yd, gdn_s, conv_s = gdn_branch(proj, misc, st['conv'], lw['gdn_conv_w'], lw['gdn_a_log'], lw['gdn_dt_bias'],
                                   lw['gdn_norm_g'], st['gdn'], bsz, seq, gdn_rows, gdn_n)
    x = merge_out(x, lw['norm1'], sc1, sh1, g1, ya, yb, yc, yd, lw['w_gate'], lw['w_br'], lw['w_out'], tm, tps)
    x = moe_out(x, lw['norm2'], sc2, sh2, g2, lw['w_router'], lw['router_bias'], lw['w_e1'], lw['w_e3'],
                lw['w_e2'], final_g, final, tm, tps)
    new_st = {'k': kr, 'v': v, 'ik': ikr, 's5': s5_h, 'ret': ret_s, 'gdn': gdn_s, 'conv': conv_s}
    return x, new_st


def kernel(x_prompt, x_sample, c_prompt, c_sample, cache_k, cache_v, cache_idx_k, page_table,
           state_s5_re, state_s5_im, state_ret, state_gdn, state_gdn_conv,
           norm1_g, norm2_g, final_g, w_ada, b_ada, w_in,
           s5_a_re, s5_a_im, s5_b_re, s5_b_im, s5_c_re, s5_c_im, s5_d, s5_log_dt, s5_w_glu, s5_b_glu,
           gdn_conv_w, gdn_a_log, gdn_dt_bias, gdn_norm_g,
           w_br, w_gate, w_out, w_router, router_bias, w_e1, w_e3, w_e2):
    bsz, seq, _ = x_prompt.shape
    dbs, dseq, _ = x_sample.shape
    depth = w_in.shape[0]
    n_pool = cache_k.shape[1]
    past = page_table.shape[1] * PAGE_SIZE
    mp, ms = bsz * seq, dbs * dseq
    tm_p = 512
    pos_p = jnp.arange(seq, dtype=jnp.int32)
    pos_s = past + jnp.arange(dseq, dtype=jnp.int32)
    pos_s_tok = jnp.tile(pos_s, dbs)

    def tables(pos):
        return {'tab_a': _rope_tables(pos, ROT_DIMS, ROPE_THETA, HEAD_DIM),
                'tab_i': _rope_tables(pos, IDX_ROT, ROPE_THETA, IDX_DIM),
                'tab_k': _rope_tables(pos, IDX_ROT, ROPE_THETA, IDX_DIM, active=IDX_DIM)}

    tabs_p = dict(tables(pos_p), tab_r=_rope_tables(pos_p, RET_DK, RET_THETA, RET_DK))
    tabs_s = dict(tables(pos_s_tok), tab_r=_rope_tables(pos_s, RET_DK, RET_THETA, RET_DK))
    ck = cache_k.transpose(0, 1, 3, 4, 2).reshape(depth * n_pool, KVW, PAGE_SIZE)
    cv = cache_v.transpose(0, 1, 3, 4, 2).reshape(depth * n_pool, KVW, PAGE_SIZE)
    cik = cache_idx_k.transpose(0, 1, 3, 2).reshape(depth * n_pool, IDX_DIM, PAGE_SIZE)
    wr_pad = jnp.pad(w_router, ((0, 0), (0, LANE - N_EXPERTS)))
    rb_pad = jnp.pad(router_bias, (0, LANE - N_EXPERTS)).reshape(1, LANE)
    c_all = jnp.concatenate([c_prompt, c_sample], axis=0)

    geom_p = (bsz, seq, tm_p, seq // tm_p, 128, 256, 256, 256, 256)
    geom_s = (dbs, dseq, ms, 1, dseq, LANE, dseq, GDN_CHUNK, dseq)
    zero_st = {'s5': jnp.zeros((bsz, 2 * S5_NS), F32), 'ret': jnp.zeros((bsz, BW, RET_DK), F32),
               'gdn': jnp.zeros((bsz, BW, GDN_DK), F32), 'conv': jnp.zeros((bsz, GDN_CONV - 1, 3 * BW), F32)}

    xp = x_prompt.reshape(mp, D_MODEL)
    xs = x_sample.reshape(ms, D_MODEL)
    outs_p, outs_s = [], []
    for l in range(depth):
        w_in_p, w_misc, w_iwt = pack_w_in(w_in[l])
        abar8, win = s5_params(s5_log_dt[l], s5_a_re[l], s5_a_im[l], s5_b_re[l], s5_b_im[l])
        lw = {'norm1': norm1_g[l], 'norm2': norm2_g[l], 'w_in': w_in_p, 'w_misc': w_misc, 'w_iwt': w_iwt,
              's5_abar': abar8, 's5_win': win,
              's5_wout': _bf(jnp.concatenate([_block_diag_out(s5_c_re[l]), _block_diag_out(s5_c_im[l])], axis=0)),
              's5_d': s5_d[l], 's5_w_glu': _bf(s5_w_glu[l]), 's5_b_glu': s5_b_glu[l],
              'gdn_conv_w': gdn_conv_w[l], 'gdn_a_log': gdn_a_log[l], 'gdn_dt_bias': gdn_dt_bias[l],
              'gdn_norm_g': gdn_norm_g[l],
              'w_br': _bf(w_br[l]), 'w_gate': _bf(w_gate[l]), 'w_out': _bf(w_out[l]),
              'w_router': wr_pad, 'router_bias': rb_pad,
              'w_e1': _bf(w_e1[l]), 'w_e3': _bf(w_e3[l]), 'w_e2': _bf(w_e2[l])}
        mod = ada_mod(c_all, _bf(w_ada[l]), b_ada[l])
        mods = [mod[:, i * D_MODEL:(i + 1) * D_MODEL] for i in range(6)]
        mods_p = [m[:bsz].reshape(bsz, 1, D_MODEL) for m in mods]
        mods_s = [jnp.repeat(m[bsz:], dseq, axis=0).reshape(1, ms, D_MODEL) for m in mods]
        final = l == depth - 1

        def attend_p(qr, kr, v, vt, iqr, ikr, ik4, iwt, misc):
            return attn_prompt(qr, kr, vt, iqr, ik4, iwt, bsz, seq)

        def attend_s(qr, kr, v, vt, iqr, ikr, ik4, iwt, misc, l=l):
            return attn_sample(qr, kr, v, iqr, ik4, misc, ck, cv, cik, page_table, l, dseq)

        st_s = {'s5': jnp.concatenate([state_s5_re[l].reshape(dbs, S5_NS), state_s5_im[l].reshape(dbs, S5_NS)], axis=1),
                'ret': state_ret[l].reshape(dbs, BW, RET_DK), 'gdn': state_gdn[l].reshape(dbs, BW, GDN_DK),
                'conv': state_gdn_conv[l]}
        xp, ns_p = _trunk_layer(xp, mods_p, geom_p, attend_p, zero_st, dict(lw, **tabs_p), final_g, final)
        xs, ns_s = _trunk_layer(xs, mods_s, geom_s, attend_s, st_s, dict(lw, **tabs_s), final_g, final)
        outs_p.append(ns_p)
        outs_s.append(ns_s)

    def stack(outs, name, shape):
        return jnp.stack([o[name] for o in outs], axis=0).reshape((depth,) + shape)

    def states(outs, b, t):
        re = jnp.stack([o['s5'][:, :S5_NS] for o in outs], axis=0).reshape(depth, b, S5_GROUPS, S5_STATE)
        im = jnp.stack([o['s5'][:, S5_NS:] for o in outs], axis=0).reshape(depth, b, S5_GROUPS, S5_STATE)
        ret = stack(outs, 'ret', (b, RET_HEADS, RET_DK, RET_DK))
        gdn = stack(outs, 'gdn', (b, GDN_HEADS, GDN_DK, GDN_DK))
        return (stack(outs, 'k', (b, t, KV_HEADS, HEAD_DIM)), stack(outs, 'v', (b, t, KV_HEADS, HEAD_DIM)),
                stack(outs, 'ik', (b, t, IDX_DIM)), re, im, ret, gdn,
                stack(outs, 'conv', (b, GDN_CONV - 1, 3 * BW)))

    kp, vp, ikp, rep, imp, retp, gdnp, convp = states(outs_p, bsz, seq)
    ks_, vs_, iks, res, ims, rets, gdns, convs = states(outs_s, dbs, dseq)
    return (xp.reshape(bsz, seq, D_MODEL), xs.reshape(dbs, dseq, D_MODEL), kp, vp, ikp, ks_, vs_, iks,
            rep, imp, res, ims, retp, rets, gdnp, gdns, convp, convs)
```
